```python
import math
import jax
import jax.numpy as jnp
from jax import lax
import numpy as np

D_MODEL = 1024
BATCH = 16
SEQ = 256
DEPTH = 2
DEC_BATCH = 2
DEC_SEQ = 4096
PAST_LEN = 256

GRID_W = 64
HEAD_DIM = 64
D_SSM = D_MODEL // 2
SSM_GROUP_CH = 16
SSM_GROUPS = D_SSM // SSM_GROUP_CH
SSM_STATE = 64
N_HEADS_WIN = D_MODEL // (2 * HEAD_DIM)
N_KV_WIN = N_HEADS_WIN // 4
GRP_WIN = N_HEADS_WIN // N_KV_WIN
N_HEADS_GLB = D_MODEL // (2 * HEAD_DIM)
N_KV_GLB = N_HEADS_GLB // 4
GRP_GLB = N_HEADS_GLB // N_KV_GLB
D_WIN = N_HEADS_WIN * HEAD_DIM
D_KV_WIN = N_KV_WIN * HEAD_DIM
D_GLB = N_HEADS_GLB * HEAD_DIM
D_KV_GLB = N_KV_GLB * HEAD_DIM
WINDOW = 128
WIN_BLK = 128
N_SUB = 1 + 2 * WINDOW // WIN_BLK
Q_BLK = 128
ROPE_BASE = 10000.0
D_FF = 4 * D_MODEL
LN_EPS = 1e-5
RMS_EPS = 1e-6
ATTN_SCALE = HEAD_DIM ** -0.5
DEEPNORM_ALPHA = (2.0 * DEPTH) ** 0.25
DEEPNORM_BETA = (8.0 * DEPTH) ** -0.25
NEG_INF = -1e30
IN_SIZES = (D_SSM, D_WIN, D_KV_WIN, D_KV_WIN, D_GLB, D_KV_GLB, D_KV_GLB, D_MODEL, D_MODEL, D_MODEL)
IN_OFFSETS = tuple(int(v) for v in np.cumsum(IN_SIZES)[:-1])
N_IN = sum(IN_SIZES)

kernel_name = 'hybrid_diffusion_s5_window_global_step'


def _adaln(cond, w_mod, b_mod):
    m = jax.nn.silu(cond) @ w_mod + b_mod
    return [t[:, None, :] for t in jnp.split(m, 6, axis=-1)]


def _layer_norm(z, g, b):
    zf = z.astype(jnp.float32)
    mu = jnp.mean(zf, axis=-1, keepdims=True)
    var = jnp.mean(jnp.square(zf - mu), axis=-1, keepdims=True)
    out = (zf - mu) * lax.rsqrt(var + LN_EPS) * g.astype(jnp.float32) + b.astype(jnp.float32)
    return out.astype(z.dtype)


def _deepnorm_residual(x, f, g, b):
    return _layer_norm(DEEPNORM_ALPHA * x + f, g, b)


def _rms_norm(x, g):
    xf = x.astype(jnp.float32)
    out = xf * lax.rsqrt(jnp.mean(jnp.square(xf), axis=-1, keepdims=True) + RMS_EPS) * g.astype(jnp.float32)
    return out.astype(x.dtype)


def _axial_rope_tables(n_tok):
    rows = n_tok // GRID_W
    row = jnp.repeat(jnp.arange(rows, dtype=jnp.float32), GRID_W)
    col = jnp.tile(jnp.arange(GRID_W, dtype=jnp.float32), rows)
    n_freq = HEAD_DIM // 4
    inv = ROPE_BASE ** (-jnp.arange(n_freq, dtype=jnp.float32) / n_freq)
    ang = jnp.concatenate([row[:, None] * inv, col[:, None] * inv], axis=-1)
    return jnp.cos(ang), jnp.sin(ang)


def _rope(x, cos, sin):
    half = HEAD_DIM // 2
    shp = (1, cos.shape[0]) + (1,) * (x.ndim - 3) + (half,)
    c, s = cos.reshape(shp), sin.reshape(shp)
    xf = x.astype(jnp.float32)
    x1, x2 = xf[..., :half], xf[..., half:]
    return jnp.concatenate([x1 * c - x2 * s, x2 * c + x1 * s], axis=-1).astype(x.dtype)


def _ssm_combine(e1, e2):
    a1, b1 = e1
    a2, b2 = e2
    return a1 * a2, a2 * b1 + b2


def _s5_bidirectional(u, lam_re, lam_im, log_step, b_re, b_im, c_re, c_im, d_skip, w_glu, s0):
    f32 = jnp.float32
    bsz, n_tok, _ = u.shape
    uf = u.astype(f32).reshape(bsz, n_tok, SSM_GROUPS, SSM_GROUP_CH)
    lam = lax.complex(lam_re.astype(f32), lam_im.astype(f32))
    step = jnp.exp(log_step.astype(f32))[..., None]
    lam_bar = jnp.exp(lam * step)
    b_bar = ((lam_bar - 1.0) / lam)[..., None] * lax.complex(b_re.astype(f32), b_im.astype(f32))
    c_mat = lax.complex(c_re.astype(f32), c_im.astype(f32))
    y = d_skip.astype(f32).reshape(SSM_GROUPS, SSM_GROUP_CH) * uf
    finals = []
    for d, rev in ((0, False), (1, True)):
        bu = jnp.einsum('blgc,gpc->blgp', uf, b_bar[d])
        if s0 is not None:
            s0c = lax.complex(s0[:, d, 0].astype(f32), s0[:, d, 1].astype(f32))
            bu = bu.at[:, n_tok - 1 if rev else 0].add(lam_bar[d] * s0c)
        a = jnp.broadcast_to(lam_bar[d], bu.shape)
        _, states = lax.associative_scan(_ssm_combine, (a, bu), reverse=rev, axis=1)
        y = y + jnp.einsum('blgp,gcp->blgc', states, c_mat[d]).real
        finals.append(states[:, 0] if rev else states[:, -1])
    fin = jnp.stack(finals, axis=1)
    fin = jnp.stack([fin.real, fin.imag], axis=2).astype(u.dtype)
    y = jax.nn.gelu(y.reshape(bsz, n_tok, D_SSM)).astype(u.dtype)
    y = y * jax.nn.sigmoid(y @ w_glu)
    return y, fin


def _blocked_attention(q, k, v, sink):
    bsz, n_q, hkv, grp, hd = q.shape
    n_k = k.shape[1]
    qb = jnp.moveaxis(q.reshape(bsz, n_q // Q_BLK, Q_BLK, hkv, grp, hd), 1, 0)

    def one_block(qq):
        s = jnp.einsum('bqhgd,bkhd->bhgqk', qq, k).astype(jnp.float32) * ATTN_SCALE
        if sink is not None:
            s_sink = jnp.broadcast_to(sink.astype(jnp.float32)[None, :, :, None, None], s.shape[:-1] + (1,))
            s = jnp.concatenate([s, s_sink], axis=-1)
        p = jax.nn.softmax(s, axis=-1)[..., :n_k]
        return jnp.einsum('bhgqk,bkhd->bqhgd', p.astype(v.dtype), v)

    out = lax.map(one_block, qb)
    return jnp.moveaxis(out, 0, 1).reshape(bsz, n_q, hkv, grp, hd)


def _window_attention(q, k, v, k_ctx, v_ctx, sink):
    bsz, n_tok, hkv, grp, hd = q.shape
    nb = n_tok // WIN_BLK
    nk = N_SUB * WIN_BLK
    qb = q.reshape(bsz, nb, WIN_BLK, hkv, grp, hd)
    pad = ((0, 0), (WINDOW, WINDOW), (0, 0), (0, 0))
    kp, vp = jnp.pad(k, pad), jnp.pad(v, pad)
    kb = jnp.concatenate([kp[:, j * WIN_BLK:j * WIN_BLK + n_tok].reshape(bsz, nb, WIN_BLK, hkv, hd) for j in range(N_SUB)], axis=2)
    vb = jnp.concatenate([vp[:, j * WIN_BLK:j * WIN_BLK + n_tok].reshape(bsz, nb, WIN_BLK, hkv, hd) for j in range(N_SUB)], axis=2)
    blk = jnp.arange(nb)[:, None, None] * WIN_BLK
    qpos = blk + jnp.arange(WIN_BLK)[None, :, None]
    kpos = blk - WINDOW + jnp.arange(nk)[None, None, :]
    mask = (jnp.abs(kpos - qpos) <= WINDOW) & (kpos >= 0) & (kpos < n_tok)
    s_band = jnp.einsum('bnqhgd,bnkhd->bnhgqk', qb, kb).astype(jnp.float32) * ATTN_SCALE
    s_band = jnp.where(mask[None, :, None, None], s_band, NEG_INF)
    s_ctx = jnp.einsum('bnqhgd,bkhd->bnhgqk', qb, k_ctx).astype(jnp.float32) * ATTN_SCALE
    s_sink = jnp.broadcast_to(sink.astype(jnp.float32)[None, None, :, :, None, None], s_band.shape[:-1] + (1,))
    p = jax.nn.softmax(jnp.concatenate([s_band, s_ctx, s_sink], axis=-1), axis=-1)
    n_ctx = k_ctx.shape[1]
    out = (jnp.einsum('bnhgqk,bnkhd->bnqhgd', p[..., :nk].astype(v.dtype), vb)
           + jnp.einsum('bnhgqk,bkhd->bnqhgd', p[..., nk:nk + n_ctx].astype(v.dtype), v_ctx))
    return out.reshape(bsz, n_tok, hkv, grp, hd)


def _merge_branches(ya, yw, yg, ga, gw, gg, w_br_ssm, w_br_win, w_br_glb, w_out):
    m = (jax.nn.sigmoid(ga) * (ya @ w_br_ssm) + jax.nn.sigmoid(gw) * (yw @ w_br_win)
         + jax.nn.sigmoid(gg) * (yg @ w_br_glb))
    return m @ w_out


def _sq_relu_mlp(h, w_up, w_down):
    return jnp.square(jax.nn.relu(h @ w_up)) @ w_down


def _layer(x, cond, lp, ctx=None, rope=None):
    bsz, n_tok, _ = x.shape
    sh1, sc1, g1, sh2, sc2, g2 = _adaln(cond, lp['w_mod'], lp['b_mod'])
    h = x * (1.0 + sc1) + sh1
    u, qw, kw, vw, qg, kg, vg, ga, gw, gg = jnp.split(h @ lp['w_in'], IN_OFFSETS, axis=-1)
    qw = qw.reshape(bsz, n_tok, N_KV_WIN, GRP_WIN, HEAD_DIM)
    kw = kw.reshape(bsz, n_tok, N_KV_WIN, HEAD_DIM)
    vw = vw.reshape(bsz, n_tok, N_KV_WIN, HEAD_DIM)
    qg = _rms_norm(qg.reshape(bsz, n_tok, N_KV_GLB, GRP_GLB, HEAD_DIM), lp['q_norm'])
    kg = _rms_norm(kg.reshape(bsz, n_tok, N_KV_GLB, HEAD_DIM), lp['k_norm'])
    vg = vg.reshape(bsz, n_tok, N_KV_GLB, HEAD_DIM)
    s0 = None if ctx is None else ctx[0]
    ya, s_fin = _s5_bidirectional(u, lp['lam_re'], lp['lam_im'], lp['log_step'], lp['b_re'], lp['b_im'],
                                  lp['c_re'], lp['c_im'], lp['d_skip'], lp['w_glu'], s0)
    if ctx is None:
        yw = _blocked_attention(qw, kw, vw, lp['sink'])
        yg = _blocked_attention(qg, kg, vg, None)
        new_ctx = (s_fin, kw, vw, kg, vg)
    else:
        _, k_wc, v_wc, k_gc, v_gc = ctx
        cos, sin = rope
        qw, kw, qg, kg = (_rope(t, cos, sin) for t in (qw, kw, qg, kg))
        yw = _window_attention(qw, kw, vw, k_wc, v_wc, lp['sink'])
        yg = _blocked_attention(qg, jnp.concatenate([kg, k_gc], axis=1), jnp.concatenate([vg, v_gc], axis=1), None)
        new_ctx = None
    f = _merge_branches(ya, yw.reshape(bsz, n_tok, D_WIN), yg.reshape(bsz, n_tok, D_GLB), ga, gw, gg,
                        lp['w_br_ssm'], lp['w_br_win'], lp['w_br_glb'], lp['w_out'])
    x = _deepnorm_residual(x, g1 * f, lp['ln1_g'], lp['ln1_b'])
    h2 = x * (1.0 + sc2) + sh2
    x = _deepnorm_residual(x, g2 * _sq_relu_mlp(h2, lp['w_up'], lp['w_down']), lp['ln2_g'], lp['ln2_b'])
    return x, new_ctx


def setup_inputs(seed: int = 0) -> dict:
    key = jax.random.key(seed)
    ks = iter(jax.random.split(key, 40))
    f32 = jnp.float32

    def nrm(shape, scale):
        return jax.random.normal(next(ks), shape, f32) * scale

    lam_n = jnp.pi * jnp.arange(SSM_STATE, dtype=f32)
    return {
        'x_prompt': nrm((BATCH, SEQ, D_MODEL), 1.0),
        'x_sample': nrm((DEC_BATCH, DEC_SEQ, D_MODEL), 1.0),
        'state_ssm': nrm((DEC_BATCH, DEPTH, 2, 2, SSM_GROUPS, SSM_STATE), 0.3),
        'cache_k_win': nrm((DEC_BATCH, DEPTH, PAST_LEN, N_KV_WIN, HEAD_DIM), 1.0),
        'cache_v_win': nrm((DEC_BATCH, DEPTH, PAST_LEN, N_KV_WIN, HEAD_DIM), 1.0),
        'cache_k_glb': nrm((DEC_BATCH, DEPTH, PAST_LEN, N_KV_GLB, HEAD_DIM), 1.0),
        'cache_v_glb': nrm((DEC_BATCH, DEPTH, PAST_LEN, N_KV_GLB, HEAD_DIM), 1.0),
        'c': nrm((DEC_BATCH, D_MODEL), 1.0),
        'c_ctx': nrm((D_MODEL,), 1.0),
        'w_mod': nrm((DEPTH, D_MODEL, 6 * D_MODEL), D_MODEL ** -0.5),
        'b_mod': nrm((DEPTH, 6 * D_MODEL), 0.02),
        'w_in': nrm((DEPTH, D_MODEL, N_IN), D_MODEL ** -0.5),
        'ssm_lam_re': -0.5 + nrm((DEPTH, 2, SSM_GROUPS, SSM_STATE), 0.01),
        'ssm_lam_im': lam_n + nrm((DEPTH, 2, SSM_GROUPS, SSM_STATE), 0.01),
        'ssm_log_step': jax.random.uniform(next(ks), (DEPTH, 2, SSM_GROUPS), f32, math.log(1e-3), math.log(1e-1)),
        'ssm_b_re': nrm((DEPTH, 2, SSM_GROUPS, SSM_STATE, SSM_GROUP_CH), (2 * SSM_GROUP_CH) ** -0.5),
        'ssm_b_im': nrm((DEPTH, 2, SSM_GROUPS, SSM_STATE, SSM_GROUP_CH), (2 * SSM_GROUP_CH) ** -0.5),
        'ssm_c_re': nrm((DEPTH, 2, SSM_GROUPS, SSM_GROUP_CH, SSM_STATE), SSM_STATE ** -0.5),
        'ssm_c_im': nrm((DEPTH, 2, SSM_GROUPS, SSM_GROUP_CH, SSM_STATE), SSM_STATE ** -0.5),
        'ssm_d': nrm((DEPTH, D_SSM), 1.0),
        'w_glu': nrm((DEPTH, D_SSM, D_SSM), D_SSM ** -0.5),
        'sink_win': nrm((DEPTH, N_HEADS_WIN), 0.5),
        'q_norm_glb': 1.0 + nrm((DEPTH, HEAD_DIM), 0.02),
        'k_norm_glb': 1.0 + nrm((DEPTH, HEAD_DIM), 0.02),
        'w_br_ssm': nrm((DEPTH, D_SSM, D_MODEL), D_SSM ** -0.5),
        'w_br_win': nrm((DEPTH, D_WIN, D_MODEL), D_WIN ** -0.5),
        'w_br_glb': nrm((DEPTH, D_GLB, D_MODEL), D_GLB ** -0.5),
        'w_out': nrm((DEPTH, D_MODEL, D_MODEL), D_MODEL ** -0.5 * DEEPNORM_BETA),
        'ln1_g': 1.0 + nrm((DEPTH, D_MODEL), 0.02),
        'ln1_b': nrm((DEPTH, D_MODEL), 0.02),
        'w_up': nrm((DEPTH, D_MODEL, D_FF), D_MODEL ** -0.5),
        'w_down': nrm((DEPTH, D_FF, D_MODEL), D_FF ** -0.5 * DEEPNORM_BETA),
        'ln2_g': 1.0 + nrm((DEPTH, D_MODEL), 0.02),
        'ln2_b': nrm((DEPTH, D_MODEL), 0.02),
    }


def reference(x_prompt, x_sample, state_ssm, cache_k_win, cache_v_win, cache_k_glb, cache_v_glb, c, c_ctx,
              w_mod, b_mod, w_in, ssm_lam_re, ssm_lam_im, ssm_log_step, ssm_b_re, ssm_b_im, ssm_c_re, ssm_c_im,
              ssm_d, w_glu, sink_win, q_norm_glb, k_norm_glb, w_br_ssm, w_br_win, w_br_glb, w_out,
              ln1_g, ln1_b, w_up, w_down, ln2_g, ln2_b):
    rope = _axial_rope_tables(x_sample.shape[1])
    xp, xs = x_prompt, x_sample
    new_ssm, new_kw, new_vw, new_kg, new_vg = [], [], [], [], []
    for l in range(DEPTH):
        lp = dict(w_mod=w_mod[l], b_mod=b_mod[l], w_in=w_in[l],
                  lam_re=ssm_lam_re[l], lam_im=ssm_lam_im[l], log_step=ssm_log_step[l],
                  b_re=ssm_b_re[l], b_im=ssm_b_im[l], c_re=ssm_c_re[l], c_im=ssm_c_im[l],
                  d_skip=ssm_d[l], w_glu=w_glu[l], sink=sink_win[l].reshape(N_KV_WIN, GRP_WIN),
                  q_norm=q_norm_glb[l], k_norm=k_norm_glb[l],
                  w_br_ssm=w_br_ssm[l], w_br_win=w_br_win[l], w_br_glb=w_br_glb[l], w_out=w_out[l],
                  ln1_g=ln1_g[l], ln1_b=ln1_b[l], w_up=w_up[l], w_down=w_down[l], ln2_g=ln2_g[l], ln2_b=ln2_b[l])
        xp, (s_fin, kw, vw, kg, vg) = _layer(xp, c_ctx[None, :], lp)
        new_ssm.append(s_fin)
        new_kw.append(kw)
        new_vw.append(vw)
        new_kg.append(kg)
        new_vg.append(vg)
        ctx = (state_ssm[:, l], cache_k_win[:, l], cache_v_win[:, l], cache_k_glb[:, l], cache_v_glb[:, l])
        xs, _ = _layer(xs, c, lp, ctx, rope)
    new_state_ssm = jnp.stack(new_ssm, axis=1)
    new_cache_k_win = jnp.stack(new_kw, axis=1)
    new_cache_v_win = jnp.stack(new_vw, axis=1)
    new_cache_k_glb = jnp.stack(new_kg, axis=1)
    new_cache_v_glb = jnp.stack(new_vg, axis=1)
    return (xp, xs, new_state_ssm, new_cache_k_win, new_cache_v_win, new_cache_k_glb, new_cache_v_glb)
```

```python
import functools

import jax
import jax.numpy as jnp
import numpy as np
from jax import lax
from jax.experimental import pallas as pl
from jax.experimental.pallas import tpu as pltpu

F32 = jnp.float32
BF16 = jnp.bfloat16

D_MODEL = 1024
DEPTH = 2
GRID_W = 64
HEAD_DIM = 64
D_SSM = 512
SSM_GROUP_CH = 16
SSM_GROUPS = 32
SSM_STATE = 64
N_HEADS = 8
N_KV = 2
GRP = N_HEADS // N_KV
D_ATT = N_HEADS * HEAD_DIM
D_KV = N_KV * HEAD_DIM
WINDOW = 128
ROPE_BASE = 10000.0
D_FF = 4 * D_MODEL
LN_EPS = 1e-5
RMS_EPS = 1e-6
ATTN_SCALE = HEAD_DIM ** -0.5
DEEPNORM_ALPHA = (2.0 * DEPTH) ** 0.25
NEG_INF = -1e30
N_IN = D_SSM + 2 * (D_ATT + 2 * D_KV) + 3 * D_MODEL
O_U = 0
O_QW = O_U + D_SSM
O_KW = O_QW + D_ATT
O_VW = O_KW + D_KV
O_QG = O_VW + D_KV
O_KG = O_QG + D_ATT
O_VG = O_KG + D_KV
O_GATE = O_VG + D_KV

S5_CHUNK = 16
S5_PAIR = 2 * SSM_GROUP_CH
S5_NPAIR = SSM_GROUPS // 2
S5_ROW = S5_CHUNK * S5_PAIR
S5_PSTATE = 2 * SSM_STATE
N_STATE = SSM_GROUPS * SSM_STATE

LANES = 128
ROW_TILE = 512
VMEM_LIMIT = 56 * 1024 * 1024


def _cparams(n_axes):
    return pltpu.CompilerParams(dimension_semantics=("arbitrary",) * n_axes, vmem_limit_bytes=VMEM_LIMIT)


def _resident(shape):
    nd = len(shape)
    return pl.BlockSpec(shape, lambda *_: (0,) * nd, pipeline_mode=pl.Buffered(1))


def _dot(a, b):
    return jnp.dot(a, b, preferred_element_type=F32)


def _mod_kernel(c_ref, w_ref, b_ref, o_ref):
    c = c_ref[...]
    a = (c * jax.nn.sigmoid(c)).astype(BF16)
    o_ref[0] = _dot(a, w_ref[0].astype(BF16)) + b_ref[0]


def _modulation(cond8, w_mod, b_mod):
    tn = 512
    n_out = w_mod.shape[-1]
    return pl.pallas_call(
        _mod_kernel,
        grid=(DEPTH, n_out // tn),
        in_specs=[
            pl.BlockSpec((8, D_MODEL), lambda l, n: (0, 0)),
            pl.BlockSpec((1, D_MODEL, tn), lambda l, n: (l, 0, n)),
            pl.BlockSpec((1, 1, tn), lambda l, n: (l, 0, n)),
        ],
        out_specs=pl.BlockSpec((1, 8, tn), lambda l, n: (l, 0, n)),
        out_shape=jax.ShapeDtypeStruct((DEPTH, 8, n_out), F32),
        compiler_params=_cparams(2),
        name="modulation",
    )(cond8, w_mod, b_mod.reshape(DEPTH, 1, n_out))


def _mod_spec(chunk, row_of_tile):
    return pl.BlockSpec((1, 1, D_MODEL), lambda i: (row_of_tile(i), 0, chunk))


def _head_rms(x, gain, ones_bd):
    sq = x * x
    hi = sq.astype(BF16)
    lo = (sq - hi.astype(F32)).astype(BF16)
    ss = _dot(hi, ones_bd) + _dot(lo, ones_bd)
    return x * lax.rsqrt(ss * (1.0 / HEAD_DIM) + RMS_EPS) * gain


def _rope_chunk(xc, cos_t, sin_t):
    lane = lax.broadcasted_iota(jnp.int32, xc.shape, 1)
    first_half = (lane & (HEAD_DIM - 1)) < (HEAD_DIM // 2)
    partner = jnp.where(first_half, pltpu.roll(xc, LANES - HEAD_DIM // 2, 1), pltpu.roll(xc, HEAD_DIM // 2, 1))
    return xc * cos_t + partner * sin_t


def _inproj_kernel(*refs, rope):
    if rope:
        (x_ref, sh_ref, sc_ref, w_ref, qn_ref, kn_ref, bdq_ref, bdk_ref, cos_ref, sin_ref,
         u_ref, qw_ref, kw_ref, vw_ref, qg_ref, kg_ref, vg_ref, gate_ref) = refs
    else:
        (x_ref, sh_ref, sc_ref, w_ref, qn_ref, kn_ref, bdq_ref, bdk_ref,
         u_ref, qw_ref, kw_ref, vw_ref, qg_ref, kg_ref, vg_ref, gate_ref) = refs
    h = (x_ref[...] * (1.0 + sc_ref[0]) + sh_ref[0]).astype(BF16)

    def proj(lo, width):
        return _dot(h, w_ref[:, lo:lo + width])

    def finish(x, out_ref, scale):
        n = x.shape[-1]
        for c in range(n // LANES):
            xc = x[:, c * LANES:(c + 1) * LANES]
            if rope:
                xc = _rope_chunk(xc, cos_ref[...], sin_ref[...])
            if scale != 1.0:
                xc = xc * scale
            out_ref[:, c * LANES:(c + 1) * LANES] = xc.astype(out_ref.dtype)

    u_ref[...] = proj(O_U, D_SSM)
    finish(proj(O_QW, D_ATT), qw_ref, ATTN_SCALE)
    finish(proj(O_KW, D_KV), kw_ref, 1.0)
    vw_ref[...] = proj(O_VW, D_KV)
    finish(_head_rms(proj(O_QG, D_ATT), qn_ref[...], bdq_ref[...]), qg_ref, ATTN_SCALE)
    finish(_head_rms(proj(O_KG, D_KV), kn_ref[...], bdk_ref[...]), kg_ref, 1.0)
    vg_ref[...] = proj(O_VG, D_KV)
    for c in range(3):
        gate_ref[:, c * D_MODEL:(c + 1) * D_MODEL] = proj(O_GATE + c * D_MODEL, D_MODEL).astype(gate_ref.dtype)


def _in_projection(x, mod_l, row_of_tile, w_in, qn, kn, bdq, bdk, rope_tabs, seq_len):
    n_tok = x.shape[0]
    tm = ROW_TILE
    rope = rope_tabs is not None
    row = lambda w: pl.BlockSpec((tm, w), lambda i: (i, 0))
    in_specs = [
        row(D_MODEL),
        _mod_spec(0, row_of_tile), _mod_spec(1, row_of_tile),
        _resident((D_MODEL, N_IN)),
        _resident((1, D_ATT)), _resident((1, D_KV)),
        _resident((D_ATT, D_ATT)), _resident((D_KV, D_KV)),
    ]
    args = [x, mod_l, mod_l, w_in, qn, kn, bdq, bdk]
    if rope:
        tiles_per_seq = seq_len // tm
        tab = pl.BlockSpec((tm, LANES), lambda i: (i % tiles_per_seq, 0))
        in_specs += [tab, tab]
        args += list(rope_tabs)
    out_shape = [
        jax.ShapeDtypeStruct((n_tok, D_SSM), F32),
        jax.ShapeDtypeStruct((n_tok, D_ATT), BF16),
        jax.ShapeDtypeStruct((n_tok, D_KV), F32),
        jax.ShapeDtypeStruct((n_tok, D_KV), F32),
        jax.ShapeDtypeStruct((n_tok, D_ATT), BF16),
        jax.ShapeDtypeStruct((n_tok, D_KV), F32),
        jax.ShapeDtypeStruct((n_tok, D_KV), F32),
        jax.ShapeDtypeStruct((n_tok, 3 * D_MODEL), BF16),
    ]
    out_specs = [row(s.shape[1]) for s in out_shape]
    return pl.pallas_call(
        functools.partial(_inproj_kernel, rope=rope),
        grid=(n_tok // tm,),
        in_specs=in_specs,
        out_specs=out_specs,
        out_shape=out_shape,
        compiler_params=_cparams(1),
        name="in_projection",
    )(*args)


def _s5_param_kernel(lr_ref, li_ref, ls_ref, br_ref, bi_ref, cr_ref, ci_ref,
                     pw_ref, eb_ref, ca_ref, k_ref):
    lr, li = lr_ref[0, 0], li_ref[0, 0]
    dt = jnp.exp(ls_ref[0, 0])
    mag = jnp.exp(lr * dt)
    ar, ai = mag * jnp.cos(li * dt), mag * jnp.sin(li * dt)
    den = lr * lr + li * li
    fr = ((ar - 1.0) * lr + ai * li) / den
    fi = (ai * lr - (ar - 1.0) * li) / den
    br, bi = br_ref[0, 0], bi_ref[0, 0]
    bbr, bbi = fr * br - fi * bi, fr * bi + fi * br
    cr, ci = cr_ref[0, 0], ci_ref[0, 0]
    nt = (((1,), (1,)), ((), ()))
    pr, pi = jnp.ones_like(ar), jnp.zeros_like(ar)
    for j in range(S5_CHUNK + 1):
        pw_ref[0, 0, 0, j:j + 1, :] = pr
        pw_ref[0, 0, 1, j:j + 1, :] = pi
        car, cai = cr * pr - ci * pi, cr * pi + ci * pr
        if j < S5_CHUNK:
            eb_ref[0, 0, 0, j] = pr * bbr - pi * bbi
            eb_ref[0, 0, 1, j] = pr * bbi + pi * bbr
            k_ref[0, 0, j] = (lax.dot_general(car, bbr, nt, precision=lax.Precision.HIGHEST, preferred_element_type=F32)
                              - lax.dot_general(cai, bbi, nt, precision=lax.Precision.HIGHEST, preferred_element_type=F32))
        if j >= 1:
            ca_ref[0, 0, 0, j - 1] = car
            ca_ref[0, 0, 1, j - 1] = cai
        pr, pi = pr * ar - pi * ai, pr * ai + pi * ar


def _s5_params(lam_re, lam_im, log_step, b_re, b_im, c_re, c_im):
    g, p, cg = SSM_GROUPS, SSM_STATE, SSM_GROUP_CH
    vec = pl.BlockSpec((1, 1, 1, p), lambda d, i: (d, i, 0, 0))
    mat = pl.BlockSpec((1, 1, cg, p), lambda d, i: (d, i, 0, 0))
    ls = jnp.broadcast_to(log_step[:, :, None, None], (2, g, 1, p))
    return pl.pallas_call(
        _s5_param_kernel,
        grid=(2, g),
        in_specs=[vec, vec, vec, mat, mat, mat, mat],
        out_specs=[
            pl.BlockSpec((1, 1, 2, S5_CHUNK + 1, p), lambda d, i: (d, i, 0, 0, 0)),
            pl.BlockSpec((1, 1, 2, S5_CHUNK, cg, p), lambda d, i: (d, i, 0, 0, 0, 0)),
            pl.BlockSpec((1, 1, 2, S5_CHUNK, cg, p), lambda d, i: (d, i, 0, 0, 0, 0)),
            pl.BlockSpec((1, 1, S5_CHUNK, cg, cg), lambda d, i: (d, i, 0, 0, 0)),
        ],
        out_shape=[
            jax.ShapeDtypeStruct((2, g, 2, S5_CHUNK + 1, p), F32),
            jax.ShapeDtypeStruct((2, g, 2, S5_CHUNK, cg, p), F32),
            jax.ShapeDtypeStruct((2, g, 2, S5_CHUNK, cg, p), F32),
            jax.ShapeDtypeStruct((2, g, S5_CHUNK, cg, cg), F32),
        ],
        compiler_params=_cparams(2),
        name="s5_params",
    )(lam_re.reshape(2, g, 1, p), lam_im.reshape(2, g, 1, p), ls,
      jnp.swapaxes(b_re, -1, -2), jnp.swapaxes(b_im, -1, -2), c_re, c_im)


def _s5_operators(pw, eb, ca, kk):
    g, p, cg, tc = SSM_GROUPS, SSM_STATE, SSM_GROUP_CH, S5_CHUNK
    eye2 = jnp.eye(2, dtype=F32)
    t = jnp.arange(tc)
    lag = t[None, :] - t[:, None]
    kf = jnp.where((lag >= 0)[None, :, :, None, None], kk[0][:, jnp.clip(lag, 0, tc - 1)], 0.0)
    kb = jnp.where((lag <= 0)[None, :, :, None, None], kk[1][:, jnp.clip(-lag, 0, tc - 1)], 0.0)
    ktot = (kf + kb).reshape(S5_NPAIR, 2, tc, tc, cg, cg)
    top = jnp.einsum('ngtsoc,gh->ntgcsho', ktot, eye2).reshape(S5_NPAIR, S5_ROW, S5_ROW).astype(BF16)

    def end_op(d, part):
        e = eb[d, :, part]
        e = e[:, ::-1] if d == 0 else e
        e = e.reshape(S5_NPAIR, 2, tc, cg, p)
        return jnp.einsum('ngtcp,gh->ntgchp', e, eye2).reshape(S5_NPAIR, S5_ROW, S5_PSTATE).astype(BF16)

    def carry_op(d, part):
        m = ca[d, :, part]
        m = m if d == 0 else m[:, ::-1]
        m = m if part == 0 else -m
        m = m.reshape(S5_NPAIR, 2, tc, cg, p)
        return jnp.einsum('ngtop,gh->ngptho', m, eye2).reshape(S5_NPAIR, S5_PSTATE, S5_ROW).astype(BF16)

    ends = [end_op(d, part) for d in (0, 1) for part in (0, 1)]
    carries = [carry_op(d, part) for d in (0, 1) for part in (0, 1)]
    decay = [pw[d, :, part, tc].reshape(1, N_STATE) for d in (0, 1) for part in (0, 1)]
    return top, ends, carries, decay


def _s5_ends_kernel(u_ref, e0, e1, e2, e3, o0, o1, o2, o3):
    u = u_ref[0].astype(BF16)
    for e_ref, o_ref in ((e0, o0), (e1, o1), (e2, o2), (e3, o3)):
        o_ref[...] = _dot(u, e_ref[0])


def _s5_ends(u_rows, ends, rt):
    n_rows = u_rows.shape[1]
    e_spec = pl.BlockSpec((1, S5_ROW, S5_PSTATE), lambda n, r: (n, 0, 0))
    o_spec = pl.BlockSpec((rt, S5_PSTATE), lambda n, r: (r, n))
    return pl.pallas_call(
        _s5_ends_kernel,
        grid=(S5_NPAIR, n_rows // rt),
        in_specs=[pl.BlockSpec((1, rt, S5_ROW), lambda n, r: (n, r, 0))] + [e_spec] * 4,
        out_specs=[o_spec] * 4,
        out_shape=[jax.ShapeDtypeStruct((n_rows, N_STATE), F32)] * 4,
        compiler_params=_cparams(2),
        name="s5_chunk_ends",
    )(u_rows, *ends)


def _s5_scan_kernel(efr, efi, ebr, ebi, afr, afi, abr, abi, sfr, sfi, sbr, sbi,
                    pfr, pfi, nbr, nbi, ofr, ofi, obr, obi):
    nc = efr.shape[0]
    a_fr, a_fi, a_br, a_bi = afr[...], afi[...], abr[...], abi[...]

    def body(c, carry):
        fr, fi, br, bi = carry
        cb = nc - 1 - c
        pfr[c] = fr
        pfi[c] = fi
        nbr[cb] = br
        nbi[cb] = bi
        nfr = a_fr * fr - a_fi * fi + efr[c]
        nfi = a_fr * fi + a_fi * fr + efi[c]
        nbr_ = a_br * br - a_bi * bi + ebr[cb]
        nbi_ = a_br * bi + a_bi * br + ebi[cb]
        return nfr, nfi, nbr_, nbi_

    fr, fi, br, bi = lax.fori_loop(0, nc, body, (sfr[...], sfi[...], sbr[...], sbi[...]))
    ofr[...] = fr
    ofi[...] = fi
    obr[...] = br
    obi[...] = bi


def _s5_scan(se, decay, s0, nc, nb):
    cw = 256
    seq = pl.BlockSpec((nc, nb, cw), lambda i: (0, 0, i))
    vec = pl.BlockSpec((1, cw), lambda i: (0, i))
    st = pl.BlockSpec((nb, cw), lambda i: (0, i))
    se3 = [s.reshape(nc, nb, N_STATE) for s in se]
    return pl.pallas_call(
        _s5_scan_kernel,
        grid=(N_STATE // cw,),
        in_specs=[seq] * 4 + [vec] * 4 + [st] * 4,
        out_specs=[seq] * 4 + [st] * 4,
        out_shape=[jax.ShapeDtypeStruct((nc, nb, N_STATE), F32)] * 4 + [jax.ShapeDtypeStruct((nb, N_STATE), F32)] * 4,
        compiler_params=_cparams(1),
        name="s5_chunk_scan",
    )(*se3, *decay, *s0)


def _s5_out_kernel(u_ref, top_ref, d_ref, pfr, pfi, nbr, nbi, m0, m1, m2, m3, o_ref):
    u = u_ref[0]
    y = _dot(u.astype(BF16), top_ref[0]) + d_ref[0] * u
    for s_ref, m_ref in ((pfr, m0), (pfi, m1), (nbr, m2), (nbi, m3)):
        y = y + _dot(s_ref[...].astype(BF16), m_ref[0])
    o_ref[0] = jax.nn.gelu(y)


def _s5_out(u_rows, top, d_rows, states, carries, rt):
    n_rows = u_rows.shape[1]
    s_spec = pl.BlockSpec((rt, S5_PSTATE), lambda n, r: (r, n))
    m_spec = pl.BlockSpec((1, S5_PSTATE, S5_ROW), lambda n, r: (n, 0, 0))
    return pl.pallas_call(
        _s5_out_kernel,
        grid=(S5_NPAIR, n_rows // rt),
        in_specs=[pl.BlockSpec((1, rt, S5_ROW), lambda n, r: (n, r, 0)),
                  pl.BlockSpec((1, S5_ROW, S5_ROW), lambda n, r: (n, 0, 0)),
                  pl.BlockSpec((1, 1, S5_ROW), lambda n, r: (n, 0, 0))] + [s_spec] * 4 + [m_spec] * 4,
        out_specs=pl.BlockSpec((1, rt, S5_ROW), lambda n, r: (n, r, 0)),
        out_shape=jax.ShapeDtypeStruct(u_rows.shape, F32),
        compiler_params=_cparams(2),
        name="s5_outputs",
    )(u_rows, top, d_rows, *states, *carries)


def _s5_branch(u, ops, d_skip, s0, nb, seq_len):
    top, ends, carries, decay = ops
    nc = seq_len // S5_CHUNK
    n_rows = nc * nb
    rt = min(n_rows, 256)
    u_rows = u.reshape(nb, nc, S5_CHUNK, S5_NPAIR, S5_PAIR).transpose(3, 1, 0, 2, 4).reshape(S5_NPAIR, n_rows, S5_ROW)
    d_rows = jnp.tile(d_skip.reshape(S5_NPAIR, 1, S5_PAIR), (1, S5_CHUNK, 1)).reshape(S5_NPAIR, 1, S5_ROW)
    se = _s5_ends(u_rows, ends, rt)
    scanned = _s5_scan(se, decay, s0, nc, nb)
    states = [s.reshape(n_rows, N_STATE) for s in scanned[:4]]
    y_rows = _s5_out(u_rows, top, d_rows, states, carries, rt)
    y = y_rows.reshape(S5_NPAIR, nc, nb, S5_CHUNK, S5_PAIR).transpose(2, 1, 3, 0, 4).reshape(nb * seq_len, D_SSM)
    return y, scanned[4:]


def _sink_rows(sink_ref, h, tq):
    return jnp.concatenate([jnp.full((tq, 1), sink_ref[h * GRP + j], F32) for j in range(GRP)], axis=0)


def _attn_full_kernel(*refs, tq, tk, use_sink):
    if use_sink:
        sink_ref, q_ref, k_ref, v_ref, o_ref, m_ref, l_ref, acc_ref = refs
    else:
        q_ref, k_ref, v_ref, o_ref, m_ref, l_ref, acc_ref = refs
    q = q_ref[0].reshape(GRP * tq, HEAD_DIM)
    n_chunks = k_ref.shape[2] // tk
    if use_sink:
        m_ref[...] = _sink_rows(sink_ref, pl.program_id(1), tq)
        l_ref[...] = jnp.ones(l_ref.shape, F32)
    else:
        m_ref[...] = jnp.full(m_ref.shape, NEG_INF, F32)
        l_ref[...] = jnp.zeros(l_ref.shape, F32)
    acc_ref[...] = jnp.zeros(acc_ref.shape, F32)

    def body(c, carry):
        start = pl.multiple_of(c * tk, tk)
        kc = k_ref[0, 0, pl.ds(start, tk), :]
        vc = v_ref[0, 0, pl.ds(start, tk), :]
        s = lax.dot_general(q, kc, (((1,), (1,)), ((), ())), preferred_element_type=F32)
        m_old = m_ref[...]
        m_new = jnp.maximum(m_old, jnp.max(s, axis=-1, keepdims=True))
        p = jnp.exp(s - m_new)
        alpha = jnp.exp(m_old - m_new)
        l_ref[...] = alpha * l_ref[...] + jnp.sum(p, axis=-1, keepdims=True)
        acc_ref[...] = alpha * acc_ref[...] + _dot(p.astype(BF16), vc)
        m_ref[...] = m_new
        return carry

    lax.fori_loop(0, n_chunks, body, 0)
    o_ref[0] = (acc_ref[...] / l_ref[...]).reshape(GRP, tq, HEAD_DIM).astype(o_ref.dtype)


def _attention_full(q, k, v, sink, tq, tk):
    nb, _, n_q, _ = q.shape
    n_k = k.shape[2]
    use_sink = sink is not None
    q_spec = pl.BlockSpec((1, GRP, tq, HEAD_DIM), lambda b, h, i: (b, h, i, 0))
    kv_spec = pl.BlockSpec((1, 1, n_k, HEAD_DIM), lambda b, h, i: (b, h, 0, 0))
    in_specs = [q_spec, kv_spec, kv_spec]
    args = [q, k, v]
    if use_sink:
        in_specs = [pl.BlockSpec(memory_space=pltpu.SMEM)] + in_specs
        args = [sink] + args
    rows = GRP * tq
    return pl.pallas_call(
        functools.partial(_attn_full_kernel, tq=tq, tk=tk, use_sink=use_sink),
        grid=(nb, N_KV, n_q // tq),
        in_specs=in_specs,
        out_specs=q_spec,
        out_shape=jax.ShapeDtypeStruct(q.shape, BF16),
        scratch_shapes=[pltpu.VMEM((rows, 1), F32), pltpu.VMEM((rows, 1), F32), pltpu.VMEM((rows, HEAD_DIM), F32)],
        compiler_params=_cparams(3),
        name="attention_full",
    )(*args)


def _attn_band_kernel(sink_ref, q_ref, k_ref, v_ref, o_ref, *, tq, n_lat, n_ctx):
    i = pl.program_id(2)
    q = q_ref[0].reshape(GRP * tq, HEAD_DIM)
    span = tq + 2 * WINDOW
    start = pl.multiple_of(jnp.clip(i * tq - WINDOW, 0, n_lat - span), WINDOW)
    nt = (((1,), (1,)), ((), ()))
    s_band = lax.dot_general(q, k_ref[0, 0, pl.ds(start, span), :], nt, preferred_element_type=F32)
    row = lax.broadcasted_iota(jnp.int32, s_band.shape, 0)
    col = lax.broadcasted_iota(jnp.int32, s_band.shape, 1)
    qpos = i * tq + (row & (tq - 1))
    kpos = start + col
    s_band = jnp.where(jnp.abs(kpos - qpos) <= WINDOW, s_band, NEG_INF)
    s_ctx = lax.dot_general(q, k_ref[0, 0, pl.ds(n_lat, n_ctx), :], nt, preferred_element_type=F32)
    sink = _sink_rows(sink_ref, pl.program_id(1), tq)
    m = jnp.maximum(jnp.maximum(jnp.max(s_band, axis=-1, keepdims=True), jnp.max(s_ctx, axis=-1, keepdims=True)), sink)
    p_band = jnp.exp(s_band - m)
    p_ctx = jnp.exp(s_ctx - m)
    den = jnp.sum(p_band, axis=-1, keepdims=True) + jnp.sum(p_ctx, axis=-1, keepdims=True) + jnp.exp(sink - m)
    acc = (_dot(p_band.astype(BF16), v_ref[0, 0, pl.ds(start, span), :])
           + _dot(p_ctx.astype(BF16), v_ref[0, 0, pl.ds(n_lat, n_ctx), :]))
    o_ref[0] = (acc / den).reshape(GRP, tq, HEAD_DIM).astype(o_ref.dtype)


def _attention_band(q, k, v, sink, n_lat, n_ctx, tq):
    nb = q.shape[0]
    n_k = k.shape[2]
    q_spec = pl.BlockSpec((1, GRP, tq, HEAD_DIM), lambda b, h, i: (b, h, i, 0))
    kv_spec = pl.BlockSpec((1, 1, n_k, HEAD_DIM), lambda b, h, i: (b, h, 0, 0))
    return pl.pallas_call(
        functools.partial(_attn_band_kernel, tq=tq, n_lat=n_lat, n_ctx=n_ctx),
        grid=(nb, N_KV, n_lat // tq),
        in_specs=[pl.BlockSpec(memory_space=pltpu.SMEM), q_spec, kv_spec, kv_spec],
        out_specs=q_spec,
        out_shape=jax.ShapeDtypeStruct(q.shape, BF16),
        compiler_params=_cparams(3),
        name="attention_band",
    )(sink, q, k, v)


def _split_heads(x, nb, seq_len, n_heads):
    return x.reshape(nb, seq_len, n_heads, HEAD_DIM).transpose(0, 2, 1, 3)


def _merge_heads(x):
    nb, n_heads, seq_len, _ = x.shape
    return x.transpose(0, 2, 1, 3).reshape(nb * seq_len, n_heads * HEAD_DIM)


def _layer_norm(z, g, b):
    mu = jnp.mean(z, axis=-1, keepdims=True)
    zc = z - mu
    var = jnp.mean(zc * zc, axis=-1, keepdims=True)
    return zc * lax.rsqrt(var + LN_EPS) * g + b


def _merge_kernel(x_ref, g1_ref, ya_ref, yw_ref, yg_ref, gate_ref,
                  wglu_ref, wa_ref, ww_ref, wg_ref, wout_ref, lng_ref, lnb_ref, o_ref):
    ya = ya_ref[...]
    ya = ya * jax.nn.sigmoid(_dot(ya.astype(BF16), wglu_ref[...]))

    def gate(c):
        return jax.nn.sigmoid(gate_ref[:, c * D_MODEL:(c + 1) * D_MODEL].astype(F32))

    m = (gate(0) * _dot(ya.astype(BF16), wa_ref[...])
         + gate(1) * _dot(yw_ref[...], ww_ref[...])
         + gate(2) * _dot(yg_ref[...], wg_ref[...]))
    f = _dot(m.astype(BF16), wout_ref[...])
    o_ref[...] = _layer_norm(DEEPNORM_ALPHA * x_ref[...] + g1_ref[0] * f, lng_ref[...], lnb_ref[...])


def _merge(x, mod_l, row_of_tile, ya, yw, yg, gates, w_glu, w_a, w_w, w_g, w_out, ln_g, ln_b):
    n_tok = x.shape[0]
    tm = ROW_TILE
    row = lambda w: pl.BlockSpec((tm, w), lambda i: (i, 0))
    return pl.pallas_call(
        _merge_kernel,
        grid=(n_tok // tm,),
        in_specs=[row(D_MODEL), _mod_spec(2, row_of_tile), row(D_SSM), row(D_ATT), row(D_ATT), row(3 * D_MODEL),
                  _resident((D_SSM, D_SSM)), _resident((D_SSM, D_MODEL)), _resident((D_ATT, D_MODEL)),
                  _resident((D_ATT, D_MODEL)), _resident((D_MODEL, D_MODEL)),
                  _resident((1, D_MODEL)), _resident((1, D_MODEL))],
        out_specs=row(D_MODEL),
        out_shape=jax.ShapeDtypeStruct((n_tok, D_MODEL), F32),
        compiler_params=_cparams(1),
        name="merge_residual",
    )(x, mod_l, ya, yw, yg, gates, w_glu, w_a, w_w, w_g, w_out, ln_g, ln_b)


def _mlp_kernel(x_ref, sh_ref, sc_ref, g2_ref, wup_ref, wdn_ref, lng_ref, lnb_ref, o_ref):
    x = x_ref[...]
    h = (x * (1.0 + sc_ref[0]) + sh_ref[0]).astype(BF16)
    ff_tile = 1024
    acc = jnp.zeros(x.shape, F32)
    for c in range(D_FF // ff_tile):
        up = jnp.maximum(_dot(h, wup_ref[:, c * ff_tile:(c + 1) * ff_tile]), 0.0)
        acc = acc + _dot((up * up).astype(BF16), wdn_ref[c * ff_tile:(c + 1) * ff_tile, :])
    o_ref[...] = _layer_norm(DEEPNORM_ALPHA * x + g2_ref[0] * acc, lng_ref[...], lnb_ref[...])


def _mlp(x, mod_l, row_of_tile, w_up, w_down, ln_g, ln_b):
    n_tok = x.shape[0]
    tm = ROW_TILE
    row = pl.BlockSpec((tm, D_MODEL), lambda i: (i, 0))
    return pl.pallas_call(
        _mlp_kernel,
        grid=(n_tok // tm,),
        in_specs=[row, _mod_spec(3, row_of_tile), _mod_spec(4, row_of_tile), _mod_spec(5, row_of_tile),
                  _resident((D_MODEL, D_FF)), _resident((D_FF, D_MODEL)),
                  _resident((1, D_MODEL)), _resident((1, D_MODEL))],
        out_specs=row,
        out_shape=jax.ShapeDtypeStruct((n_tok, D_MODEL), F32),
        compiler_params=_cparams(1),
        name="mlp_residual",
    )(x, mod_l, mod_l, mod_l, w_up, w_down, ln_g, ln_b)


def _rope_tables(n_tok):
    rows = n_tok // GRID_W
    row = jnp.repeat(jnp.arange(rows, dtype=F32), GRID_W)
    col = jnp.tile(jnp.arange(GRID_W, dtype=F32), rows)
    n_freq = HEAD_DIM // 4
    inv = ROPE_BASE ** (-jnp.arange(n_freq, dtype=F32) / n_freq)
    ang = jnp.concatenate([row[:, None] * inv, col[:, None] * inv], axis=-1)
    cos, sin = jnp.cos(ang), jnp.sin(ang)
    cos_t = jnp.tile(jnp.concatenate([cos, cos], axis=-1), (1, LANES // HEAD_DIM))
    sin_t = jnp.tile(jnp.concatenate([-sin, sin], axis=-1), (1, LANES // HEAD_DIM))
    return cos_t, sin_t


def _block_diag_ones(n):
    idx = np.arange(n) // HEAD_DIM
    return jnp.asarray(idx[:, None] == idx[None, :], dtype=BF16)


def _layer(x, lw, mod_l, row_of_tile, nb, seq_len, ctx, rope_tabs):
    u, qw, kw, vw, qg, kg, vg, gates = _in_projection(
        x, mod_l, row_of_tile, lw['w_in'], lw['qn'], lw['kn'], lw['bdq'], lw['bdk'], rope_tabs, seq_len)
    if ctx is None:
        s0 = [jnp.zeros((nb, N_STATE), F32)] * 4
    else:
        s0 = [ctx[0][:, d, part].reshape(nb, N_STATE) for d in (0, 1) for part in (0, 1)]
    ya, s_fin = _s5_branch(u, lw['s5_ops'], lw['d_skip'], s0, nb, seq_len)

    def kv_heads(t):
        return _split_heads(t, nb, seq_len, N_KV)

    qw_h = _split_heads(qw, nb, seq_len, N_HEADS)
    qg_h = _split_heads(qg, nb, seq_len, N_HEADS)
    if ctx is None:
        yw = _attention_full(qw_h, kv_heads(kw).astype(BF16), kv_heads(vw).astype(BF16), lw['sink'], seq_len, seq_len)
        yg = _attention_full(qg_h, kv_heads(kg).astype(BF16), kv_heads(vg).astype(BF16), None, seq_len, seq_len)
        new_ctx = (s_fin, kw, vw, kg, vg)
    else:
        _, k_wc, v_wc, k_gc, v_gc = ctx
        n_ctx = k_wc.shape[1]

        def with_ctx(new, cached):
            cached = cached.transpose(0, 2, 1, 3)
            return jnp.concatenate([kv_heads(new), cached], axis=2).astype(BF16)

        yw = _attention_band(qw_h, with_ctx(kw, k_wc), with_ctx(vw, v_wc), lw['sink'], seq_len, n_ctx, 256)
        yg = _attention_full(qg_h, with_ctx(kg, k_gc), with_ctx(vg, v_gc), None, 256, 256)
        new_ctx = None
    x1 = _merge(x, mod_l, row_of_tile, ya, _merge_heads(yw), _merge_heads(yg), gates,
                lw['w_glu'], lw['w_br_ssm'], lw['w_br_win'], lw['w_br_glb'], lw['w_out'], lw['ln1_g'], lw['ln1_b'])
    x2 = _mlp(x1, mod_l, row_of_tile, lw['w_up'], lw['w_down'], lw['ln2_g'], lw['ln2_b'])
    return x2, new_ctx


def kernel(x_prompt, x_sample, state_ssm, cache_k_win, cache_v_win, cache_k_glb, cache_v_glb, c, c_ctx, w_mod, b_mod, w_in, ssm_lam_re, ssm_lam_im, ssm_log_step, ssm_b_re, ssm_b_im, ssm_c_re, ssm_c_im, ssm_d, w_glu, sink_win, q_norm_glb, k_norm_glb, w_br_ssm, w_br_win, w_br_glb, w_out, ln1_g, ln1_b, w_up, w_down, ln2_g, ln2_b):
    n_ctx_b, ctx_len, _ = x_prompt.shape
    n_lat_b, lat_len, _ = x_sample.shape
    assert ctx_len % ROW_TILE == 0 or ROW_TILE % ctx_len == 0
    assert lat_len % ROW_TILE == 0 and (n_ctx_b * ctx_len) % ROW_TILE == 0

    cond8 = jnp.zeros((8, D_MODEL), F32).at[0].set(c_ctx).at[1:1 + n_lat_b].set(c)
    mod = _modulation(cond8, w_mod, b_mod).reshape(DEPTH, 8, 1, 6 * D_MODEL)
    rope_tabs = _rope_tables(lat_len)
    bdq, bdk = _block_diag_ones(D_ATT), _block_diag_ones(D_KV)
    lat_tiles = lat_len // ROW_TILE
    ctx_row = lambda i: 0
    lat_row = lambda i: 1 + i // lat_tiles

    xp = x_prompt.reshape(n_ctx_b * ctx_len, D_MODEL)
    xs = x_sample.reshape(n_lat_b * lat_len, D_MODEL)
    new_ssm, new_kw, new_vw, new_kg, new_vg = [], [], [], [], []
    for l in range(DEPTH):
        pw, eb, ca, kk = _s5_params(ssm_lam_re[l], ssm_lam_im[l], ssm_log_step[l],
                                    ssm_b_re[l], ssm_b_im[l], ssm_c_re[l], ssm_c_im[l])
        lw = dict(
            w_in=w_in[l].astype(BF16),
            qn=jnp.tile(q_norm_glb[l], N_HEADS).reshape(1, D_ATT), kn=jnp.tile(k_norm_glb[l], N_KV).reshape(1, D_KV),
            bdq=bdq, bdk=bdk,
            s5_ops=_s5_operators(pw, eb, ca, kk), d_skip=ssm_d[l],
            sink=sink_win[l],
            w_glu=w_glu[l].astype(BF16), w_br_ssm=w_br_ssm[l].astype(BF16), w_br_win=w_br_win[l].astype(BF16),
            w_br_glb=w_br_glb[l].astype(BF16), w_out=w_out[l].astype(BF16),
            ln1_g=ln1_g[l].reshape(1, D_MODEL), ln1_b=ln1_b[l].reshape(1, D_MODEL),
            w_up=w_up[l].astype(BF16), w_down=w_down[l].astype(BF16),
            ln2_g=ln2_g[l].reshape(1, D_MODEL), ln2_b=ln2_b[l].reshape(1, D_MODEL),
        )
        xp, (s_fin, kw, vw, kg, vg) = _layer(xp, lw, mod[l], ctx_row, n_ctx_b, ctx_len, None, None)
        new_ssm.append(jnp.stack(s_fin, axis=1).reshape(n_ctx_b, 2, 2, SSM_GROUPS, SSM_STATE))
        for acc, t in ((new_kw, kw), (new_vw, vw), (new_kg, kg), (new_vg, vg)):
            acc.append(t.reshape(n_ctx_b, ctx_len, N_KV, HEAD_DIM))
        ctx = (state_ssm[:, l], cache_k_win[:, l], cache_v_win[:, l], cache_k_glb[:, l], cache_v_glb[:, l])
        xs, _ = _layer(xs, lw, mod[l], lat_row, n_lat_b, lat_len, ctx, rope_tabs)
    return (xp.reshape(x_prompt.shape), xs.reshape(x_sample.shape),
            jnp.stack(new_ssm, axis=1), jnp.stack(new_kw, axis=1), jnp.stack(new_vw, axis=1),
            jnp.stack(new_kg, axis=1), jnp.stack(new_vg, axis=1))
```

```python
import functools

import jax
import jax.numpy as jnp
import numpy as np
from jax import lax
from jax.experimental import pallas as pl
from jax.experimental.pallas import tpu as pltpu

F32 = jnp.float32
BF16 = jnp.bfloat16

D_MODEL = 1024
DEPTH = 2
GRID_W = 64
HEAD_DIM = 64
D_SSM = 512
SSM_GROUP_CH = 16
SSM_GROUPS = 32
SSM_STATE = 64
N_HEADS = 8
N_KV = 2
GRP = N_HEADS // N_KV
D_ATT = N_HEADS * HEAD_DIM
D_KV = N_KV * HEAD_DIM
WINDOW = 128
ROPE_BASE = 10000.0
D_FF = 4 * D_MODEL
LN_EPS = 1e-5
RMS_EPS = 1e-6
ATTN_SCALE = HEAD_DIM ** -0.5
DEEPNORM_ALPHA = (2.0 * DEPTH) ** 0.25
NEG_INF = -1e30
N_IN = D_SSM + 2 * (D_ATT + 2 * D_KV) + 3 * D_MODEL
O_U = 0
O_QW = O_U + D_SSM
O_KW = O_QW + D_ATT
O_VW = O_KW + D_KV
O_QG = O_VW + D_KV
O_KG = O_QG + D_ATT
O_VG = O_KG + D_KV
O_GATE = O_VG + D_KV

S5_CHUNK = 16
S5_PAIR = 2 * SSM_GROUP_CH
S5_NPAIR = SSM_GROUPS // 2
S5_ROW = S5_CHUNK * S5_PAIR
S5_PSTATE = 2 * SSM_STATE
N_STATE = SSM_GROUPS * SSM_STATE

LANES = 128
ROW_TILE = 512
VMEM_LIMIT = 56 * 1024 * 1024


def _cparams(n_axes):
    return pltpu.CompilerParams(dimension_semantics=("arbitrary",) * n_axes, vmem_limit_bytes=VMEM_LIMIT)


def _resident(shape):
    nd = len(shape)
    return pl.BlockSpec(shape, lambda *_: (0,) * nd, pipeline_mode=pl.Buffered(1))


def _dot(a, b):
    return jnp.dot(a, b, preferred_element_type=F32)


def _mod_kernel(c_ref, w_ref, b_ref, o_ref):
    c = c_ref[...]
    a = (c * jax.nn.sigmoid(c)).astype(BF16)
    o_ref[0] = _dot(a, w_ref[0].astype(BF16)) + b_ref[0]


def _modulation(cond8, w_mod, b_mod):
    tn = 512
    n_out = w_mod.shape[-1]
    return pl.pallas_call(
        _mod_kernel,
        grid=(DEPTH, n_out // tn),
        in_specs=[
            pl.BlockSpec((8, D_MODEL), lambda l, n: (0, 0)),
            pl.BlockSpec((1, D_MODEL, tn), lambda l, n: (l, 0, n)),
            pl.BlockSpec((1, 1, tn), lambda l, n: (l, 0, n)),
        ],
        out_specs=pl.BlockSpec((1, 8, tn), lambda l, n: (l, 0, n)),
        out_shape=jax.ShapeDtypeStruct((DEPTH, 8, n_out), F32),
        compiler_params=_cparams(2),
        name="modulation",
    )(cond8, w_mod, b_mod.reshape(DEPTH, 1, n_out))


def _mod_spec(chunk, row_of_tile):
    return pl.BlockSpec((1, 1, D_MODEL), lambda i: (row_of_tile(i), 0, chunk))


def _head_rms(x, gain, ones_bd):
    sq = x * x
    hi = sq.astype(BF16)
    lo = (sq - hi.astype(F32)).astype(BF16)
    ss = _dot(hi, ones_bd) + _dot(lo, ones_bd)
    return x * lax.rsqrt(ss * (1.0 / HEAD_DIM) + RMS_EPS) * gain


def _rope_chunk(xc, cos_t, sin_t):
    lane = lax.broadcasted_iota(jnp.int32, xc.shape, 1)
    first_half = (lane & (HEAD_DIM - 1)) < (HEAD_DIM // 2)
    partner = jnp.where(first_half, pltpu.roll(xc, LANES - HEAD_DIM // 2, 1), pltpu.roll(xc, HEAD_DIM // 2, 1))
    return xc * cos_t + partner * sin_t


def _inproj_kernel(*refs, rope):
    if rope:
        (x_ref, sh_ref, sc_ref, w_ref, qn_ref, kn_ref, bdq_ref, bdk_ref, cos_ref, sin_ref,
         u_ref, qw_ref, kw_ref, vw_ref, qg_ref, kg_ref, vg_ref, gate_ref) = refs
    else:
        (x_ref, sh_ref, sc_ref, w_ref, qn_ref, kn_ref, bdq_ref, bdk_ref,
         u_ref, qw_ref, kw_ref, vw_ref, qg_ref, kg_ref, vg_ref, gate_ref) = refs
    h = (x_ref[...] * (1.0 + sc_ref[0]) + sh_ref[0]).astype(BF16)

    def proj(lo, width):
        return _dot(h, w_ref[:, lo:lo + width])

    def finish(x, out_ref, scale):
        n = x.shape[-1]
        for c in range(n // LANES):
            xc = x[:, c * LANES:(c + 1) * LANES]
            if rope:
                xc = _rope_chunk(xc, cos_ref[...], sin_ref[...])
            if scale != 1.0:
                xc = xc * scale
            out_ref[:, c * LANES:(c + 1) * LANES] = xc.astype(out_ref.dtype)

    u_ref[...] = proj(O_U, D_SSM)
    finish(proj(O_QW, D_ATT), qw_ref, ATTN_SCALE)
    finish(proj(O_KW, D_KV), kw_ref, 1.0)
    vw_ref[...] = proj(O_VW, D_KV)
    finish(_head_rms(proj(O_QG, D_ATT), qn_ref[...], bdq_ref[...]), qg_ref, ATTN_SCALE)
    finish(_head_rms(proj(O_KG, D_KV), kn_ref[...], bdk_ref[...]), kg_ref, 1.0)
    vg_ref[...] = proj(O_VG, D_KV)
    for c in range(3):
        gate_ref[:, c * D_MODEL:(c + 1) * D_MODEL] = proj(O_GATE + c * D_MODEL, D_MODEL).astype(gate_ref.dtype)


def _in_projection(x, mod_l, row_of_tile, w_in, qn, kn, bdq, bdk, rope_tabs, seq_len):
    n_tok = x.shape[0]
    tm = ROW_TILE
    rope = rope_tabs is not None
    row = lambda w: pl.BlockSpec((tm, w), lambda i: (i, 0))
    in_specs = [
        row(D_MODEL),
        _mod_spec(0, row_of_tile), _mod_spec(1, row_of_tile),
        _resident((D_MODEL, N_IN)),
        _resident((1, D_ATT)), _resident((1, D_KV)),
        _resident((D_ATT, D_ATT)), _resident((D_KV, D_KV)),
    ]
    args = [x, mod_l, mod_l, w_in, qn, kn, bdq, bdk]
    if rope:
        tiles_per_seq = seq_len // tm
        tab = pl.BlockSpec((tm, LANES), lambda i: (i % tiles_per_seq, 0))
        in_specs += [tab, tab]
        args += list(rope_tabs)
    out_shape = [
        jax.ShapeDtypeStruct((n_tok, D_SSM), F32),
        jax.ShapeDtypeStruct((n_tok, D_ATT), BF16),
        jax.ShapeDtypeStruct((n_tok, D_KV), F32),
        jax.ShapeDtypeStruct((n_tok, D_KV), F32),
        jax.ShapeDtypeStruct((n_tok, D_ATT), BF16),
        jax.ShapeDtypeStruct((n_tok, D_KV), F32),
        jax.ShapeDtypeStruct((n_tok, D_KV), F32),
        jax.ShapeDtypeStruct((n_tok, 3 * D_MODEL), BF16),
    ]
    out_specs = [row(s.shape[1]) for s in out_shape]
    return pl.pallas_call(
        functools.partial(_inproj_kernel, rope=rope),
        grid=(n_tok // tm,),
        in_specs=in_specs,
        out_specs=out_specs,
        out_shape=out_shape,
        compiler_params=_cparams(1),
        name="in_projection",
    )(*args)


def _s5_param_kernel(lr_ref, li_ref, ls_ref, br_ref, bi_ref, cr_ref, ci_ref,
                     pw_ref, eb_ref, ca_ref, k_ref):
    lr, li = lr_ref[0, 0], li_ref[0, 0]
    dt = jnp.exp(ls_ref[0, 0])
    mag = jnp.exp(lr * dt)
    ar, ai = mag * jnp.cos(li * dt), mag * jnp.sin(li * dt)
    den = lr * lr + li * li
    fr = ((ar - 1.0) * lr + ai * li) / den
    fi = (ai * lr - (ar - 1.0) * li) / den
    br, bi = br_ref[0, 0], bi_ref[0, 0]
    bbr, bbi = fr * br - fi * bi, fr * bi + fi * br
    cr, ci = cr_ref[0, 0], ci_ref[0, 0]
    nt = (((1,), (1,)), ((), ()))
    pr, pi = jnp.ones_like(ar), jnp.zeros_like(ar)
    for j in range(S5_CHUNK + 1):
        pw_ref[0, 0, 0, j:j + 1, :] = pr
        pw_ref[0, 0, 1, j:j + 1, :] = pi
        car, cai = cr * pr - ci * pi, cr * pi + ci * pr
        if j < S5_CHUNK:
            eb_ref[0, 0, 0, j] = pr * bbr - pi * bbi
            eb_ref[0, 0, 1, j] = pr * bbi + pi * bbr
            k_ref[0, 0, j] = (lax.dot_general(car, bbr, nt, precision=lax.Precision.HIGHEST, preferred_element_type=F32)
                              - lax.dot_general(cai, bbi, nt, precision=lax.Precision.HIGHEST, preferred_element_type=F32))
        if j >= 1:
            ca_ref[0, 0, 0, j - 1] = car
            ca_ref[0, 0, 1, j - 1] = cai
        pr, pi = pr * ar - pi * ai, pr * ai + pi * ar


def _s5_params(lam_re, lam_im, log_step, b_re, b_im, c_re, c_im):
    g, p, cg = SSM_GROUPS, SSM_STATE, SSM_GROUP_CH
    vec = pl.BlockSpec((1, 1, 1, p), lambda d, i: (d, i, 0, 0))
    mat = pl.BlockSpec((1, 1, cg, p), lambda d, i: (d, i, 0, 0))
    ls = jnp.broadcast_to(log_step[:, :, None, None], (2, g, 1, p))
    return pl.pallas_call(
        _s5_param_kernel,
        grid=(2, g),
        in_specs=[vec, vec, vec, mat, mat, mat, mat],
        out_specs=[
            pl.BlockSpec((1, 1, 2, S5_CHUNK + 1, p), lambda d, i: (d, i, 0, 0, 0)),
            pl.BlockSpec((1, 1, 2, S5_CHUNK, cg, p), lambda d, i: (d, i, 0, 0, 0, 0)),
            pl.BlockSpec((1, 1, 2, S5_CHUNK, cg, p), lambda d, i: (d, i, 0, 0, 0, 0)),
            pl.BlockSpec((1, 1, S5_CHUNK, cg, cg), lambda d, i: (d, i, 0, 0, 0)),
        ],
        out_shape=[
            jax.ShapeDtypeStruct((2, g, 2, S5_CHUNK + 1, p), F32),
            jax.ShapeDtypeStruct((2, g, 2, S5_CHUNK, cg, p), F32),
            jax.ShapeDtypeStruct((2, g, 2, S5_CHUNK, cg, p), F32),
            jax.ShapeDtypeStruct((2, g, S5_CHUNK, cg, cg), F32),
        ],
        compiler_params=_cparams(2),
        name="s5_params",
    )(lam_re.reshape(2, g, 1, p), lam_im.reshape(2, g, 1, p), ls,
      jnp.swapaxes(b_re, -1, -2), jnp.swapaxes(b_im, -1, -2), c_re, c_im)


def _s5_operators(pw, eb, ca, kk):
    g, p, cg, tc = SSM_GROUPS, SSM_STATE, SSM_GROUP_CH, S5_CHUNK
    eye2 = jnp.eye(2, dtype=F32)
    t = jnp.arange(tc)
    lag = t[None, :] - t[:, None]
    kf = jnp.where((lag >= 0)[None, :, :, None, None], kk[0][:, jnp.clip(lag, 0, tc - 1)], 0.0)
    kb = jnp.where((lag <= 0)[None, :, :, None, None], kk[1][:, jnp.clip(-lag, 0, tc - 1)], 0.0)
    ktot = (kf + kb).reshape(S5_NPAIR, 2, tc, tc, cg, cg)
    top = jnp.einsum('ngtsoc,gh->ntgcsho', ktot, eye2).reshape(S5_NPAIR, S5_ROW, S5_ROW).astype(BF16)

    def end_op(d, part):
        e = eb[d, :, part]
        e = e[:, ::-1] if d == 0 else e
        e = e.reshape(S5_NPAIR, 2, tc, cg, p)
        return jnp.einsum('ngtcp,gh->ntgchp', e, eye2).reshape(S5_NPAIR, S5_ROW, S5_PSTATE).astype(BF16)

    def carry_op(d, part):
        m = ca[d, :, part]
        m = m if d == 0 else m[:, ::-1]
        m = m if part == 0 else -m
        m = m.reshape(S5_NPAIR, 2, tc, cg, p)
        return jnp.einsum('ngtop,gh->ngptho', m, eye2).reshape(S5_NPAIR, S5_PSTATE, S5_ROW).astype(BF16)

    ends = [end_op(d, part) for d in (0, 1) for part in (0, 1)]
    carries = [carry_op(d, part) for d in (0, 1) for part in (0, 1)]
    decay = [pw[d, :, part, tc].reshape(1, N_STATE) for d in (0, 1) for part in (0, 1)]
    return top, ends, carries, decay


def _s5_ends_kernel(u_ref, e0, e1, e2, e3, o0, o1, o2, o3):
    u = u_ref[0].astype(BF16)
    for e_ref, o_ref in ((e0, o0), (e1, o1), (e2, o2), (e3, o3)):
        o_ref[...] = _dot(u, e_ref[0])


def _s5_ends(u_rows, ends, rt):
    n_rows = u_rows.shape[1]
    e_spec = pl.BlockSpec((1, S5_ROW, S5_PSTATE), lambda n, r: (n, 0, 0))
    o_spec = pl.BlockSpec((rt, S5_PSTATE), lambda n, r: (r, n))
    return pl.pallas_call(
        _s5_ends_kernel,
        grid=(S5_NPAIR, n_rows // rt),
        in_specs=[pl.BlockSpec((1, rt, S5_ROW), lambda n, r: (n, r, 0))] + [e_spec] * 4,
        out_specs=[o_spec] * 4,
        out_shape=[jax.ShapeDtypeStruct((n_rows, N_STATE), F32)] * 4,
        compiler_params=_cparams(2),
        name="s5_chunk_ends",
    )(u_rows, *ends)


def _s5_scan_kernel(efr, efi, ebr, ebi, afr, afi, abr, abi, sfr, sfi, sbr, sbi,
                    pfr, pfi, nbr, nbi, ofr, ofi, obr, obi):
    nc = efr.shape[0]
    a_fr, a_fi, a_br, a_bi = afr[...], afi[...], abr[...], abi[...]

    def body(c, carry):
        fr, fi, br, bi = carry
        cb = nc - 1 - c
        pfr[c] = fr
        pfi[c] = fi
        nbr[cb] = br
        nbi[cb] = bi
        nfr = a_fr * fr - a_fi * fi + efr[c]
        nfi = a_fr * fi + a_fi * fr + efi[c]
        nbr_ = a_br * br - a_bi * bi + ebr[cb]
        nbi_ = a_br * bi + a_bi * br + ebi[cb]
        return nfr, nfi, nbr_, nbi_

    fr, fi, br, bi = lax.fori_loop(0, nc, body, (sfr[...], sfi[...], sbr[...], sbi[...]))
    ofr[...] = fr
    ofi[...] = fi
    obr[...] = br
    obi[...] = bi


def _s5_scan(se, decay, s0, nc, nb):
    cw = 256
    seq = pl.BlockSpec((nc, nb, cw), lambda i: (0, 0, i))
    vec = pl.BlockSpec((1, cw), lambda i: (0, i))
    st = pl.BlockSpec((nb, cw), lambda i: (0, i))
    se3 = [s.reshape(nc, nb, N_STATE) for s in se]
    return pl.pallas_call(
        _s5_scan_kernel,
        grid=(N_STATE // cw,),
        in_specs=[seq] * 4 + [vec] * 4 + [st] * 4,
        out_specs=[seq] * 4 + [st] * 4,
        out_shape=[jax.ShapeDtypeStruct((nc, nb, N_STATE), F32)] * 4 + [jax.ShapeDtypeStruct((nb, N_STATE), F32)] * 4,
        compiler_params=_cparams(1),
        name="s5_chunk_scan",
    )(*se3, *decay, *s0)


def _s5_out_kernel(u_ref, top_ref, d_ref, pfr, pfi, nbr, nbi, m0, m1, m2, m3, o_ref):
    u = u_ref[0]
    y = _dot(u.astype(BF16), top_ref[0]) + d_ref[0] * u
    for s_ref, m_ref in ((pfr, m0), (pfi, m1), (nbr, m2), (nbi, m3)):
        y = y + _dot(s_ref[...].astype(BF16), m_ref[0])
    o_ref[0] = jax.nn.gelu(y)


def _s5_out(u_rows, top, d_rows, states, carries, rt):
    n_rows = u_rows.shape[1]
    s_spec = pl.BlockSpec((rt, S5_PSTATE), lambda n, r: (r, n))
    m_spec = pl.BlockSpec((1, S5_PSTATE, S5_ROW), lambda n, r: (n, 0, 0))
    return pl.pallas_call(
        _s5_out_kernel,
        grid=(S5_NPAIR, n_rows // rt),
        in_specs=[pl.BlockSpec((1, rt, S5_ROW), lambda n, r: (n, r, 0)),
                  pl.BlockSpec((1, S5_ROW, S5_ROW), lambda n, r: (n, 0, 0)),
                  pl.BlockSpec((1, 1, S5_ROW), lambda n, r: (n, 0, 0))] + [s_spec] * 4 + [m_spec] * 4,
        out_specs=pl.BlockSpec((1, rt, S5_ROW), lambda n, r: (n, r, 0)),
        out_shape=jax.ShapeDtypeStruct(u_rows.shape, F32),
        compiler_params=_cparams(2),
        name="s5_outputs",
    )(u_rows, top, d_rows, *states, *carries)


def _s5_branch(u, ops, d_skip, s0, nb, seq_len):
    top, ends, carries, decay = ops
    nc = seq_len // S5_CHUNK
    n_rows = nc * nb
    rt = min(n_rows, 256)
    u_rows = u.reshape(nb, nc, S5_CHUNK, S5_NPAIR, S5_PAIR).transpose(3, 1, 0, 2, 4).reshape(S5_NPAIR, n_rows, S5_ROW)
    d_rows = jnp.tile(d_skip.reshape(S5_NPAIR, 1, S5_PAIR), (1, S5_CHUNK, 1)).reshape(S5_NPAIR, 1, S5_ROW)
    se = _s5_ends(u_rows, ends, rt)
    scanned = _s5_scan(se, decay, s0, nc, nb)
    states = [s.reshape(n_rows, N_STATE) for s in scanned[:4]]
    y_rows = _s5_out(u_rows, top, d_rows, states, carries, rt)
    y = y_rows.reshape(S5_NPAIR, nc, nb, S5_CHUNK, S5_PAIR).transpose(2, 1, 3, 0, 4).reshape(nb * seq_len, D_SSM)
    return y, scanned[4:]


Q_SLOT_ORDER = tuple(h for j in range(GRP) for h in (j, GRP + j))
NT_DIMS = (((1,), (1,)), ((), ()))


def _stack_group_queries(q_ref, h, tq):
    lane = lax.broadcasted_iota(jnp.int32, (tq, LANES), 1)
    keep = (lane >= h * HEAD_DIM) & (lane < (h + 1) * HEAD_DIM)
    zero = jnp.zeros((tq, LANES), BF16)
    return jnp.concatenate([jnp.where(keep, q_ref[:, j * LANES:(j + 1) * LANES], zero) for j in range(GRP)], axis=0)


def _sink_lanes(sink_ref, h, tq):
    return jnp.concatenate([jnp.full((1, tq), sink_ref[h * GRP + j], F32) for j in range(GRP)], axis=1)


def _store_heads(o_ref, h, o_t, tq):
    for j in range(GRP):
        head = h * GRP + j
        o_ref[0, head * HEAD_DIM:(head + 1) * HEAD_DIM, :] = o_t[:, j * tq:(j + 1) * tq].astype(o_ref.dtype)


def _attn_full_kernel(*refs, tq, use_sink):
    if use_sink:
        sink_ref, q_ref, k_ref, vt_ref, o_ref, m_ref, l_ref, acc_ref = refs
    else:
        q_ref, k_ref, vt_ref, o_ref, m_ref, l_ref, acc_ref = refs
    n_chunks = k_ref.shape[1]
    for h in range(N_KV):
        qs = _stack_group_queries(q_ref, h, tq)
        if use_sink:
            m_ref[...] = _sink_lanes(sink_ref, h, tq)
            l_ref[...] = jnp.ones(l_ref.shape, F32)
        else:
            m_ref[...] = jnp.full(m_ref.shape, NEG_INF, F32)
            l_ref[...] = jnp.zeros(l_ref.shape, F32)
        acc_ref[...] = jnp.zeros(acc_ref.shape, F32)

        def body(c, carry):
            s = lax.dot_general(k_ref[0, c], qs, NT_DIMS, preferred_element_type=F32)
            m_old = m_ref[...]
            m_new = jnp.maximum(m_old, jnp.max(s, axis=0, keepdims=True))
            p = jnp.exp(s - m_new)
            alpha = jnp.exp(m_old - m_new)
            l_ref[...] = alpha * l_ref[...] + jnp.sum(p, axis=0, keepdims=True)
            vt = vt_ref[0, c, h * HEAD_DIM:(h + 1) * HEAD_DIM, :]
            acc_ref[...] = alpha * acc_ref[...] + _dot(vt, p.astype(BF16))
            m_ref[...] = m_new
            return carry

        lax.fori_loop(0, n_chunks, body, 0)
        _store_heads(o_ref, h, acc_ref[...] / l_ref[...], tq)


def _attn_band_kernel(sink_ref, q_ref, k_ref, vt_ref, o_ref, *, tq, n_lat, n_ctx):
    i = pl.program_id(1)
    n_band = tq // WINDOW + 2
    lat_chunks, ctx_chunks = n_lat // WINDOW, n_ctx // WINDOW
    c0 = jnp.clip(i * (tq // WINDOW) - 1, 0, lat_chunks - n_band)
    rows = GRP * tq
    row = lax.broadcasted_iota(jnp.int32, (n_band * WINDOW, rows), 0)
    col = lax.broadcasted_iota(jnp.int32, (n_band * WINDOW, rows), 1)
    in_band = jnp.abs((c0 * WINDOW + row) - (i * tq + (col & (tq - 1)))) <= WINDOW
    for h in range(N_KV):
        qs = _stack_group_queries(q_ref, h, tq)
        k_band = k_ref[0, pl.ds(c0, n_band)].reshape(n_band * WINDOW, LANES)
        s_band = jnp.where(in_band, lax.dot_general(k_band, qs, NT_DIMS, preferred_element_type=F32), NEG_INF)
        k_ctx = k_ref[0, lat_chunks:lat_chunks + ctx_chunks].reshape(n_ctx, LANES)
        s_ctx = lax.dot_general(k_ctx, qs, NT_DIMS, preferred_element_type=F32)
        sink = _sink_lanes(sink_ref, h, tq)
        m = jnp.maximum(jnp.maximum(jnp.max(s_band, axis=0, keepdims=True), jnp.max(s_ctx, axis=0, keepdims=True)), sink)
        p_band = jnp.exp(s_band - m)
        p_ctx = jnp.exp(s_ctx - m)
        den = jnp.sum(p_band, axis=0, keepdims=True) + jnp.sum(p_ctx, axis=0, keepdims=True) + jnp.exp(sink - m)
        p_band, p_ctx = p_band.astype(BF16), p_ctx.astype(BF16)
        hd = slice(h * HEAD_DIM, (h + 1) * HEAD_DIM)
        acc = jnp.zeros((HEAD_DIM, rows), F32)
        for t in range(n_band):
            acc = acc + _dot(vt_ref[0, c0 + t, hd, :], p_band[t * WINDOW:(t + 1) * WINDOW])
        for t in range(ctx_chunks):
            acc = acc + _dot(vt_ref[0, lat_chunks + t, hd, :], p_ctx[t * WINDOW:(t + 1) * WINDOW])
        _store_heads(o_ref, h, acc / den, tq)


def _attention(q, k, v, sink, nb, seq_len, tq, tk, band_ctx=None):
    n_k = k.shape[1]
    kc = k.reshape(nb, n_k // tk, tk, D_KV)
    vt = v.reshape(nb, n_k // tk, tk, D_KV).transpose(0, 1, 3, 2)
    tiles = seq_len // tq
    q_spec = pl.BlockSpec((tq, D_ATT), lambda b, i: (b * tiles + i, 0))
    k_spec = pl.BlockSpec((1, n_k // tk, tk, D_KV), lambda b, i: (b, 0, 0, 0))
    vt_spec = pl.BlockSpec((1, n_k // tk, D_KV, tk), lambda b, i: (b, 0, 0, 0))
    in_specs, args = [q_spec, k_spec, vt_spec], [q, kc, vt]
    if sink is not None:
        in_specs, args = [pl.BlockSpec(memory_space=pltpu.SMEM)] + in_specs, [sink] + args
    rows = GRP * tq
    if band_ctx is None:
        body = functools.partial(_attn_full_kernel, tq=tq, use_sink=sink is not None)
        scratch = [pltpu.VMEM((1, rows), F32), pltpu.VMEM((1, rows), F32), pltpu.VMEM((HEAD_DIM, rows), F32)]
        name = "attention_full"
    else:
        body = functools.partial(_attn_band_kernel, tq=tq, n_lat=seq_len, n_ctx=band_ctx)
        scratch = []
        name = "attention_band"
    o_t = pl.pallas_call(
        body,
        grid=(nb, tiles),
        in_specs=in_specs,
        out_specs=pl.BlockSpec((1, D_ATT, tq), lambda b, i: (b, 0, i)),
        out_shape=jax.ShapeDtypeStruct((nb, D_ATT, seq_len), BF16),
        scratch_shapes=scratch,
        compiler_params=_cparams(2),
        name=name,
    )(*args)
    return o_t.transpose(0, 2, 1).reshape(nb * seq_len, D_ATT)


def _layer_norm(z, g, b):
    mu = jnp.mean(z, axis=-1, keepdims=True)
    zc = z - mu
    var = jnp.mean(zc * zc, axis=-1, keepdims=True)
    return zc * lax.rsqrt(var + LN_EPS) * g + b


def _merge_kernel(x_ref, g1_ref, ya_ref, yw_ref, yg_ref, gate_ref,
                  wglu_ref, wa_ref, ww_ref, wg_ref, wout_ref, lng_ref, lnb_ref, o_ref):
    ya = ya_ref[...]
    ya = ya * jax.nn.sigmoid(_dot(ya.astype(BF16), wglu_ref[...]))

    def gate(c):
        return jax.nn.sigmoid(gate_ref[:, c * D_MODEL:(c + 1) * D_MODEL].astype(F32))

    m = (gate(0) * _dot(ya.astype(BF16), wa_ref[...])
         + gate(1) * _dot(yw_ref[...], ww_ref[...])
         + gate(2) * _dot(yg_ref[...], wg_ref[...]))
    f = _dot(m.astype(BF16), wout_ref[...])
    o_ref[...] = _layer_norm(DEEPNORM_ALPHA * x_ref[...] + g1_ref[0] * f, lng_ref[...], lnb_ref[...])


def _merge(x, mod_l, row_of_tile, ya, yw, yg, gates, w_glu, w_a, w_w, w_g, w_out, ln_g, ln_b):
    n_tok = x.shape[0]
    tm = ROW_TILE
    row = lambda w: pl.BlockSpec((tm, w), lambda i: (i, 0))
    return pl.pallas_call(
        _merge_kernel,
        grid=(n_tok // tm,),
        in_specs=[row(D_MODEL), _mod_spec(2, row_of_tile), row(D_SSM), row(D_ATT), row(D_ATT), row(3 * D_MODEL),
                  _resident((D_SSM, D_SSM)), _resident((D_SSM, D_MODEL)), _resident((D_ATT, D_MODEL)),
                  _resident((D_ATT, D_MODEL)), _resident((D_MODEL, D_MODEL)),
                  _resident((1, D_MODEL)), _resident((1, D_MODEL))],
        out_specs=row(D_MODEL),
        out_shape=jax.ShapeDtypeStruct((n_tok, D_MODEL), F32),
        compiler_params=_cparams(1),
        name="merge_residual",
    )(x, mod_l, ya, yw, yg, gates, w_glu, w_a, w_w, w_g, w_out, ln_g, ln_b)


def _mlp_kernel(x_ref, sh_ref, sc_ref, g2_ref, wup_ref, wdn_ref, lng_ref, lnb_ref, o_ref):
    x = x_ref[...]
    h = (x * (1.0 + sc_ref[0]) + sh_ref[0]).astype(BF16)
    ff_tile = 1024
    acc = jnp.zeros(x.shape, F32)
    for c in range(D_FF // ff_tile):
        up = jnp.maximum(_dot(h, wup_ref[:, c * ff_tile:(c + 1) * ff_tile]), 0.0)
        acc = acc + _dot((up * up).astype(BF16), wdn_ref[c * ff_tile:(c + 1) * ff_tile, :])
    o_ref[...] = _layer_norm(DEEPNORM_ALPHA * x + g2_ref[0] * acc, lng_ref[...], lnb_ref[...])


def _mlp(x, mod_l, row_of_tile, w_up, w_down, ln_g, ln_b):
    n_tok = x.shape[0]
    tm = ROW_TILE
    row = pl.BlockSpec((tm, D_MODEL), lambda i: (i, 0))
    return pl.pallas_call(
        _mlp_kernel,
        grid=(n_tok // tm,),
        in_specs=[row, _mod_spec(3, row_of_tile), _mod_spec(4, row_of_tile), _mod_spec(5, row_of_tile),
                  _resident((D_MODEL, D_FF)), _resident((D_FF, D_MODEL)),
                  _resident((1, D_MODEL)), _resident((1, D_MODEL))],
        out_specs=row,
        out_shape=jax.ShapeDtypeStruct((n_tok, D_MODEL), F32),
        compiler_params=_cparams(1),
        name="mlp_residual",
    )(x, mod_l, mod_l, mod_l, w_up, w_down, ln_g, ln_b)


def _rope_tables(n_tok):
    rows = n_tok // GRID_W
    row = jnp.repeat(jnp.arange(rows, dtype=F32), GRID_W)
    col = jnp.tile(jnp.arange(GRID_W, dtype=F32), rows)
    n_freq = HEAD_DIM // 4
    inv = ROPE_BASE ** (-jnp.arange(n_freq, dtype=F32) / n_freq)
    ang = jnp.concatenate([row[:, None] * inv, col[:, None] * inv], axis=-1)
    cos, sin = jnp.cos(ang), jnp.sin(ang)
    cos_t = jnp.tile(jnp.concatenate([cos, cos], axis=-1), (1, LANES // HEAD_DIM))
    sin_t = jnp.tile(jnp.concatenate([-sin, sin], axis=-1), (1, LANES // HEAD_DIM))
    return cos_t, sin_t


def _block_diag_ones(n):
    idx = np.arange(n) // HEAD_DIM
    return jnp.asarray(idx[:, None] == idx[None, :], dtype=BF16)


def _permute_q_columns(w):
    def perm(block):
        return block.reshape(D_MODEL, N_HEADS, HEAD_DIM)[:, np.array(Q_SLOT_ORDER), :].reshape(D_MODEL, D_ATT)
    return jnp.concatenate([w[:, :O_QW], perm(w[:, O_QW:O_KW]), w[:, O_KW:O_QG], perm(w[:, O_QG:O_KG]), w[:, O_KG:]], axis=1)


def _layer(x, lw, mod_l, row_of_tile, nb, seq_len, ctx, rope_tabs):
    u, qw, kw, vw, qg, kg, vg, gates = _in_projection(
        x, mod_l, row_of_tile, lw['w_in'], lw['qn'], lw['kn'], lw['bdq'], lw['bdk'], rope_tabs, seq_len)
    if ctx is None:
        s0 = [jnp.zeros((nb, N_STATE), F32)] * 4
    else:
        s0 = [ctx[0][:, d, part].reshape(nb, N_STATE) for d in (0, 1) for part in (0, 1)]
    ya, s_fin = _s5_branch(u, lw['s5_ops'], lw['d_skip'], s0, nb, seq_len)

    def seq(t, n):
        return t.reshape(nb, n, D_KV)

    if ctx is None:
        yw = _attention(qw, seq(kw, seq_len).astype(BF16), seq(vw, seq_len).astype(BF16), lw['sink'],
                        nb, seq_len, seq_len, seq_len)
        yg = _attention(qg, seq(kg, seq_len).astype(BF16), seq(vg, seq_len).astype(BF16), None,
                        nb, seq_len, seq_len, seq_len)
        new_ctx = (s_fin, kw, vw, kg, vg)
    else:
        _, k_wc, v_wc, k_gc, v_gc = ctx
        n_ctx = k_wc.shape[1]

        def with_ctx(new, cached):
            return jnp.concatenate([seq(new, seq_len), seq(cached, n_ctx)], axis=1).astype(BF16)

        yw = _attention(qw, with_ctx(kw, k_wc), with_ctx(vw, v_wc), lw['sink'], nb, seq_len, 256, WINDOW, band_ctx=n_ctx)
        yg = _attention(qg, with_ctx(kg, k_gc), with_ctx(vg, v_gc), None, nb, seq_len, 256, 256)
        new_ctx = None
    x1 = _merge(x, mod_l, row_of_tile, ya, yw, yg, gates,
                lw['w_glu'], lw['w_br_ssm'], lw['w_br_win'], lw['w_br_glb'], lw['w_out'], lw['ln1_g'], lw['ln1_b'])
    x2 = _mlp(x1, mod_l, row_of_tile, lw['w_up'], lw['w_down'], lw['ln2_g'], lw['ln2_b'])
    return x2, new_ctx


def kernel(x_prompt, x_sample, state_ssm, cache_k_win, cache_v_win, cache_k_glb, cache_v_glb, c, c_ctx, w_mod, b_mod, w_in, ssm_lam_re, ssm_lam_im, ssm_log_step, ssm_b_re, ssm_b_im, ssm_c_re, ssm_c_im, ssm_d, w_glu, sink_win, q_norm_glb, k_norm_glb, w_br_ssm, w_br_win, w_br_glb, w_out, ln1_g, ln1_b, w_up, w_down, ln2_g, ln2_b):
    n_ctx_b, ctx_len, _ = x_prompt.shape
    n_lat_b, lat_len, _ = x_sample.shape
    assert ctx_len % ROW_TILE == 0 or ROW_TILE % ctx_len == 0
    assert lat_len % ROW_TILE == 0 and (n_ctx_b * ctx_len) % ROW_TILE == 0

    cond8 = jnp.zeros((8, D_MODEL), F32).at[0].set(c_ctx).at[1:1 + n_lat_b].set(c)
    mod = _modulation(cond8, w_mod, b_mod).reshape(DEPTH, 8, 1, 6 * D_MODEL)
    rope_tabs = _rope_tables(lat_len)
    bdq, bdk = _block_diag_ones(D_ATT), _block_diag_ones(D_KV)
    lat_tiles = lat_len // ROW_TILE
    ctx_row = lambda i: 0
    lat_row = lambda i: 1 + i // lat_tiles

    xp = x_prompt.reshape(n_ctx_b * ctx_len, D_MODEL)
    xs = x_sample.reshape(n_lat_b * lat_len, D_MODEL)
    new_ssm, new_kw, new_vw, new_kg, new_vg = [], [], [], [], []
    for l in range(DEPTH):
        pw, eb, ca, kk = _s5_params(ssm_lam_re[l], ssm_lam_im[l], ssm_log_step[l],
                                    ssm_b_re[l], ssm_b_im[l], ssm_c_re[l], ssm_c_im[l])
        lw = dict(
            w_in=_permute_q_columns(w_in[l]).astype(BF16),
            qn=jnp.tile(q_norm_glb[l], N_HEADS).reshape(1, D_ATT), kn=jnp.tile(k_norm_glb[l], N_KV).reshape(1, D_KV),
            bdq=bdq, bdk=bdk,
            s5_ops=_s5_operators(pw, eb, ca, kk), d_skip=ssm_d[l],
            sink=sink_win[l],
            w_glu=w_glu[l].astype(BF16), w_br_ssm=w_br_ssm[l].astype(BF16), w_br_win=w_br_win[l].astype(BF16),
            w_br_glb=w_br_glb[l].astype(BF16), w_out=w_out[l].astype(BF16),
            ln1_g=ln1_g[l].reshape(1, D_MODEL), ln1_b=ln1_b[l].reshape(1, D_MODEL),
            w_up=w_up[l].astype(BF16), w_down=w_down[l].astype(BF16),
            ln2_g=ln2_g[l].reshape(1, D_MODEL), ln2_b=ln2_b[l].reshape(1, D_MODEL),
        )
        xp, (s_fin, kw, vw, kg, vg) = _layer(xp, lw, mod[l], ctx_row, n_ctx_b, ctx_len, None, None)
        new_ssm.append(jnp.stack(s_fin, axis=1).reshape(n_ctx_b, 2, 2, SSM_GROUPS, SSM_STATE))
        for acc, t in ((new_kw, kw), (new_vw, vw), (new_kg, kg), (new_vg, vg)):
            acc.append(t.reshape(n_ctx_b, ctx_len, N_KV, HEAD_DIM))
        ctx = (state_ssm[:, l], cache_k_win[:, l], cache_v_win[:, l], cache_k_glb[:, l], cache_v_glb[:, l])
        xs, _ = _layer(xs, lw, mod[l], lat_row, n_lat_b, lat_len, ctx, rope_tabs)
    return (xp.reshape(x_prompt.shape), xs.reshape(x_sample.shape),
            jnp.stack(new_ssm, axis=1), jnp.stack(new_kw, axis=1), jnp.stack(new_vw, axis=1),
            jnp.stack(new_kg, axis=1), jnp.stack(new_vg, axis=1))
```

```python
import functools

import jax
import jax.numpy as jnp
import numpy as np
from jax import lax
from jax.experimental import pallas as pl
from jax.experimental.pallas import tpu as pltpu

F32 = jnp.float32
BF16 = jnp.bfloat16

D_MODEL = 1024
DEPTH = 2
GRID_W = 64
HEAD_DIM = 64
D_SSM = 512
SSM_GROUP_CH = 16
SSM_GROUPS = 32
SSM_STATE = 64
N_HEADS = 8
N_KV = 2
GRP = N_HEADS // N_KV
D_ATT = N_HEADS * HEAD_DIM
D_KV = N_KV * HEAD_DIM
WINDOW = 128
ROPE_BASE = 10000.0
D_FF = 4 * D_MODEL
LN_EPS = 1e-5
RMS_EPS = 1e-6
ATTN_SCALE = HEAD_DIM ** -0.5
DEEPNORM_ALPHA = (2.0 * DEPTH) ** 0.25
NEG_INF = -1e30
N_IN = D_SSM + 2 * (D_ATT + 2 * D_KV) + 3 * D_MODEL
O_U = 0
O_QW = O_U + D_SSM
O_KW = O_QW + D_ATT
O_VW = O_KW + D_KV
O_QG = O_VW + D_KV
O_KG = O_QG + D_ATT
O_VG = O_KG + D_KV
O_GATE = O_VG + D_KV

S5_CHUNK = 16
S5_PAIR = 2 * SSM_GROUP_CH
S5_NPAIR = SSM_GROUPS // 2
S5_ROW = S5_CHUNK * S5_PAIR
S5_PSTATE = 2 * SSM_STATE
N_STATE = SSM_GROUPS * SSM_STATE

LANES = 128
ROW_TILE = 512
CHUNKS_PER_TILE = ROW_TILE // S5_CHUNK
LANE_SLABS = D_SSM // LANES
PAIRS_PER_SLAB = LANES // S5_PAIR
VMEM_LIMIT = 56 * 1024 * 1024
NT_DIMS = (((1,), (1,)), ((), ()))


def _cparams(n_axes):
    return pltpu.CompilerParams(dimension_semantics=("arbitrary",) * n_axes, vmem_limit_bytes=VMEM_LIMIT)


def _resident(shape):
    nd = len(shape)
    return pl.BlockSpec(shape, lambda *_: (0,) * nd, pipeline_mode=pl.Buffered(1))


def _dot(a, b):
    return jnp.dot(a, b, preferred_element_type=F32)


def _mod_kernel(c_ref, w_ref, b_ref, o_ref):
    c = c_ref[...]
    a = (c * jax.nn.sigmoid(c)).astype(BF16)
    o_ref[0] = _dot(a, w_ref[0].astype(BF16)) + b_ref[0]


def _modulation(cond8, w_mod, b_mod):
    tn = 512
    n_out = w_mod.shape[-1]
    return pl.pallas_call(
        _mod_kernel,
        grid=(DEPTH, n_out // tn),
        in_specs=[
            pl.BlockSpec((8, D_MODEL), lambda l, n: (0, 0)),
            pl.BlockSpec((1, D_MODEL, tn), lambda l, n: (l, 0, n)),
            pl.BlockSpec((1, 1, tn), lambda l, n: (l, 0, n)),
        ],
        out_specs=pl.BlockSpec((1, 8, tn), lambda l, n: (l, 0, n)),
        out_shape=jax.ShapeDtypeStruct((DEPTH, 8, n_out), F32),
        compiler_params=_cparams(2),
        name="modulation",
    )(cond8, w_mod, b_mod.reshape(DEPTH, 1, n_out))


def _mod_spec(chunk, row_of_tile):
    return pl.BlockSpec((1, 1, D_MODEL), lambda i: (row_of_tile(i), 0, chunk))


def _tokens_to_chunk_rows(u, scr, out_ref):
    n_chunks = u.shape[0] // S5_CHUNK
    for j in range(LANE_SLABS):
        scr[j] = u[:, j * LANES:(j + 1) * LANES]
    pieces = [[None] * S5_CHUNK for _ in range(S5_NPAIR)]
    for j in range(LANE_SLABS):
        for t in range(S5_CHUNK):
            step_rows = scr[j, pl.ds(t, n_chunks, stride=S5_CHUNK), :]
            for m in range(PAIRS_PER_SLAB):
                pieces[j * PAIRS_PER_SLAB + m][t] = step_rows[:, m * S5_PAIR:(m + 1) * S5_PAIR]
    for n in range(S5_NPAIR):
        out_ref[n] = jnp.concatenate(pieces[n], axis=1)


def _chunk_rows_to_tokens(y_ref, scr):
    n_chunks = y_ref.shape[1]
    for j in range(LANE_SLABS):
        for t in range(S5_CHUNK):
            piece = jnp.concatenate(
                [y_ref[j * PAIRS_PER_SLAB + m, :, t * S5_PAIR:(t + 1) * S5_PAIR] for m in range(PAIRS_PER_SLAB)], axis=1)
            scr[j, pl.ds(t, n_chunks, stride=S5_CHUNK), :] = piece
    return jnp.concatenate([scr[j] for j in range(LANE_SLABS)], axis=1)


def _head_rms(x, gain, ones_bd):
    sq = x * x
    hi = sq.astype(BF16)
    lo = (sq - hi.astype(F32)).astype(BF16)
    ss = _dot(hi, ones_bd) + _dot(lo, ones_bd)
    return x * lax.rsqrt(ss * (1.0 / HEAD_DIM) + RMS_EPS) * gain


def _rope_chunk(xc, cos_t, sin_t):
    lane = lax.broadcasted_iota(jnp.int32, xc.shape, 1)
    first_half = (lane & (HEAD_DIM - 1)) < (HEAD_DIM // 2)
    partner = jnp.where(first_half, pltpu.roll(xc, LANES - HEAD_DIM // 2, 1), pltpu.roll(xc, HEAD_DIM // 2, 1))
    return xc * cos_t + partner * sin_t


def _inproj_kernel(*refs, rope):
    if rope:
        (x_ref, sh_ref, sc_ref, w_ref, qn_ref, kn_ref, bdq_ref, bdk_ref, cos_ref, sin_ref,
         u_ref, qw_ref, kw_ref, vw_ref, qg_ref, kg_ref, vg_ref, gate_ref, u_scr) = refs
    else:
        (x_ref, sh_ref, sc_ref, w_ref, qn_ref, kn_ref, bdq_ref, bdk_ref,
         u_ref, qw_ref, kw_ref, vw_ref, qg_ref, kg_ref, vg_ref, gate_ref, u_scr) = refs
    h = (x_ref[...] * (1.0 + sc_ref[0]) + sh_ref[0]).astype(BF16)

    def proj(lo, width):
        return _dot(h, w_ref[:, lo:lo + width])

    def finish(x, out_ref, scale):
        n = x.shape[-1]
        for c in range(n // LANES):
            xc = x[:, c * LANES:(c + 1) * LANES]
            if rope:
                xc = _rope_chunk(xc, cos_ref[...], sin_ref[...])
            if scale != 1.0:
                xc = xc * scale
            out_ref[:, c * LANES:(c + 1) * LANES] = xc.astype(out_ref.dtype)

    _tokens_to_chunk_rows(proj(O_U, D_SSM), u_scr, u_ref)
    finish(proj(O_QW, D_ATT), qw_ref, ATTN_SCALE)
    finish(proj(O_KW, D_KV), kw_ref, 1.0)
    vw_ref[...] = proj(O_VW, D_KV)
    finish(_head_rms(proj(O_QG, D_ATT), qn_ref[...], bdq_ref[...]), qg_ref, ATTN_SCALE)
    finish(_head_rms(proj(O_KG, D_KV), kn_ref[...], bdk_ref[...]), kg_ref, 1.0)
    vg_ref[...] = proj(O_VG, D_KV)
    for c in range(3):
        gate_ref[:, c * D_MODEL:(c + 1) * D_MODEL] = proj(O_GATE + c * D_MODEL, D_MODEL).astype(gate_ref.dtype)


def _in_projection(x, mod_l, row_of_tile, w_in, qn, kn, bdq, bdk, rope_tabs, seq_len):
    n_tok = x.shape[0]
    tm = ROW_TILE
    rope = rope_tabs is not None
    row = lambda w: pl.BlockSpec((tm, w), lambda i: (i, 0))
    in_specs = [
        row(D_MODEL),
        _mod_spec(0, row_of_tile), _mod_spec(1, row_of_tile),
        _resident((D_MODEL, N_IN)),
        _resident((1, D_ATT)), _resident((1, D_KV)),
        _resident((D_ATT, D_ATT)), _resident((D_KV, D_KV)),
    ]
    args = [x, mod_l, mod_l, w_in, qn, kn, bdq, bdk]
    if rope:
        tiles_per_seq = seq_len // tm
        tab = pl.BlockSpec((tm, LANES), lambda i: (i % tiles_per_seq, 0))
        in_specs += [tab, tab]
        args += list(rope_tabs)
    out_shape = [
        jax.ShapeDtypeStruct((S5_NPAIR, n_tok // S5_CHUNK, S5_ROW), F32),
        jax.ShapeDtypeStruct((n_tok, D_ATT), BF16),
        jax.ShapeDtypeStruct((n_tok, D_KV), F32),
        jax.ShapeDtypeStruct((n_tok, D_KV), F32),
        jax.ShapeDtypeStruct((n_tok, D_ATT), BF16),
        jax.ShapeDtypeStruct((n_tok, D_KV), F32),
        jax.ShapeDtypeStruct((n_tok, D_KV), F32),
        jax.ShapeDtypeStruct((n_tok, 3 * D_MODEL), BF16),
    ]
    out_specs = ([pl.BlockSpec((S5_NPAIR, CHUNKS_PER_TILE, S5_ROW), lambda i: (0, i, 0))]
                 + [row(s.shape[1]) for s in out_shape[1:]])
    return pl.pallas_call(
        functools.partial(_inproj_kernel, rope=rope),
        grid=(n_tok // tm,),
        in_specs=in_specs,
        out_specs=out_specs,
        out_shape=out_shape,
        scratch_shapes=[pltpu.VMEM((LANE_SLABS, tm, LANES), F32)],
        compiler_params=_cparams(1),
        name="in_projection",
    )(*args)


def _zoh(lr, li, ls):
    dt = jnp.exp(ls)
    mag = jnp.exp(lr * dt)
    ar, ai = mag * jnp.cos(li * dt), mag * jnp.sin(li * dt)
    den = lr * lr + li * li
    fr = ((ar - 1.0) * lr + ai * li) / den
    fi = (ai * lr - (ar - 1.0) * li) / den
    return ar, ai, fr, fi


def _powers(ar, ai, n):
    out = [(jnp.ones_like(ar), jnp.zeros_like(ar))]
    for _ in range(n):
        pr, pi = out[-1]
        out.append((pr * ar - pi * ai, pr * ai + pi * ar))
    return out


def _s5_operator_kernel(lr_ref, li_ref, ls_ref, lrc_ref, lic_ref, lsc_ref,
                        btr_ref, bti_ref, cr_ref, ci_ref, ctr_ref, cti_ref,
                        top_ref, e0, e1, e2, e3, m0, m1, m2, m3, d0, d1, d2, d3, ext_scr, m_scr):
    tc, cg, p = S5_CHUNK, SSM_GROUP_CH, SSM_STATE
    ends = ((e0, e1), (e2, e3))
    decays = ((d0, d1), (d2, d3))
    ext_scr[...] = jnp.zeros(ext_scr.shape, F32)
    m_scr[...] = jnp.zeros(m_scr.shape, F32)
    zero_half = jnp.zeros((cg, p), F32)
    for d in range(2):
        for g2 in range(2):
            ar, ai, fr, fi = _zoh(lr_ref[d, 0, g2], li_ref[d, 0, g2], ls_ref[d, 0, g2])
            btr, bti = btr_ref[d, 0, g2], bti_ref[d, 0, g2]
            bbr, bbi = fr * btr - fi * bti, fr * bti + fi * btr
            cr, ci = cr_ref[d, 0, g2], ci_ref[d, 0, g2]
            pw = _powers(ar, ai, tc)
            group_rows = slice(g2 * cg, (g2 + 1) * cg)
            for j in range(tc):
                pr, pi = pw[j]
                t = tc - 1 - j if d == 0 else j
                rows = slice(t * S5_PAIR + g2 * cg, t * S5_PAIR + (g2 + 1) * cg)
                for part, val in enumerate((pr * bbr - pi * bbi, pr * bbi + pi * bbr)):
                    halves = [val, zero_half] if g2 == 0 else [zero_half, val]
                    ends[d][part][0, rows, :] = jnp.concatenate(halves, axis=1).astype(BF16)
                car, cai = cr * pr - ci * pi, cr * pi + ci * pr
                kt = (lax.dot_general(bbr, car, NT_DIMS, precision=lax.Precision.HIGHEST, preferred_element_type=F32)
                      - lax.dot_general(bbi, cai, NT_DIMS, precision=lax.Precision.HIGHEST, preferred_element_type=F32))
                slot = tc - 1 + j if d == 0 else tc - 1 - j
                lanes = slice(slot * S5_PAIR + g2 * cg, slot * S5_PAIR + (g2 + 1) * cg)
                if d == 1 and j == 0:
                    ext_scr[group_rows, lanes] = ext_scr[group_rows, lanes] + kt
                else:
                    ext_scr[group_rows, lanes] = kt
            for part in range(2):
                decays[d][part][:, g2 * p:(g2 + 1) * p] = pw[tc][part]
            arc, aic, _, _ = _zoh(lrc_ref[d, 0, g2], lic_ref[d, 0, g2], lsc_ref[d, 0, g2])
            pwc = _powers(arc, aic, tc)
            ctr, cti = ctr_ref[d, 0, g2], cti_ref[d, 0, g2]
            state_rows = slice(g2 * p, (g2 + 1) * p)
            for j in range(1, tc + 1):
                prc, pic = pwc[j]
                t = j - 1 if d == 0 else tc - j
                lanes = slice(t * S5_PAIR + g2 * cg, t * S5_PAIR + (g2 + 1) * cg)
                m_scr[2 * d, state_rows, lanes] = ctr * prc - cti * pic
                m_scr[2 * d + 1, state_rows, lanes] = -(ctr * pic + cti * prc)
    for t in range(tc):
        for g2 in range(2):
            rows = slice(t * S5_PAIR + g2 * cg, t * S5_PAIR + (g2 + 1) * cg)
            window = slice((tc - 1 - t) * S5_PAIR, (tc - 1 - t) * S5_PAIR + S5_ROW)
            top_ref[0, rows, :] = ext_scr[g2 * cg:(g2 + 1) * cg, window].astype(BF16)
    for k, m_ref in enumerate((m0, m1, m2, m3)):
        m_ref[0] = m_scr[k].astype(BF16)


def _s5_operators(lam_re, lam_im, log_step, b_re, b_im, c_re, c_im):
    g, p, cg = SSM_GROUPS, SSM_STATE, SSM_GROUP_CH
    ls = jnp.broadcast_to(log_step[:, :, None], (2, g, p))

    def rows(v):
        return v.reshape(2, S5_NPAIR, 2, 1, p)

    def cols(v):
        return v.reshape(2, S5_NPAIR, 2, p, 1)

    def mat(v, a, b):
        return v.reshape(2, S5_NPAIR, 2, a, b)

    def spec(*tail):
        return pl.BlockSpec((2, 1, 2) + tail, lambda n: (0, n, 0, 0, 0))

    out = pl.pallas_call(
        _s5_operator_kernel,
        grid=(S5_NPAIR,),
        in_specs=[spec(1, p)] * 3 + [spec(p, 1)] * 3 + [spec(cg, p)] * 4 + [spec(p, cg)] * 2,
        out_specs=([pl.BlockSpec((1, S5_ROW, S5_ROW), lambda n: (n, 0, 0))]
                   + [pl.BlockSpec((1, S5_ROW, S5_PSTATE), lambda n: (n, 0, 0))] * 4
                   + [pl.BlockSpec((1, S5_PSTATE, S5_ROW), lambda n: (n, 0, 0))] * 4
                   + [pl.BlockSpec((1, S5_PSTATE), lambda n: (0, n))] * 4),
        out_shape=([jax.ShapeDtypeStruct((S5_NPAIR, S5_ROW, S5_ROW), BF16)]
                   + [jax.ShapeDtypeStruct((S5_NPAIR, S5_ROW, S5_PSTATE), BF16)] * 4
                   + [jax.ShapeDtypeStruct((S5_NPAIR, S5_PSTATE, S5_ROW), BF16)] * 4
                   + [jax.ShapeDtypeStruct((1, N_STATE), F32)] * 4),
        scratch_shapes=[pltpu.VMEM((S5_PAIR, 2 * S5_ROW), F32), pltpu.VMEM((4, S5_PSTATE, S5_ROW), F32)],
        compiler_params=_cparams(1),
        name="s5_operators",
    )(rows(lam_re), rows(lam_im), rows(ls), cols(lam_re), cols(lam_im), cols(ls),
      mat(jnp.swapaxes(b_re, -1, -2), cg, p), mat(jnp.swapaxes(b_im, -1, -2), cg, p), mat(c_re, cg, p), mat(c_im, cg, p),
      mat(jnp.swapaxes(c_re, -1, -2), p, cg), mat(jnp.swapaxes(c_im, -1, -2), p, cg))
    return out[0], out[1:5], out[5:9], out[9:13]


def _s5_ends_kernel(u_ref, e0, e1, e2, e3, o0, o1, o2, o3):
    u = u_ref[0].astype(BF16)
    for e_ref, o_ref in ((e0, o0), (e1, o1), (e2, o2), (e3, o3)):
        o_ref[...] = _dot(u, e_ref[0])


def _s5_ends(u_rows, ends, rt):
    n_rows = u_rows.shape[1]
    e_spec = pl.BlockSpec((1, S5_ROW, S5_PSTATE), lambda n, r: (n, 0, 0))
    o_spec = pl.BlockSpec((rt, S5_PSTATE), lambda n, r: (r, n))
    return pl.pallas_call(
        _s5_ends_kernel,
        grid=(S5_NPAIR, n_rows // rt),
        in_specs=[pl.BlockSpec((1, rt, S5_ROW), lambda n, r: (n, r, 0))] + [e_spec] * 4,
        out_specs=[o_spec] * 4,
        out_shape=[jax.ShapeDtypeStruct((n_rows, N_STATE), F32)] * 4,
        compiler_params=_cparams(2),
        name="s5_chunk_ends",
    )(u_rows, *ends)


def _s5_scan_kernel(efr, efi, ebr, ebi, afr, afi, abr, abi, sfr, sfi, sbr, sbi,
                    pfr, pfi, nbr, nbi, ofr, ofi, obr, obi):
    nc = efr.shape[1]
    a_fr, a_fi, a_br, a_bi = afr[...], afi[...], abr[...], abi[...]

    def body(c, carry):
        fr, fi, br, bi = carry
        rc, rb = pl.ds(c, 1), pl.ds(nc - 1 - c, 1)
        pfr[0, rc, :] = fr
        pfi[0, rc, :] = fi
        nbr[0, rb, :] = br
        nbi[0, rb, :] = bi
        nfr = a_fr * fr - a_fi * fi + efr[0, rc, :]
        nfi = a_fr * fi + a_fi * fr + efi[0, rc, :]
        nbr_ = a_br * br - a_bi * bi + ebr[0, rb, :]
        nbi_ = a_br * bi + a_bi * br + ebi[0, rb, :]
        return nfr, nfi, nbr_, nbi_

    fr, fi, br, bi = lax.fori_loop(0, nc, body, (sfr[0], sfi[0], sbr[0], sbi[0]))
    ofr[0] = fr
    ofi[0] = fi
    obr[0] = br
    obi[0] = bi


def _s5_scan(se, decay, s0, nc, nb):
    cw = 1024
    seq = pl.BlockSpec((1, nc, cw), lambda b, i: (b, 0, i))
    vec = pl.BlockSpec((1, cw), lambda b, i: (0, i))
    st = pl.BlockSpec((1, 1, cw), lambda b, i: (b, 0, i))
    return pl.pallas_call(
        _s5_scan_kernel,
        grid=(nb, N_STATE // cw),
        in_specs=[seq] * 4 + [vec] * 4 + [st] * 4,
        out_specs=[seq] * 4 + [st] * 4,
        out_shape=[jax.ShapeDtypeStruct((nb, nc, N_STATE), F32)] * 4 + [jax.ShapeDtypeStruct((nb, 1, N_STATE), F32)] * 4,
        compiler_params=_cparams(2),
        name="s5_chunk_scan",
    )(*[s.reshape(nb, nc, N_STATE) for s in se], *decay, *[s.reshape(nb, 1, N_STATE) for s in s0])


def _s5_out_kernel(u_ref, top_ref, d_ref, pfr, pfi, nbr, nbi, m0, m1, m2, m3, o_ref):
    u = u_ref[0]
    y = _dot(u.astype(BF16), top_ref[0]) + d_ref[0] * u
    for s_ref, m_ref in ((pfr, m0), (pfi, m1), (nbr, m2), (nbi, m3)):
        y = y + _dot(s_ref[...].astype(BF16), m_ref[0])
    o_ref[0] = jax.nn.gelu(y)


def _s5_out(u_rows, top, d_rows, states, carries, rt):
    n_rows = u_rows.shape[1]
    s_spec = pl.BlockSpec((rt, S5_PSTATE), lambda n, r: (r, n))
    m_spec = pl.BlockSpec((1, S5_PSTATE, S5_ROW), lambda n, r: (n, 0, 0))
    return pl.pallas_call(
        _s5_out_kernel,
        grid=(S5_NPAIR, n_rows // rt),
        in_specs=[pl.BlockSpec((1, rt, S5_ROW), lambda n, r: (n, r, 0)),
                  pl.BlockSpec((1, S5_ROW, S5_ROW), lambda n, r: (n, 0, 0)),
                  pl.BlockSpec((1, 1, S5_ROW), lambda n, r: (n, 0, 0))] + [s_spec] * 4 + [m_spec] * 4,
        out_specs=pl.BlockSpec((1, rt, S5_ROW), lambda n, r: (n, r, 0)),
        out_shape=jax.ShapeDtypeStruct(u_rows.shape, F32),
        compiler_params=_cparams(2),
        name="s5_outputs",
    )(u_rows, top, d_rows, *states, *carries)


def _s5_branch(u_rows, ops, d_skip, s0, nb, seq_len):
    top, ends, carries, decay = ops
    nc = seq_len // S5_CHUNK
    n_rows = nc * nb
    rt = min(n_rows, 256)
    d_rows = jnp.tile(d_skip.reshape(S5_NPAIR, 1, S5_PAIR), (1, S5_CHUNK, 1)).reshape(S5_NPAIR, 1, S5_ROW)
    se = _s5_ends(u_rows, ends, rt)
    scanned = _s5_scan(se, decay, s0, nc, nb)
    states = [s.reshape(n_rows, N_STATE) for s in scanned[:4]]
    y_rows = _s5_out(u_rows, top, d_rows, states, carries, rt)
    return y_rows, [s.reshape(nb, N_STATE) for s in scanned[4:]]


Q_SLOT_ORDER = tuple(h for j in range(GRP) for h in (j, GRP + j))


def _stack_group_queries(q_ref, h, tq):
    lane = lax.broadcasted_iota(jnp.int32, (tq, LANES), 1)
    keep = (lane >= h * HEAD_DIM) & (lane < (h + 1) * HEAD_DIM)
    zero = jnp.zeros((tq, LANES), BF16)
    return jnp.concatenate([jnp.where(keep, q_ref[:, j * LANES:(j + 1) * LANES], zero) for j in range(GRP)], axis=0)


def _sink_lanes(sink_ref, h, tq):
    return jnp.concatenate([jnp.full((1, tq), sink_ref[h * GRP + j], F32) for j in range(GRP)], axis=1)


def _store_heads(o_ref, h, o_t, tq):
    for j in range(GRP):
        head = h * GRP + j
        o_ref[0, head * HEAD_DIM:(head + 1) * HEAD_DIM, :] = o_t[:, j * tq:(j + 1) * tq].astype(o_ref.dtype)


def _attn_full_kernel(*refs, tq, use_sink):
    if use_sink:
        sink_ref, q_ref, k_ref, vt_ref, o_ref, m_ref, l_ref, acc_ref = refs
    else:
        q_ref, k_ref, vt_ref, o_ref, m_ref, l_ref, acc_ref = refs
    n_chunks = k_ref.shape[1]
    for h in range(N_KV):
        qs = _stack_group_queries(q_ref, h, tq)
        if use_sink:
            m_ref[...] = _sink_lanes(sink_ref, h, tq)
            l_ref[...] = jnp.ones(l_ref.shape, F32)
        else:
            m_ref[...] = jnp.full(m_ref.shape, NEG_INF, F32)
            l_ref[...] = jnp.zeros(l_ref.shape, F32)
        acc_ref[...] = jnp.zeros(acc_ref.shape, F32)

        def body(c, carry):
            s = lax.dot_general(k_ref[0, c], qs, NT_DIMS, preferred_element_type=F32)
            m_old = m_ref[...]
            m_new = jnp.maximum(m_old, jnp.max(s, axis=0, keepdims=True))
            p = jnp.exp(s - m_new)
            alpha = jnp.exp(m_old - m_new)
            l_ref[...] = alpha * l_ref[...] + jnp.sum(p, axis=0, keepdims=True)
            vt = vt_ref[0, c, h * HEAD_DIM:(h + 1) * HEAD_DIM, :]
            acc_ref[...] = alpha * acc_ref[...] + _dot(vt, p.astype(BF16))
            m_ref[...] = m_new
            return carry

        lax.fori_loop(0, n_chunks, body, 0)
        _store_heads(o_ref, h, acc_ref[...] / l_ref[...], tq)


def _attn_band_kernel(sink_ref, q_ref, k_ref, vt_ref, o_ref, *, tq, n_lat, n_ctx):
    i = pl.program_id(1)
    n_band = tq // WINDOW + 2
    lat_chunks, ctx_chunks = n_lat // WINDOW, n_ctx // WINDOW
    c0 = jnp.clip(i * (tq // WINDOW) - 1, 0, lat_chunks - n_band)
    rows = GRP * tq
    row = lax.broadcasted_iota(jnp.int32, (n_band * WINDOW, rows), 0)
    col = lax.broadcasted_iota(jnp.int32, (n_band * WINDOW, rows), 1)
    in_band = jnp.abs((c0 * WINDOW + row) - (i * tq + (col & (tq - 1)))) <= WINDOW
    for h in range(N_KV):
        qs = _stack_group_queries(q_ref, h, tq)
        k_band = k_ref[0, pl.ds(c0, n_band)].reshape(n_band * WINDOW, LANES)
        s_band = jnp.where(in_band, lax.dot_general(k_band, qs, NT_DIMS, preferred_element_type=F32), NEG_INF)
        k_ctx = k_ref[0, lat_chunks:lat_chunks + ctx_chunks].reshape(n_ctx, LANES)
        s_ctx = lax.dot_general(k_ctx, qs, NT_DIMS, preferred_element_type=F32)
        sink = _sink_lanes(sink_ref, h, tq)
        m = jnp.maximum(jnp.maximum(jnp.max(s_band, axis=0, keepdims=True), jnp.max(s_ctx, axis=0, keepdims=True)), sink)
        p_band = jnp.exp(s_band - m)
        p_ctx = jnp.exp(s_ctx - m)
        den = jnp.sum(p_band, axis=0, keepdims=True) + jnp.sum(p_ctx, axis=0, keepdims=True) + jnp.exp(sink - m)
        p_band, p_ctx = p_band.astype(BF16), p_ctx.astype(BF16)
        hd = slice(h * HEAD_DIM, (h + 1) * HEAD_DIM)
        acc = jnp.zeros((HEAD_DIM, rows), F32)
        for t in range(n_band):
            acc = acc + _dot(vt_ref[0, c0 + t, hd, :], p_band[t * WINDOW:(t + 1) * WINDOW])
        for t in range(ctx_chunks):
            acc = acc + _dot(vt_ref[0, lat_chunks + t, hd, :], p_ctx[t * WINDOW:(t + 1) * WINDOW])
        _store_heads(o_ref, h, acc / den, tq)


def _attention(q, k, v, sink, nb, seq_len, tq, tk, band_ctx=None):
    n_k = k.shape[1]
    kc = k.reshape(nb, n_k // tk, tk, D_KV)
    vt = v.reshape(nb, n_k // tk, tk, D_KV).transpose(0, 1, 3, 2)
    tiles = seq_len // tq
    q_spec = pl.BlockSpec((tq, D_ATT), lambda b, i: (b * tiles + i, 0))
    k_spec = pl.BlockSpec((1, n_k // tk, tk, D_KV), lambda b, i: (b, 0, 0, 0))
    vt_spec = pl.BlockSpec((1, n_k // tk, D_KV, tk), lambda b, i: (b, 0, 0, 0))
    in_specs, args = [q_spec, k_spec, vt_spec], [q, kc, vt]
    if sink is not None:
        in_specs, args = [pl.BlockSpec(memory_space=pltpu.SMEM)] + in_specs, [sink] + args
    rows = GRP * tq
    if band_ctx is None:
        body = functools.partial(_attn_full_kernel, tq=tq, use_sink=sink is not None)
        scratch = [pltpu.VMEM((1, rows), F32), pltpu.VMEM((1, rows), F32), pltpu.VMEM((HEAD_DIM, rows), F32)]
        name = "attention_full"
    else:
        body = functools.partial(_attn_band_kernel, tq=tq, n_lat=seq_len, n_ctx=band_ctx)
        scratch = []
        name = "attention_band"
    o_t = pl.pallas_call(
        body,
        grid=(nb, tiles),
        in_specs=in_specs,
        out_specs=pl.BlockSpec((1, D_ATT, tq), lambda b, i: (b, 0, i)),
        out_shape=jax.ShapeDtypeStruct((nb, D_ATT, seq_len), BF16),
        scratch_shapes=scratch,
        compiler_params=_cparams(2),
        name=name,
    )(*args)
    return o_t.transpose(0, 2, 1).reshape(nb * seq_len, D_ATT)


def _layer_norm(z, g, b):
    mu = jnp.mean(z, axis=-1, keepdims=True)
    zc = z - mu
    var = jnp.mean(zc * zc, axis=-1, keepdims=True)
    return zc * lax.rsqrt(var + LN_EPS) * g + b


def _merge_kernel(x_ref, g1_ref, ya_ref, yw_ref, yg_ref, gate_ref,
                  wglu_ref, wa_ref, ww_ref, wg_ref, wout_ref, lng_ref, lnb_ref, o_ref, ya_scr):
    ya = _chunk_rows_to_tokens(ya_ref, ya_scr)
    ya = ya * jax.nn.sigmoid(_dot(ya.astype(BF16), wglu_ref[...]))

    def gate(c):
        return jax.nn.sigmoid(gate_ref[:, c * D_MODEL:(c + 1) * D_MODEL].astype(F32))

    m = (gate(0) * _dot(ya.astype(BF16), wa_ref[...])
         + gate(1) * _dot(yw_ref[...], ww_ref[...])
         + gate(2) * _dot(yg_ref[...], wg_ref[...]))
    f = _dot(m.astype(BF16), wout_ref[...])
    o_ref[...] = _layer_norm(DEEPNORM_ALPHA * x_ref[...] + g1_ref[0] * f, lng_ref[...], lnb_ref[...])


def _merge(x, mod_l, row_of_tile, ya_rows, yw, yg, gates, w_glu, w_a, w_w, w_g, w_out, ln_g, ln_b):
    n_tok = x.shape[0]
    tm = ROW_TILE
    row = lambda w: pl.BlockSpec((tm, w), lambda i: (i, 0))
    return pl.pallas_call(
        _merge_kernel,
        grid=(n_tok // tm,),
        in_specs=[row(D_MODEL), _mod_spec(2, row_of_tile),
                  pl.BlockSpec((S5_NPAIR, CHUNKS_PER_TILE, S5_ROW), lambda i: (0, i, 0)),
                  row(D_ATT), row(D_ATT), row(3 * D_MODEL),
                  _resident((D_SSM, D_SSM)), _resident((D_SSM, D_MODEL)), _resident((D_ATT, D_MODEL)),
                  _resident((D_ATT, D_MODEL)), _resident((D_MODEL, D_MODEL)),
                  _resident((1, D_MODEL)), _resident((1, D_MODEL))],
        out_specs=row(D_MODEL),
        out_shape=jax.ShapeDtypeStruct((n_tok, D_MODEL), F32),
        scratch_shapes=[pltpu.VMEM((LANE_SLABS, tm, LANES), F32)],
        compiler_params=_cparams(1),
        name="merge_residual",
    )(x, mod_l, ya_rows, yw, yg, gates, w_glu, w_a, w_w, w_g, w_out, ln_g, ln_b)


def _mlp_kernel(x_ref, sh_ref, sc_ref, g2_ref, wup_ref, wdn_ref, lng_ref, lnb_ref, o_ref):
    x = x_ref[...]
    h = (x * (1.0 + sc_ref[0]) + sh_ref[0]).astype(BF16)
    ff_tile = 1024
    acc = jnp.zeros(x.shape, F32)
    for c in range(D_FF // ff_tile):
        up = jnp.maximum(_dot(h, wup_ref[:, c * ff_tile:(c + 1) * ff_tile]), 0.0)
        acc = acc + _dot((up * up).astype(BF16), wdn_ref[c * ff_tile:(c + 1) * ff_tile, :])
    o_ref[...] = _layer_norm(DEEPNORM_ALPHA * x + g2_ref[0] * acc, lng_ref[...], lnb_ref[...])


def _mlp(x, mod_l, row_of_tile, w_up, w_down, ln_g, ln_b):
    n_tok = x.shape[0]
    tm = ROW_TILE
    row = pl.BlockSpec((tm, D_MODEL), lambda i: (i, 0))
    return pl.pallas_call(
        _mlp_kernel,
        grid=(n_tok // tm,),
        in_specs=[row, _mod_spec(3, row_of_tile), _mod_spec(4, row_of_tile), _mod_spec(5, row_of_tile),
                  _resident((D_MODEL, D_FF)), _resident((D_FF, D_MODEL)),
                  _resident((1, D_MODEL)), _resident((1, D_MODEL))],
        out_specs=row,
        out_shape=jax.ShapeDtypeStruct((n_tok, D_MODEL), F32),
        compiler_params=_cparams(1),
        name="mlp_residual",
    )(x, mod_l, mod_l, mod_l, w_up, w_down, ln_g, ln_b)


def _rope_tables(n_tok):
    rows = n_tok // GRID_W
    row = jnp.repeat(jnp.arange(rows, dtype=F32), GRID_W)
    col = jnp.tile(jnp.arange(GRID_W, dtype=F32), rows)
    n_freq = HEAD_DIM // 4
    inv = ROPE_BASE ** (-jnp.arange(n_freq, dtype=F32) / n_freq)
    ang = jnp.concatenate([row[:, None] * inv, col[:, None] * inv], axis=-1)
    cos, sin = jnp.cos(ang), jnp.sin(ang)
    cos_t = jnp.tile(jnp.concatenate([cos, cos], axis=-1), (1, LANES // HEAD_DIM))
    sin_t = jnp.tile(jnp.concatenate([-sin, sin], axis=-1), (1, LANES // HEAD_DIM))
    return cos_t, sin_t


def _block_diag_ones(n):
    idx = np.arange(n) // HEAD_DIM
    return jnp.asarray(idx[:, None] == idx[None, :], dtype=BF16)


def _permute_q_columns(w):
    def perm(block):
        return block.reshape(D_MODEL, N_HEADS, HEAD_DIM)[:, np.array(Q_SLOT_ORDER), :].reshape(D_MODEL, D_ATT)
    return jnp.concatenate([w[:, :O_QW], perm(w[:, O_QW:O_KW]), w[:, O_KW:O_QG], perm(w[:, O_QG:O_KG]), w[:, O_KG:]], axis=1)


def _layer(x, lw, mod_l, row_of_tile, nb, seq_len, ctx, rope_tabs):
    u_rows, qw, kw, vw, qg, kg, vg, gates = _in_projection(
        x, mod_l, row_of_tile, lw['w_in'], lw['qn'], lw['kn'], lw['bdq'], lw['bdk'], rope_tabs, seq_len)
    if ctx is None:
        s0 = [jnp.zeros((nb, N_STATE), F32)] * 4
    else:
        s0 = [ctx[0][:, d, part].reshape(nb, N_STATE) for d in (0, 1) for part in (0, 1)]
    ya_rows, s_fin = _s5_branch(u_rows, lw['s5_ops'], lw['d_skip'], s0, nb, seq_len)

    def seq(t, n):
        return t.reshape(nb, n, D_KV)

    if ctx is None:
        yw = _attention(qw, seq(kw, seq_len).astype(BF16), seq(vw, seq_len).astype(BF16), lw['sink'],
                        nb, seq_len, seq_len, seq_len)
        yg = _attention(qg, seq(kg, seq_len).astype(BF16), seq(vg, seq_len).astype(BF16), None,
                        nb, seq_len, seq_len, seq_len)
        new_ctx = (s_fin, kw, vw, kg, vg)
    else:
        _, k_wc, v_wc, k_gc, v_gc = ctx
        n_ctx = k_wc.shape[1]

        def with_ctx(new, cached):
            return jnp.concatenate([seq(new, seq_len), seq(cached, n_ctx)], axis=1).astype(BF16)

        yw = _attention(qw, with_ctx(kw, k_wc), with_ctx(vw, v_wc), lw['sink'], nb, seq_len, 256, WINDOW, band_ctx=n_ctx)
        yg = _attention(qg, with_ctx(kg, k_gc), with_ctx(vg, v_gc), None, nb, seq_len, 256, 256)
        new_ctx = None
    x1 = _merge(x, mod_l, row_of_tile, ya_rows, yw, yg, gates,
                lw['w_glu'], lw['w_br_ssm'], lw['w_br_win'], lw['w_br_glb'], lw['w_out'], lw['ln1_g'], lw['ln1_b'])
    x2 = _mlp(x1, mod_l, row_of_tile, lw['w_up'], lw['w_down'], lw['ln2_g'], lw['ln2_b'])
    return x2, new_ctx


def kernel(x_prompt, x_sample, state_ssm, cache_k_win, cache_v_win, cache_k_glb, cache_v_glb, c, c_ctx, w_mod, b_mod, w_in, ssm_lam_re, ssm_lam_im, ssm_log_step, ssm_b_re, ssm_b_im, ssm_c_re, ssm_c_im, ssm_d, w_glu, sink_win, q_norm_glb, k_norm_glb, w_br_ssm, w_br_win, w_br_glb, w_out, ln1_g, ln1_b, w_up, w_down, ln2_g, ln2_b):
    n_ctx_b, ctx_len, _ = x_prompt.shape
    n_lat_b, lat_len, _ = x_sample.shape
    assert ctx_len % ROW_TILE == 0 or ROW_TILE % ctx_len == 0
    assert lat_len % ROW_TILE == 0 and (n_ctx_b * ctx_len) % ROW_TILE == 0

    cond8 = jnp.zeros((8, D_MODEL), F32).at[0].set(c_ctx).at[1:1 + n_lat_b].set(c)
    mod = _modulation(cond8, w_mod, b_mod).reshape(DEPTH, 8, 1, 6 * D_MODEL)
    rope_tabs = _rope_tables(lat_len)
    bdq, bdk = _block_diag_ones(D_ATT), _block_diag_ones(D_KV)
    lat_tiles = lat_len // ROW_TILE
    ctx_row = lambda i: 0
    lat_row = lambda i: 1 + i // lat_tiles

    xp = x_prompt.reshape(n_ctx_b * ctx_len, D_MODEL)
    xs = x_sample.reshape(n_lat_b * lat_len, D_MODEL)
    new_ssm, new_kw, new_vw, new_kg, new_vg = [], [], [], [], []
    for l in range(DEPTH):
        lw = dict(
            w_in=_permute_q_columns(w_in[l]).astype(BF16),
            qn=jnp.tile(q_norm_glb[l], N_HEADS).reshape(1, D_ATT), kn=jnp.tile(k_norm_glb[l], N_KV).reshape(1, D_KV),
            bdq=bdq, bdk=bdk,
            s5_ops=_s5_operators(ssm_lam_re[l], ssm_lam_im[l], ssm_log_step[l],
                                 ssm_b_re[l], ssm_b_im[l], ssm_c_re[l], ssm_c_im[l]),
            d_skip=ssm_d[l],
            sink=sink_win[l],
            w_glu=w_glu[l].astype(BF16), w_br_ssm=w_br_ssm[l].astype(BF16), w_br_win=w_br_win[l].astype(BF16),
            w_br_glb=w_br_glb[l].astype(BF16), w_out=w_out[l].astype(BF16),
            ln1_g=ln1_g[l].reshape(1, D_MODEL), ln1_b=ln1_b[l].reshape(1, D_MODEL),
            w_up=w_up[l].astype(BF16), w_down=w_down[l].astype(BF16),
            ln2_g=ln2_g[l].reshape(1, D_MODEL), ln2_b=ln2_b[l].reshape(1, D_MODEL),
        )
        xp, (s_fin, kw, vw, kg, vg) = _layer(xp, lw, mod[l], ctx_row, n_ctx_b, ctx_len, None, None)
        new_ssm.append(jnp.stack(s_fin, axis=1).reshape(n_ctx_b, 2, 2, SSM_GROUPS, SSM_STATE))
        for acc, t in ((new_kw, kw), (new_vw, vw), (new_kg, kg), (new_vg, vg)):
            acc.append(t.reshape(n_ctx_b, ctx_len, N_KV, HEAD_DIM))
        ctx = (state_ssm[:, l], cache_k_win[:, l], cache_v_win[:, l], cache_k_glb[:, l], cache_v_glb[:, l])
        xs, _ = _layer(xs, lw, mod[l], lat_row, n_lat_b, lat_len, ctx, rope_tabs)
    return (xp.reshape(x_prompt.shape), xs.reshape(x_sample.shape),
            jnp.stack(new_ssm, axis=1), jnp.stack(new_kw, axis=1), jnp.stack(new_vw, axis=1),
            jnp.stack(new_kg, axis=1), jnp.stack(new_vg, axis=1))
```

```python
import functools

import jax
import jax.numpy as jnp
import numpy as np
from jax import lax
from jax.experimental import pallas as pl
from jax.experimental.pallas import tpu as pltpu

F32 = jnp.float32
BF16 = jnp.bfloat16

D_MODEL = 1024
DEPTH = 2
GRID_W = 64
HEAD_DIM = 64
D_SSM = 512
SSM_GROUP_CH = 16
SSM_GROUPS = 32
SSM_STATE = 64
N_HEADS = 8
N_KV = 2
GRP = N_HEADS // N_KV
D_ATT = N_HEADS * HEAD_DIM
D_KV = N_KV * HEAD_DIM
WINDOW = 128
ROPE_BASE = 10000.0
D_FF = 4 * D_MODEL
LN_EPS = 1e-5
RMS_EPS = 1e-6
ATTN_SCALE = HEAD_DIM ** -0.5
DEEPNORM_ALPHA = (2.0 * DEPTH) ** 0.25
NEG_INF = -1e30
LOG2E = 1.4426950408889634
Q_SCALE = ATTN_SCALE * LOG2E
V_ROWS = HEAD_DIM + 16
N_IN = D_SSM + 2 * (D_ATT + 2 * D_KV) + 3 * D_MODEL
O_U = 0
O_QW = O_U + D_SSM
O_KW = O_QW + D_ATT
O_VW = O_KW + D_KV
O_QG = O_VW + D_KV
O_KG = O_QG + D_ATT
O_VG = O_KG + D_KV
O_GATE = O_VG + D_KV

S5_CHUNK = 16
S5_PAIR = 2 * SSM_GROUP_CH
S5_NPAIR = SSM_GROUPS // 2
S5_ROW = S5_CHUNK * S5_PAIR
S5_PSTATE = 2 * SSM_STATE
N_STATE = SSM_GROUPS * SSM_STATE

LANES = 128
ROW_TILE = 512
CHUNKS_PER_TILE = ROW_TILE // S5_CHUNK
LANE_SLABS = D_SSM // LANES
PAIRS_PER_SLAB = LANES // S5_PAIR
VMEM_LIMIT = 56 * 1024 * 1024
NT_DIMS = (((1,), (1,)), ((), ()))


def _cparams(n_axes):
    return pltpu.CompilerParams(dimension_semantics=("arbitrary",) * n_axes, vmem_limit_bytes=VMEM_LIMIT)


def _resident(shape):
    nd = len(shape)
    return pl.BlockSpec(shape, lambda *_: (0,) * nd, pipeline_mode=pl.Buffered(1))


def _dot(a, b):
    return jnp.dot(a, b, preferred_element_type=F32)


def _mod_kernel(c_ref, w_ref, b_ref, o_ref):
    c = c_ref[...]
    a = (c * jax.nn.sigmoid(c)).astype(BF16)
    o_ref[0] = _dot(a, w_ref[0].astype(BF16)) + b_ref[0]


def _modulation(cond8, w_mod, b_mod):
    tn = 512
    n_out = w_mod.shape[-1]
    return pl.pallas_call(
        _mod_kernel,
        grid=(DEPTH, n_out // tn),
        in_specs=[
            pl.BlockSpec((8, D_MODEL), lambda l, n: (0, 0)),
            pl.BlockSpec((1, D_MODEL, tn), lambda l, n: (l, 0, n)),
            pl.BlockSpec((1, 1, tn), lambda l, n: (l, 0, n)),
        ],
        out_specs=pl.BlockSpec((1, 8, tn), lambda l, n: (l, 0, n)),
        out_shape=jax.ShapeDtypeStruct((DEPTH, 8, n_out), F32),
        compiler_params=_cparams(2),
        name="modulation",
    )(cond8, w_mod, b_mod.reshape(DEPTH, 1, n_out))


def _mod_spec(chunk, row_of_tile):
    return pl.BlockSpec((1, 1, D_MODEL), lambda i: (row_of_tile(i), 0, chunk))


def _tokens_to_chunk_rows(u, scr, out_ref):
    n_chunks = u.shape[0] // S5_CHUNK
    for j in range(LANE_SLABS):
        scr[j] = u[:, j * LANES:(j + 1) * LANES]
    pieces = [[None] * S5_CHUNK for _ in range(S5_NPAIR)]
    for j in range(LANE_SLABS):
        for t in range(S5_CHUNK):
            step_rows = scr[j, pl.ds(t, n_chunks, stride=S5_CHUNK), :]
            for m in range(PAIRS_PER_SLAB):
                pieces[j * PAIRS_PER_SLAB + m][t] = step_rows[:, m * S5_PAIR:(m + 1) * S5_PAIR]
    for n in range(S5_NPAIR):
        out_ref[n] = jnp.concatenate(pieces[n], axis=1)


def _chunk_rows_to_tokens(y_ref, scr):
    n_chunks = y_ref.shape[1]
    for j in range(LANE_SLABS):
        for t in range(S5_CHUNK):
            piece = jnp.concatenate(
                [y_ref[j * PAIRS_PER_SLAB + m, :, t * S5_PAIR:(t + 1) * S5_PAIR] for m in range(PAIRS_PER_SLAB)], axis=1)
            scr[j, pl.ds(t, n_chunks, stride=S5_CHUNK), :] = piece
    return jnp.concatenate([scr[j] for j in range(LANE_SLABS)], axis=1)


def _head_rms(x, gain, ones_bd):
    sq = x * x
    hi = sq.astype(BF16)
    lo = (sq - hi.astype(F32)).astype(BF16)
    ss = _dot(hi, ones_bd) + _dot(lo, ones_bd)
    return x * lax.rsqrt(ss * (1.0 / HEAD_DIM) + RMS_EPS) * gain


def _rope_chunk(xc, cos_t, sin_t):
    lane = lax.broadcasted_iota(jnp.int32, xc.shape, 1)
    first_half = (lane & (HEAD_DIM - 1)) < (HEAD_DIM // 2)
    partner = jnp.where(first_half, pltpu.roll(xc, LANES - HEAD_DIM // 2, 1), pltpu.roll(xc, HEAD_DIM // 2, 1))
    return xc * cos_t + partner * sin_t


def _inproj_kernel(*refs, rope):
    if rope:
        (x_ref, sh_ref, sc_ref, w_ref, qn_ref, kn_ref, bdq_ref, bdk_ref, cos_ref, sin_ref,
         u_ref, qw_ref, kw_ref, vw_ref, qg_ref, kg_ref, vg_ref, gate_ref, u_scr) = refs
    else:
        (x_ref, sh_ref, sc_ref, w_ref, qn_ref, kn_ref, bdq_ref, bdk_ref,
         u_ref, qw_ref, kw_ref, vw_ref, qg_ref, kg_ref, vg_ref, gate_ref, u_scr) = refs
    h = (x_ref[...] * (1.0 + sc_ref[0]) + sh_ref[0]).astype(BF16)

    def proj(lo, width):
        return _dot(h, w_ref[:, lo:lo + width])

    def finish(x, out_ref, scale):
        n = x.shape[-1]
        for c in range(n // LANES):
            xc = x[:, c * LANES:(c + 1) * LANES]
            if rope:
                xc = _rope_chunk(xc, cos_ref[...], sin_ref[...])
            if scale != 1.0:
                xc = xc * scale
            out_ref[:, c * LANES:(c + 1) * LANES] = xc.astype(out_ref.dtype)

    _tokens_to_chunk_rows(proj(O_U, D_SSM), u_scr, u_ref)
    finish(proj(O_QW, D_ATT), qw_ref, Q_SCALE)
    finish(proj(O_KW, D_KV), kw_ref, 1.0)
    vw_ref[...] = proj(O_VW, D_KV)
    finish(_head_rms(proj(O_QG, D_ATT), qn_ref[...], bdq_ref[...]), qg_ref, Q_SCALE)
    finish(_head_rms(proj(O_KG, D_KV), kn_ref[...], bdk_ref[...]), kg_ref, 1.0)
    vg_ref[...] = proj(O_VG, D_KV)
    for c in range(3):
        gate_ref[:, c * D_MODEL:(c + 1) * D_MODEL] = proj(O_GATE + c * D_MODEL, D_MODEL).astype(gate_ref.dtype)


def _in_projection(x, mod_l, row_of_tile, w_in, qn, kn, bdq, bdk, rope_tabs, seq_len):
    n_tok = x.shape[0]
    tm = ROW_TILE
    rope = rope_tabs is not None
    row = lambda w: pl.BlockSpec((tm, w), lambda i: (i, 0))
    in_specs = [
        row(D_MODEL),
        _mod_spec(0, row_of_tile), _mod_spec(1, row_of_tile),
        _resident((D_MODEL, N_IN)),
        _resident((1, D_ATT)), _resident((1, D_KV)),
        _resident((D_ATT, D_ATT)), _resident((D_KV, D_KV)),
    ]
    args = [x, mod_l, mod_l, w_in, qn, kn, bdq, bdk]
    if rope:
        tiles_per_seq = seq_len // tm
        tab = pl.BlockSpec((tm, LANES), lambda i: (i % tiles_per_seq, 0))
        in_specs += [tab, tab]
        args += list(rope_tabs)
    out_shape = [
        jax.ShapeDtypeStruct((S5_NPAIR, n_tok // S5_CHUNK, S5_ROW), F32),
        jax.ShapeDtypeStruct((n_tok, D_ATT), BF16),
        jax.ShapeDtypeStruct((n_tok, D_KV), F32),
        jax.ShapeDtypeStruct((n_tok, D_KV), F32),
        jax.ShapeDtypeStruct((n_tok, D_ATT), BF16),
        jax.ShapeDtypeStruct((n_tok, D_KV), F32),
        jax.ShapeDtypeStruct((n_tok, D_KV), F32),
        jax.ShapeDtypeStruct((n_tok, 3 * D_MODEL), BF16),
    ]
    out_specs = ([pl.BlockSpec((S5_NPAIR, CHUNKS_PER_TILE, S5_ROW), lambda i: (0, i, 0))]
                 + [row(s.shape[1]) for s in out_shape[1:]])
    return pl.pallas_call(
        functools.partial(_inproj_kernel, rope=rope),
        grid=(n_tok // tm,),
        in_specs=in_specs,
        out_specs=out_specs,
        out_shape=out_shape,
        scratch_shapes=[pltpu.VMEM((LANE_SLABS, tm, LANES), F32)],
        compiler_params=_cparams(1),
        name="in_projection",
    )(*args)


def _zoh(lr, li, ls):
    dt = jnp.exp(ls)
    mag = jnp.exp(lr * dt)
    ar, ai = mag * jnp.cos(li * dt), mag * jnp.sin(li * dt)
    den = lr * lr + li * li
    fr = ((ar - 1.0) * lr + ai * li) / den
    fi = (ai * lr - (ar - 1.0) * li) / den
    return ar, ai, fr, fi


def _powers(ar, ai, n):
    out = [(jnp.ones_like(ar), jnp.zeros_like(ar))]
    for _ in range(n):
        pr, pi = out[-1]
        out.append((pr * ar - pi * ai, pr * ai + pi * ar))
    return out


def _s5_operator_kernel(lr_ref, li_ref, ls_ref, lrc_ref, lic_ref, lsc_ref,
                        btr_ref, bti_ref, cr_ref, ci_ref, ctr_ref, cti_ref,
                        top_ref, e0, e1, e2, e3, m0, m1, m2, m3, d0, d1, d2, d3, ext_scr, m_scr):
    tc, cg, p = S5_CHUNK, SSM_GROUP_CH, SSM_STATE
    ends = ((e0, e1), (e2, e3))
    decays = ((d0, d1), (d2, d3))
    ext_scr[...] = jnp.zeros(ext_scr.shape, F32)
    m_scr[...] = jnp.zeros(m_scr.shape, F32)
    zero_half = jnp.zeros((cg, p), F32)
    for d in range(2):
        for g2 in range(2):
            ar, ai, fr, fi = _zoh(lr_ref[d, 0, g2], li_ref[d, 0, g2], ls_ref[d, 0, g2])
            btr, bti = btr_ref[d, 0, g2], bti_ref[d, 0, g2]
            bbr, bbi = fr * btr - fi * bti, fr * bti + fi * btr
            cr, ci = cr_ref[d, 0, g2], ci_ref[d, 0, g2]
            pw = _powers(ar, ai, tc)
            group_rows = slice(g2 * cg, (g2 + 1) * cg)
            for j in range(tc):
                pr, pi = pw[j]
                t = tc - 1 - j if d == 0 else j
                rows = slice(t * S5_PAIR + g2 * cg, t * S5_PAIR + (g2 + 1) * cg)
                for part, val in enumerate((pr * bbr - pi * bbi, pr * bbi + pi * bbr)):
                    halves = [val, zero_half] if g2 == 0 else [zero_half, val]
                    ends[d][part][0, rows, :] = jnp.concatenate(halves, axis=1).astype(BF16)
                car, cai = cr * pr - ci * pi, cr * pi + ci * pr
                kt = (lax.dot_general(bbr, car, NT_DIMS, precision=lax.Precision.HIGHEST, preferred_element_type=F32)
                      - lax.dot_general(bbi, cai, NT_DIMS, precision=lax.Precision.HIGHEST, preferred_element_type=F32))
                slot = tc - 1 + j if d == 0 else tc - 1 - j
                lanes = slice(slot * S5_PAIR + g2 * cg, slot * S5_PAIR + (g2 + 1) * cg)
                if d == 1 and j == 0:
                    ext_scr[group_rows, lanes] = ext_scr[group_rows, lanes] + kt
                else:
                    ext_scr[group_rows, lanes] = kt
            for part in range(2):
                decays[d][part][:, g2 * p:(g2 + 1) * p] = pw[tc][part]
            arc, aic, _, _ = _zoh(lrc_ref[d, 0, g2], lic_ref[d, 0, g2], lsc_ref[d, 0, g2])
            pwc = _powers(arc, aic, tc)
            ctr, cti = ctr_ref[d, 0, g2], cti_ref[d, 0, g2]
            state_rows = slice(g2 * p, (g2 + 1) * p)
            for j in range(1, tc + 1):
                prc, pic = pwc[j]
                t = j - 1 if d == 0 else tc - j
                lanes = slice(t * S5_PAIR + g2 * cg, t * S5_PAIR + (g2 + 1) * cg)
                m_scr[2 * d, state_rows, lanes] = ctr * prc - cti * pic
                m_scr[2 * d + 1, state_rows, lanes] = -(ctr * pic + cti * prc)
    for t in range(tc):
        for g2 in range(2):
            rows = slice(t * S5_PAIR + g2 * cg, t * S5_PAIR + (g2 + 1) * cg)
            window = slice((tc - 1 - t) * S5_PAIR, (tc - 1 - t) * S5_PAIR + S5_ROW)
            top_ref[0, rows, :] = ext_scr[g2 * cg:(g2 + 1) * cg, window].astype(BF16)
    for k, m_ref in enumerate((m0, m1, m2, m3)):
        m_ref[0] = m_scr[k].astype(BF16)


def _s5_operators(lam_re, lam_im, log_step, b_re, b_im, c_re, c_im):
    g, p, cg = SSM_GROUPS, SSM_STATE, SSM_GROUP_CH
    ls = jnp.broadcast_to(log_step[:, :, None], (2, g, p))

    def rows(v):
        return v.reshape(2, S5_NPAIR, 2, 1, p)

    def cols(v):
        return v.reshape(2, S5_NPAIR, 2, p, 1)

    def mat(v, a, b):
        return v.reshape(2, S5_NPAIR, 2, a, b)

    def spec(*tail):
        return pl.BlockSpec((2, 1, 2) + tail, lambda n: (0, n, 0, 0, 0))

    out = pl.pallas_call(
        _s5_operator_kernel,
        grid=(S5_NPAIR,),
        in_specs=[spec(1, p)] * 3 + [spec(p, 1)] * 3 + [spec(cg, p)] * 4 + [spec(p, cg)] * 2,
        out_specs=([pl.BlockSpec((1, S5_ROW, S5_ROW), lambda n: (n, 0, 0))]
                   + [pl.BlockSpec((1, S5_ROW, S5_PSTATE), lambda n: (n, 0, 0))] * 4
                   + [pl.BlockSpec((1, S5_PSTATE, S5_ROW), lambda n: (n, 0, 0))] * 4
                   + [pl.BlockSpec((1, S5_PSTATE), lambda n: (0, n))] * 4),
        out_shape=([jax.ShapeDtypeStruct((S5_NPAIR, S5_ROW, S5_ROW), BF16)]
                   + [jax.ShapeDtypeStruct((S5_NPAIR, S5_ROW, S5_PSTATE), BF16)] * 4
                   + [jax.ShapeDtypeStruct((S5_NPAIR, S5_PSTATE, S5_ROW), BF16)] * 4
                   + [jax.ShapeDtypeStruct((1, N_STATE), F32)] * 4),
        scratch_shapes=[pltpu.VMEM((S5_PAIR, 2 * S5_ROW), F32), pltpu.VMEM((4, S5_PSTATE, S5_ROW), F32)],
        compiler_params=_cparams(1),
        name="s5_operators",
    )(rows(lam_re), rows(lam_im), rows(ls), cols(lam_re), cols(lam_im), cols(ls),
      mat(jnp.swapaxes(b_re, -1, -2), cg, p), mat(jnp.swapaxes(b_im, -1, -2), cg, p), mat(c_re, cg, p), mat(c_im, cg, p),
      mat(jnp.swapaxes(c_re, -1, -2), p, cg), mat(jnp.swapaxes(c_im, -1, -2), p, cg))
    return out[0], out[1:5], out[5:9], out[9:13]


def _s5_ends_kernel(u_ref, e0, e1, e2, e3, o0, o1, o2, o3):
    u = u_ref[0].astype(BF16)
    for e_ref, o_ref in ((e0, o0), (e1, o1), (e2, o2), (e3, o3)):
        o_ref[...] = _dot(u, e_ref[0])


def _s5_ends(u_rows, ends, rt):
    n_rows = u_rows.shape[1]
    e_spec = pl.BlockSpec((1, S5_ROW, S5_PSTATE), lambda n, r: (n, 0, 0))
    o_spec = pl.BlockSpec((rt, S5_PSTATE), lambda n, r: (r, n))
    return pl.pallas_call(
        _s5_ends_kernel,
        grid=(S5_NPAIR, n_rows // rt),
        in_specs=[pl.BlockSpec((1, rt, S5_ROW), lambda n, r: (n, r, 0))] + [e_spec] * 4,
        out_specs=[o_spec] * 4,
        out_shape=[jax.ShapeDtypeStruct((n_rows, N_STATE), F32)] * 4,
        compiler_params=_cparams(2),
        name="s5_chunk_ends",
    )(u_rows, *ends)


def _s5_scan_kernel(efr, efi, ebr, ebi, afr, afi, abr, abi, sfr, sfi, sbr, sbi,
                    pfr, pfi, nbr, nbi, ofr, ofi, obr, obi):
    nc = efr.shape[1]
    a_fr, a_fi, a_br, a_bi = afr[...], afi[...], abr[...], abi[...]

    def body(c, carry):
        fr, fi, br, bi = carry
        rc, rb = pl.ds(c, 1), pl.ds(nc - 1 - c, 1)
        pfr[0, rc, :] = fr
        pfi[0, rc, :] = fi
        nbr[0, rb, :] = br
        nbi[0, rb, :] = bi
        nfr = a_fr * fr - a_fi * fi + efr[0, rc, :]
        nfi = a_fr * fi + a_fi * fr + efi[0, rc, :]
        nbr_ = a_br * br - a_bi * bi + ebr[0, rb, :]
        nbi_ = a_br * bi + a_bi * br + ebi[0, rb, :]
        return nfr, nfi, nbr_, nbi_

    fr, fi, br, bi = lax.fori_loop(0, nc, body, (sfr[0], sfi[0], sbr[0], sbi[0]))
    ofr[0] = fr
    ofi[0] = fi
    obr[0] = br
    obi[0] = bi


def _s5_scan(se, decay, s0, nc, nb):
    cw = 1024
    seq = pl.BlockSpec((1, nc, cw), lambda b, i: (b, 0, i))
    vec = pl.BlockSpec((1, cw), lambda b, i: (0, i))
    st = pl.BlockSpec((1, 1, cw), lambda b, i: (b, 0, i))
    return pl.pallas_call(
        _s5_scan_kernel,
        grid=(nb, N_STATE // cw),
        in_specs=[seq] * 4 + [vec] * 4 + [st] * 4,
        out_specs=[seq] * 4 + [st] * 4,
        out_shape=[jax.ShapeDtypeStruct((nb, nc, N_STATE), F32)] * 4 + [jax.ShapeDtypeStruct((nb, 1, N_STATE), F32)] * 4,
        compiler_params=_cparams(2),
        name="s5_chunk_scan",
    )(*[s.reshape(nb, nc, N_STATE) for s in se], *decay, *[s.reshape(nb, 1, N_STATE) for s in s0])


def _s5_out_kernel(u_ref, top_ref, d_ref, pfr, pfi, nbr, nbi, m0, m1, m2, m3, o_ref):
    u = u_ref[0]
    y = _dot(u.astype(BF16), top_ref[0]) + d_ref[0] * u
    for s_ref, m_ref in ((pfr, m0), (pfi, m1), (nbr, m2), (nbi, m3)):
        y = y + _dot(s_ref[...].astype(BF16), m_ref[0])
    o_ref[0] = jax.nn.gelu(y)


def _s5_out(u_rows, top, d_rows, states, carries, rt):
    n_rows = u_rows.shape[1]
    s_spec = pl.BlockSpec((rt, S5_PSTATE), lambda n, r: (r, n))
    m_spec = pl.BlockSpec((1, S5_PSTATE, S5_ROW), lambda n, r: (n, 0, 0))
    return pl.pallas_call(
        _s5_out_kernel,
        grid=(S5_NPAIR, n_rows // rt),
        in_specs=[pl.BlockSpec((1, rt, S5_ROW), lambda n, r: (n, r, 0)),
                  pl.BlockSpec((1, S5_ROW, S5_ROW), lambda n, r: (n, 0, 0)),
                  pl.BlockSpec((1, 1, S5_ROW), lambda n, r: (n, 0, 0))] + [s_spec] * 4 + [m_spec] * 4,
        out_specs=pl.BlockSpec((1, rt, S5_ROW), lambda n, r: (n, r, 0)),
        out_shape=jax.ShapeDtypeStruct(u_rows.shape, F32),
        compiler_params=_cparams(2),
        name="s5_outputs",
    )(u_rows, top, d_rows, *states, *carries)


def _s5_branch(u_rows, ops, d_skip, s0, nb, seq_len):
    top, ends, carries, decay = ops
    nc = seq_len // S5_CHUNK
    n_rows = nc * nb
    rt = min(n_rows, 256)
    d_rows = jnp.tile(d_skip.reshape(S5_NPAIR, 1, S5_PAIR), (1, S5_CHUNK, 1)).reshape(S5_NPAIR, 1, S5_ROW)
    se = _s5_ends(u_rows, ends, rt)
    scanned = _s5_scan(se, decay, s0, nc, nb)
    states = [s.reshape(n_rows, N_STATE) for s in scanned[:4]]
    y_rows = _s5_out(u_rows, top, d_rows, states, carries, rt)
    return y_rows, [s.reshape(nb, N_STATE) for s in scanned[4:]]


Q_SLOT_ORDER = tuple(h for j in range(GRP) for h in (j, GRP + j))


def _stack_group_queries(q_ref, h, tq):
    lane = lax.broadcasted_iota(jnp.int32, (tq, LANES), 1)
    keep = (lane >= h * HEAD_DIM) & (lane < (h + 1) * HEAD_DIM)
    zero = jnp.zeros((tq, LANES), BF16)
    return jnp.concatenate([jnp.where(keep, q_ref[:, j * LANES:(j + 1) * LANES], zero) for j in range(GRP)], axis=0)


def _sink_lanes(sink_ref, h, tq):
    return jnp.concatenate([jnp.full((1, tq), sink_ref[h * GRP + j] * LOG2E, F32) for j in range(GRP)], axis=1)


def _values_with_ones(vt):
    return jnp.concatenate([vt, jnp.ones((V_ROWS - HEAD_DIM, vt.shape[1]), BF16)], axis=0)


def _store_heads(o_ref, h, o_t, tq):
    for j in range(GRP):
        head = h * GRP + j
        o_ref[0, head * HEAD_DIM:(head + 1) * HEAD_DIM, :] = o_t[:, j * tq:(j + 1) * tq].astype(o_ref.dtype)


def _attn_full_kernel(*refs, tq, use_sink):
    if use_sink:
        sink_ref, q_ref, k_ref, vt_ref, o_ref, m_ref, acc_ref, sa_ref, sb_ref = refs
    else:
        q_ref, k_ref, vt_ref, o_ref, m_ref, acc_ref, sa_ref, sb_ref = refs
    n_chunks = k_ref.shape[1]
    assert n_chunks % 2 == 1
    rows = GRP * tq
    qs = [_stack_group_queries(q_ref, h, tq) for h in range(N_KV)]
    for h in range(N_KV):
        if use_sink:
            m_ref[h] = _sink_lanes(sink_ref, h, tq)
            acc_ref[h] = jnp.concatenate([jnp.zeros((HEAD_DIM, rows), F32), jnp.ones((V_ROWS - HEAD_DIM, rows), F32)], axis=0)
        else:
            m_ref[h] = jnp.full((1, rows), NEG_INF, F32)
            acc_ref[h] = jnp.zeros((V_ROWS, rows), F32)

    def scores(c, s_ref):
        kc = k_ref[0, c]
        for h in range(N_KV):
            s_ref[h] = lax.dot_general(kc, qs[h], NT_DIMS, preferred_element_type=F32)

    def consume(c, s_ref):
        for h in range(N_KV):
            s = s_ref[h]
            m_old = m_ref[h]
            m_new = jnp.maximum(m_old, jnp.max(s, axis=0, keepdims=True))
            p = jnp.exp2(s - m_new).astype(BF16)
            alpha = jnp.exp2(m_old - m_new)
            lhs = _values_with_ones(vt_ref[0, c, h * HEAD_DIM:(h + 1) * HEAD_DIM, :])
            acc_ref[h] = alpha * acc_ref[h] + _dot(lhs, p)
            m_ref[h] = m_new

    scores(0, sa_ref)

    def body(k, carry):
        c = 2 * k
        scores(c + 1, sb_ref)
        consume(c, sa_ref)
        scores(c + 2, sa_ref)
        consume(c + 1, sb_ref)
        return carry

    lax.fori_loop(0, n_chunks // 2, body, 0, unroll=True)
    consume(n_chunks - 1, sa_ref)
    for h in range(N_KV):
        acc = acc_ref[h]
        _store_heads(o_ref, h, acc[:HEAD_DIM] / acc[HEAD_DIM:HEAD_DIM + 1], tq)


def _attn_band_kernel(sink_ref, q_ref, k_ref, vt_ref, o_ref, *, tq, n_lat, n_ctx):
    i = pl.program_id(1)
    n_band = tq // WINDOW + 2
    lat_chunks, ctx_chunks = n_lat // WINDOW, n_ctx // WINDOW
    c0 = jnp.clip(i * (tq // WINDOW) - 1, 0, lat_chunks - n_band)
    rows = GRP * tq
    row = lax.broadcasted_iota(jnp.int32, (n_band * WINDOW, rows), 0)
    col = lax.broadcasted_iota(jnp.int32, (n_band * WINDOW, rows), 1)
    in_band = jnp.abs((c0 * WINDOW + row) - (i * tq + (col & (tq - 1)))) <= WINDOW
    for h in range(N_KV):
        qs = _stack_group_queries(q_ref, h, tq)
        k_band = k_ref[0, pl.ds(c0, n_band)].reshape(n_band * WINDOW, LANES)
        s_band = jnp.where(in_band, lax.dot_general(k_band, qs, NT_DIMS, preferred_element_type=F32), NEG_INF)
        k_ctx = k_ref[0, lat_chunks:lat_chunks + ctx_chunks].reshape(n_ctx, LANES)
        s_ctx = lax.dot_general(k_ctx, qs, NT_DIMS, preferred_element_type=F32)
        sink = _sink_lanes(sink_ref, h, tq)
        m = jnp.maximum(jnp.maximum(jnp.max(s_band, axis=0, keepdims=True), jnp.max(s_ctx, axis=0, keepdims=True)), sink)
        p_band = jnp.exp2(s_band - m).astype(BF16)
        p_ctx = jnp.exp2(s_ctx - m).astype(BF16)
        hd = slice(h * HEAD_DIM, (h + 1) * HEAD_DIM)
        acc = jnp.zeros((V_ROWS, rows), F32)
        for t in range(n_band):
            acc = acc + _dot(_values_with_ones(vt_ref[0, c0 + t, hd, :]), p_band[t * WINDOW:(t + 1) * WINDOW])
        for t in range(ctx_chunks):
            acc = acc + _dot(_values_with_ones(vt_ref[0, lat_chunks + t, hd, :]), p_ctx[t * WINDOW:(t + 1) * WINDOW])
        den = acc[HEAD_DIM:HEAD_DIM + 1] + jnp.exp2(sink - m)
        _store_heads(o_ref, h, acc[:HEAD_DIM] / den, tq)


def _attention(q, k, v, sink, nb, seq_len, tq, tk, band_ctx=None):
    n_k = k.shape[1]
    kc = k.reshape(nb, n_k // tk, tk, D_KV)
    vt = v.reshape(nb, n_k // tk, tk, D_KV).transpose(0, 1, 3, 2)
    tiles = seq_len // tq
    q_spec = pl.BlockSpec((tq, D_ATT), lambda b, i: (b * tiles + i, 0))
    k_spec = pl.BlockSpec((1, n_k // tk, tk, D_KV), lambda b, i: (b, 0, 0, 0))
    vt_spec = pl.BlockSpec((1, n_k // tk, D_KV, tk), lambda b, i: (b, 0, 0, 0))
    in_specs, args = [q_spec, k_spec, vt_spec], [q, kc, vt]
    if sink is not None:
        in_specs, args = [pl.BlockSpec(memory_space=pltpu.SMEM)] + in_specs, [sink] + args
    rows = GRP * tq
    if band_ctx is None:
        body = functools.partial(_attn_full_kernel, tq=tq, use_sink=sink is not None)
        scratch = [pltpu.VMEM((N_KV, 1, rows), F32), pltpu.VMEM((N_KV, V_ROWS, rows), F32),
                   pltpu.VMEM((N_KV, tk, rows), F32), pltpu.VMEM((N_KV, tk, rows), F32)]
        name = "attention_full"
    else:
        body = functools.partial(_attn_band_kernel, tq=tq, n_lat=seq_len, n_ctx=band_ctx)
        scratch = []
        name = "attention_band"
    o_t = pl.pallas_call(
        body,
        grid=(nb, tiles),
        in_specs=in_specs,
        out_specs=pl.BlockSpec((1, D_ATT, tq), lambda b, i: (b, 0, i)),
        out_shape=jax.ShapeDtypeStruct((nb, D_ATT, seq_len), BF16),
        scratch_shapes=scratch,
        compiler_params=_cparams(2),
        name=name,
    )(*args)
    return o_t.transpose(0, 2, 1).reshape(nb * seq_len, D_ATT)


def _layer_norm(z, g, b):
    mu = jnp.mean(z, axis=-1, keepdims=True)
    zc = z - mu
    var = jnp.mean(zc * zc, axis=-1, keepdims=True)
    return zc * lax.rsqrt(var + LN_EPS) * g + b


def _merge_kernel(x_ref, g1_ref, ya_ref, yw_ref, yg_ref, gate_ref,
                  wglu_ref, wa_ref, ww_ref, wg_ref, wout_ref, lng_ref, lnb_ref, o_ref, ya_scr):
    ya = _chunk_rows_to_tokens(ya_ref, ya_scr)
    ya = ya * jax.nn.sigmoid(_dot(ya.astype(BF16), wglu_ref[...]))

    def gate(c):
        return jax.nn.sigmoid(gate_ref[:, c * D_MODEL:(c + 1) * D_MODEL].astype(F32))

    m = (gate(0) * _dot(ya.astype(BF16), wa_ref[...])
         + gate(1) * _dot(yw_ref[...], ww_ref[...])
         + gate(2) * _dot(yg_ref[...], wg_ref[...]))
    f = _dot(m.astype(BF16), wout_ref[...])
    o_ref[...] = _layer_norm(DEEPNORM_ALPHA * x_ref[...] + g1_ref[0] * f, lng_ref[...], lnb_ref[...])


def _merge(x, mod_l, row_of_tile, ya_rows, yw, yg, gates, w_glu, w_a, w_w, w_g, w_out, ln_g, ln_b):
    n_tok = x.shape[0]
    tm = ROW_TILE
    row = lambda w: pl.BlockSpec((tm, w), lambda i: (i, 0))
    return pl.pallas_call(
        _merge_kernel,
        grid=(n_tok // tm,),
        in_specs=[row(D_MODEL), _mod_spec(2, row_of_tile),
                  pl.BlockSpec((S5_NPAIR, CHUNKS_PER_TILE, S5_ROW), lambda i: (0, i, 0)),
                  row(D_ATT), row(D_ATT), row(3 * D_MODEL),
                  _resident((D_SSM, D_SSM)), _resident((D_SSM, D_MODEL)), _resident((D_ATT, D_MODEL)),
                  _resident((D_ATT, D_MODEL)), _resident((D_MODEL, D_MODEL)),
                  _resident((1, D_MODEL)), _resident((1, D_MODEL))],
        out_specs=row(D_MODEL),
        out_shape=jax.ShapeDtypeStruct((n_tok, D_MODEL), F32),
        scratch_shapes=[pltpu.VMEM((LANE_SLABS, tm, LANES), F32)],
        compiler_params=_cparams(1),
        name="merge_residual",
    )(x, mod_l, ya_rows, yw, yg, gates, w_glu, w_a, w_w, w_g, w_out, ln_g, ln_b)


def _mlp_kernel(x_ref, sh_ref, sc_ref, g2_ref, wup_ref, wdn_ref, lng_ref, lnb_ref, o_ref):
    x = x_ref[...]
    h = (x * (1.0 + sc_ref[0]) + sh_ref[0]).astype(BF16)
    ff_tile = 1024
    acc = jnp.zeros(x.shape, F32)
    for c in range(D_FF // ff_tile):
        up = jnp.maximum(_dot(h, wup_ref[:, c * ff_tile:(c + 1) * ff_tile]), 0.0)
        acc = acc + _dot((up * up).astype(BF16), wdn_ref[c * ff_tile:(c + 1) * ff_tile, :])
    o_ref[...] = _layer_norm(DEEPNORM_ALPHA * x + g2_ref[0] * acc, lng_ref[...], lnb_ref[...])


def _mlp(x, mod_l, row_of_tile, w_up, w_down, ln_g, ln_b):
    n_tok = x.shape[0]
    tm = ROW_TILE
    row = pl.BlockSpec((tm, D_MODEL), lambda i: (i, 0))
    return pl.pallas_call(
        _mlp_kernel,
        grid=(n_tok // tm,),
        in_specs=[row, _mod_spec(3, row_of_tile), _mod_spec(4, row_of_tile), _mod_spec(5, row_of_tile),
                  _resident((D_MODEL, D_FF)), _resident((D_FF, D_MODEL)),
                  _resident((1, D_MODEL)), _resident((1, D_MODEL))],
        out_specs=row,
        out_shape=jax.ShapeDtypeStruct((n_tok, D_MODEL), F32),
        compiler_params=_cparams(1),
        name="mlp_residual",
    )(x, mod_l, mod_l, mod_l, w_up, w_down, ln_g, ln_b)


def _rope_tables(n_tok):
    rows = n_tok // GRID_W
    row = jnp.repeat(jnp.arange(rows, dtype=F32), GRID_W)
    col = jnp.tile(jnp.arange(GRID_W, dtype=F32), rows)
    n_freq = HEAD_DIM // 4
    inv = ROPE_BASE ** (-jnp.arange(n_freq, dtype=F32) / n_freq)
    ang = jnp.concatenate([row[:, None] * inv, col[:, None] * inv], axis=-1)
    cos, sin = jnp.cos(ang), jnp.sin(ang)
    cos_t = jnp.tile(jnp.concatenate([cos, cos], axis=-1), (1, LANES // HEAD_DIM))
    sin_t = jnp.tile(jnp.concatenate([-sin, sin], axis=-1), (1, LANES // HEAD_DIM))
    return cos_t, sin_t


def _block_diag_ones(n):
    idx = np.arange(n) // HEAD_DIM
    return jnp.asarray(idx[:, None] == idx[None, :], dtype=BF16)


def _permute_q_columns(w):
    def perm(block):
        return block.reshape(D_MODEL, N_HEADS, HEAD_DIM)[:, np.array(Q_SLOT_ORDER), :].reshape(D_MODEL, D_ATT)
    return jnp.concatenate([w[:, :O_QW], perm(w[:, O_QW:O_KW]), w[:, O_KW:O_QG], perm(w[:, O_QG:O_KG]), w[:, O_KG:]], axis=1)


def _layer(x, lw, mod_l, row_of_tile, nb, seq_len, ctx, rope_tabs):
    u_rows, qw, kw, vw, qg, kg, vg, gates = _in_projection(
        x, mod_l, row_of_tile, lw['w_in'], lw['qn'], lw['kn'], lw['bdq'], lw['bdk'], rope_tabs, seq_len)
    if ctx is None:
        s0 = [jnp.zeros((nb, N_STATE), F32)] * 4
    else:
        s0 = [ctx[0][:, d, part].reshape(nb, N_STATE) for d in (0, 1) for part in (0, 1)]
    ya_rows, s_fin = _s5_branch(u_rows, lw['s5_ops'], lw['d_skip'], s0, nb, seq_len)

    def seq(t, n):
        return t.reshape(nb, n, D_KV)

    if ctx is None:
        yw = _attention(qw, seq(kw, seq_len).astype(BF16), seq(vw, seq_len).astype(BF16), lw['sink'],
                        nb, seq_len, seq_len, seq_len)
        yg = _attention(qg, seq(kg, seq_len).astype(BF16), seq(vg, seq_len).astype(BF16), None,
                        nb, seq_len, seq_len, seq_len)
        new_ctx = (s_fin, kw, vw, kg, vg)
    else:
        _, k_wc, v_wc, k_gc, v_gc = ctx
        n_ctx = k_wc.shape[1]

        def with_ctx(new, cached):
            return jnp.concatenate([seq(new, seq_len), seq(cached, n_ctx)], axis=1).astype(BF16)

        yw = _attention(qw, with_ctx(kw, k_wc), with_ctx(vw, v_wc), lw['sink'], nb, seq_len, 256, WINDOW, band_ctx=n_ctx)
        yg = _attention(qg, with_ctx(kg, k_gc), with_ctx(vg, v_gc), None, nb, seq_len, 256, 256)
        new_ctx = None
    x1 = _merge(x, mod_l, row_of_tile, ya_rows, yw, yg, gates,
                lw['w_glu'], lw['w_br_ssm'], lw['w_br_win'], lw['w_br_glb'], lw['w_out'], lw['ln1_g'], lw['ln1_b'])
    x2 = _mlp(x1, mod_l, row_of_tile, lw['w_up'], lw['w_down'], lw['ln2_g'], lw['ln2_b'])
    return x2, new_ctx


def kernel(x_prompt, x_sample, state_ssm, cache_k_win, cache_v_win, cache_k_glb, cache_v_glb, c, c_ctx, w_mod, b_mod, w_in, ssm_lam_re, ssm_lam_im, ssm_log_step, ssm_b_re, ssm_b_im, ssm_c_re, ssm_c_im, ssm_d, w_glu, sink_win, q_norm_glb, k_norm_glb, w_br_ssm, w_br_win, w_br_glb, w_out, ln1_g, ln1_b, w_up, w_down, ln2_g, ln2_b):
    n_ctx_b, ctx_len, _ = x_prompt.shape
    n_lat_b, lat_len, _ = x_sample.shape
    assert ctx_len % ROW_TILE == 0 or ROW_TILE % ctx_len == 0
    assert lat_len % ROW_TILE == 0 and (n_ctx_b * ctx_len) % ROW_TILE == 0

    cond8 = jnp.zeros((8, D_MODEL), F32).at[0].set(c_ctx).at[1:1 + n_lat_b].set(c)
    mod = _modulation(cond8, w_mod, b_mod).reshape(DEPTH, 8, 1, 6 * D_MODEL)
    rope_tabs = _rope_tables(lat_len)
    bdq, bdk = _block_diag_ones(D_ATT), _block_diag_ones(D_KV)
    lat_tiles = lat_len // ROW_TILE
    ctx_row = lambda i: 0
    lat_row = lambda i: 1 + i // lat_tiles

    xp = x_prompt.reshape(n_ctx_b * ctx_len, D_MODEL)
    xs = x_sample.reshape(n_lat_b * lat_len, D_MODEL)
    new_ssm, new_kw, new_vw, new_kg, new_vg = [], [], [], [], []
    for l in range(DEPTH):
        lw = dict(
            w_in=_permute_q_columns(w_in[l]).astype(BF16),
            qn=jnp.tile(q_norm_glb[l], N_HEADS).reshape(1, D_ATT), kn=jnp.tile(k_norm_glb[l], N_KV).reshape(1, D_KV),
            bdq=bdq, bdk=bdk,
            s5_ops=_s5_operators(ssm_lam_re[l], ssm_lam_im[l], ssm_log_step[l],
                                 ssm_b_re[l], ssm_b_im[l], ssm_c_re[l], ssm_c_im[l]),
            d_skip=ssm_d[l],
            sink=sink_win[l],
            w_glu=w_glu[l].astype(BF16), w_br_ssm=w_br_ssm[l].astype(BF16), w_br_win=w_br_win[l].astype(BF16),
            w_br_glb=w_br_glb[l].astype(BF16), w_out=w_out[l].astype(BF16),
            ln1_g=ln1_g[l].reshape(1, D_MODEL), ln1_b=ln1_b[l].reshape(1, D_MODEL),
            w_up=w_up[l].astype(BF16), w_down=w_down[l].astype(BF16),
            ln2_g=ln2_g[l].reshape(1, D_MODEL), ln2_b=ln2_b[l].reshape(1, D_MODEL),
        )
        xp, (s_fin, kw, vw, kg, vg) = _layer(xp, lw, mod[l], ctx_row, n_ctx_b, ctx_len, None, None)
        new_ssm.append(jnp.stack(s_fin, axis=1).reshape(n_ctx_b, 2, 2, SSM_GROUPS, SSM_STATE))
        for acc, t in ((new_kw, kw), (new_vw, vw), (new_kg, kg), (new_vg, vg)):
            acc.append(t.reshape(n_ctx_b, ctx_len, N_KV, HEAD_DIM))
        ctx = (state_ssm[:, l], cache_k_win[:, l], cache_v_win[:, l], cache_k_glb[:, l], cache_v_glb[:, l])
        xs, _ = _layer(xs, lw, mod[l], lat_row, n_lat_b, lat_len, ctx, rope_tabs)
    return (xp.reshape(x_prompt.shape), xs.reshape(x_sample.shape),
            jnp.stack(new_ssm, axis=1), jnp.stack(new_kw, axis=1), jnp.stack(new_vw, axis=1),
            jnp.stack(new_kg, axis=1), jnp.stack(new_vg, axis=1))
```

```python
import functools

import jax
import jax.numpy as jnp
import numpy as np
from jax import lax
from jax.experimental import pallas as pl
from jax.experimental.pallas import tpu as pltpu

F32 = jnp.float32
BF16 = jnp.bfloat16

D_MODEL = 1024
DEPTH = 2
GRID_W = 64
HEAD_DIM = 64
D_SSM = 512
SSM_GROUP_CH = 16
SSM_GROUPS = 32
SSM_STATE = 64
N_HEADS = 8
N_KV = 2
GRP = N_HEADS // N_KV
D_ATT = N_HEADS * HEAD_DIM
D_KV = N_KV * HEAD_DIM
WINDOW = 128
ROPE_BASE = 10000.0
D_FF = 4 * D_MODEL
LN_EPS = 1e-5
RMS_EPS = 1e-6
ATTN_SCALE = HEAD_DIM ** -0.5
DEEPNORM_ALPHA = (2.0 * DEPTH) ** 0.25
NEG_INF = -1e30
LOG2E = 1.4426950408889634
Q_SCALE = ATTN_SCALE * LOG2E
V_ROWS = HEAD_DIM + 16
N_IN = D_SSM + 2 * (D_ATT + 2 * D_KV) + 3 * D_MODEL
O_U = 0
O_QW = O_U + D_SSM
O_KW = O_QW + D_ATT
O_VW = O_KW + D_KV
O_QG = O_VW + D_KV
O_KG = O_QG + D_ATT
O_VG = O_KG + D_KV
O_GATE = O_VG + D_KV

S5_CHUNK = 16
S5_PAIR = 2 * SSM_GROUP_CH
S5_NPAIR = SSM_GROUPS // 2
S5_ROW = S5_CHUNK * S5_PAIR
S5_PSTATE = 2 * SSM_STATE
N_STATE = SSM_GROUPS * SSM_STATE

LANES = 128
ROW_TILE = 512
CHUNKS_PER_TILE = ROW_TILE // S5_CHUNK
LANE_SLABS = D_SSM // LANES
PAIRS_PER_SLAB = LANES // S5_PAIR
VMEM_LIMIT = 56 * 1024 * 1024
NT_DIMS = (((1,), (1,)), ((), ()))


def _cparams(n_axes):
    return pltpu.CompilerParams(dimension_semantics=("arbitrary",) * n_axes, vmem_limit_bytes=VMEM_LIMIT)


def _resident(shape):
    nd = len(shape)
    return pl.BlockSpec(shape, lambda *_: (0,) * nd, pipeline_mode=pl.Buffered(1))


def _dot(a, b):
    return jnp.dot(a, b, preferred_element_type=F32)


def _mod_kernel(c_ref, w_ref, b_ref, o_ref):
    c = c_ref[...]
    a = (c * jax.nn.sigmoid(c)).astype(BF16)
    o_ref[0] = _dot(a, w_ref[0].astype(BF16)) + b_ref[0]


def _modulation(cond8, w_mod, b_mod):
    tn = 512
    n_out = w_mod.shape[-1]
    return pl.pallas_call(
        _mod_kernel,
        grid=(DEPTH, n_out // tn),
        in_specs=[
            pl.BlockSpec((8, D_MODEL), lambda l, n: (0, 0)),
            pl.BlockSpec((1, D_MODEL, tn), lambda l, n: (l, 0, n)),
            pl.BlockSpec((1, 1, tn), lambda l, n: (l, 0, n)),
        ],
        out_specs=pl.BlockSpec((1, 8, tn), lambda l, n: (l, 0, n)),
        out_shape=jax.ShapeDtypeStruct((DEPTH, 8, n_out), F32),
        compiler_params=_cparams(2),
        name="modulation",
    )(cond8, w_mod, b_mod.reshape(DEPTH, 1, n_out))


def _mod_spec(chunk, row_of_tile):
    return pl.BlockSpec((1, 1, D_MODEL), lambda i: (row_of_tile(i), 0, chunk))


def _tokens_to_chunk_rows(u, scr, out_ref):
    n_chunks = u.shape[0] // S5_CHUNK
    for j in range(LANE_SLABS):
        scr[j] = u[:, j * LANES:(j + 1) * LANES]
    pieces = [[None] * S5_CHUNK for _ in range(S5_NPAIR)]
    for j in range(LANE_SLABS):
        for t in range(S5_CHUNK):
            step_rows = scr[j, pl.ds(t, n_chunks, stride=S5_CHUNK), :]
            for m in range(PAIRS_PER_SLAB):
                pieces[j * PAIRS_PER_SLAB + m][t] = step_rows[:, m * S5_PAIR:(m + 1) * S5_PAIR]
    for n in range(S5_NPAIR):
        out_ref[n] = jnp.concatenate(pieces[n], axis=1)


def _chunk_rows_to_tokens(y_ref, scr):
    n_chunks = y_ref.shape[1]
    for j in range(LANE_SLABS):
        for t in range(S5_CHUNK):
            piece = jnp.concatenate(
                [y_ref[j * PAIRS_PER_SLAB + m, :, t * S5_PAIR:(t + 1) * S5_PAIR] for m in range(PAIRS_PER_SLAB)], axis=1)
            scr[j, pl.ds(t, n_chunks, stride=S5_CHUNK), :] = piece
    return jnp.concatenate([scr[j] for j in range(LANE_SLABS)], axis=1)


def _head_rms(x, gain, ones_bd):
    ss = _dot((x * x).astype(BF16), ones_bd)
    return x * lax.rsqrt(ss * (1.0 / HEAD_DIM) + RMS_EPS) * gain


def _rope_chunk(xc, cos_t, sin_t):
    lane = lax.broadcasted_iota(jnp.int32, xc.shape, 1)
    first_half = (lane & (HEAD_DIM - 1)) < (HEAD_DIM // 2)
    partner = jnp.where(first_half, pltpu.roll(xc, LANES - HEAD_DIM // 2, 1), pltpu.roll(xc, HEAD_DIM // 2, 1))
    return xc * cos_t + partner * sin_t


def _inproj_kernel(*refs, rope):
    n_in = 10 if rope else 8
    x_ref, sh_ref, sc_ref, w_ref, qn_ref, kn_ref, bdq_ref, bdk_ref = refs[:8]
    u_ref, qw_ref, qg_ref, gate_ref = refs[n_in:n_in + 4]
    kv_refs, u_scr = refs[n_in + 4:-1], refs[-1]
    h = (x_ref[...] * (1.0 + sc_ref[0]) + sh_ref[0]).astype(BF16)

    def proj(lo, width):
        return _dot(h, w_ref[:, lo:lo + width])

    def finish(x, out_refs, scale=1.0, rotate=rope):
        n = x.shape[-1]
        for c in range(n // LANES):
            xc = x[:, c * LANES:(c + 1) * LANES]
            if rotate:
                xc = _rope_chunk(xc, refs[8][...], refs[9][...])
            if scale != 1.0:
                xc = xc * scale
            for out_ref in out_refs:
                out_ref[:, c * LANES:(c + 1) * LANES] = xc.astype(out_ref.dtype)

    _tokens_to_chunk_rows(proj(O_U, D_SSM), u_scr, u_ref)
    finish(proj(O_QW, D_ATT), [qw_ref], Q_SCALE)
    finish(proj(O_KW, D_KV), kv_refs[0::4])
    finish(proj(O_VW, D_KV), kv_refs[1::4], rotate=False)
    finish(_head_rms(proj(O_QG, D_ATT), qn_ref[...], bdq_ref[...]), [qg_ref], Q_SCALE)
    finish(_head_rms(proj(O_KG, D_KV), kn_ref[...], bdk_ref[...]), kv_refs[2::4])
    finish(proj(O_VG, D_KV), kv_refs[3::4], rotate=False)
    for c in range(3):
        gate = jax.nn.sigmoid(proj(O_GATE + c * D_MODEL, D_MODEL))
        gate_ref[:, c * D_MODEL:(c + 1) * D_MODEL] = gate.astype(gate_ref.dtype)


def _in_projection(x, mod_l, row_of_tile, w_in, qn, kn, bdq, bdk, rope_tabs, seq_len, keep_f32_kv):
    n_tok = x.shape[0]
    tm = ROW_TILE
    rope = rope_tabs is not None
    row = lambda w: pl.BlockSpec((tm, w), lambda i: (i, 0))
    in_specs = [
        row(D_MODEL),
        _mod_spec(0, row_of_tile), _mod_spec(1, row_of_tile),
        _resident((D_MODEL, N_IN)),
        _resident((1, D_ATT)), _resident((1, D_KV)),
        _resident((D_ATT, D_ATT)), _resident((D_KV, D_KV)),
    ]
    args = [x, mod_l, mod_l, w_in, qn, kn, bdq, bdk]
    if rope:
        tiles_per_seq = seq_len // tm
        tab = pl.BlockSpec((tm, LANES), lambda i: (i % tiles_per_seq, 0))
        in_specs += [tab, tab]
        args += list(rope_tabs)
    kv_dtypes = [BF16] * 4 + ([F32] * 4 if keep_f32_kv else [])
    out_shape = ([jax.ShapeDtypeStruct((S5_NPAIR, n_tok // S5_CHUNK, S5_ROW), F32),
                  jax.ShapeDtypeStruct((n_tok, D_ATT), BF16),
                  jax.ShapeDtypeStruct((n_tok, D_ATT), BF16),
                  jax.ShapeDtypeStruct((n_tok, 3 * D_MODEL), BF16)]
                 + [jax.ShapeDtypeStruct((n_tok, D_KV), dt) for dt in kv_dtypes])
    out_specs = ([pl.BlockSpec((S5_NPAIR, CHUNKS_PER_TILE, S5_ROW), lambda i: (0, i, 0))]
                 + [row(s.shape[1]) for s in out_shape[1:]])
    return pl.pallas_call(
        functools.partial(_inproj_kernel, rope=rope),
        grid=(n_tok // tm,),
        in_specs=in_specs,
        out_specs=out_specs,
        out_shape=out_shape,
        scratch_shapes=[pltpu.VMEM((LANE_SLABS, tm, LANES), F32)],
        compiler_params=_cparams(1),
        name="in_projection",
    )(*args)


def _zoh(lr, li, ls):
    dt = jnp.exp(ls)
    mag = jnp.exp(lr * dt)
    ar, ai = mag * jnp.cos(li * dt), mag * jnp.sin(li * dt)
    den = lr * lr + li * li
    fr = ((ar - 1.0) * lr + ai * li) / den
    fi = (ai * lr - (ar - 1.0) * li) / den
    return ar, ai, fr, fi


def _powers(ar, ai, n):
    out = [(jnp.ones_like(ar), jnp.zeros_like(ar))]
    for _ in range(n):
        pr, pi = out[-1]
        out.append((pr * ar - pi * ai, pr * ai + pi * ar))
    return out


def _s5_operator_kernel(lr_ref, li_ref, ls_ref, lrc_ref, lic_ref, lsc_ref,
                        btr_ref, bti_ref, cr_ref, ci_ref, ctr_ref, cti_ref,
                        top_ref, e0, e1, e2, e3, m0, m1, m2, m3, d0, d1, d2, d3, ext_scr, m_scr):
    tc, cg, p = S5_CHUNK, SSM_GROUP_CH, SSM_STATE
    ends = ((e0, e1), (e2, e3))
    decays = ((d0, d1), (d2, d3))
    ext_scr[...] = jnp.zeros(ext_scr.shape, F32)
    m_scr[...] = jnp.zeros(m_scr.shape, F32)
    zero_half = jnp.zeros((cg, p), F32)
    for d in range(2):
        for g2 in range(2):
            ar, ai, fr, fi = _zoh(lr_ref[d, 0, g2], li_ref[d, 0, g2], ls_ref[d, 0, g2])
            btr, bti = btr_ref[d, 0, g2], bti_ref[d, 0, g2]
            bbr, bbi = fr * btr - fi * bti, fr * bti + fi * btr
            cr, ci = cr_ref[d, 0, g2], ci_ref[d, 0, g2]
            pw = _powers(ar, ai, tc)
            group_rows = slice(g2 * cg, (g2 + 1) * cg)
            for j in range(tc):
                pr, pi = pw[j]
                t = tc - 1 - j if d == 0 else j
                rows = slice(t * S5_PAIR + g2 * cg, t * S5_PAIR + (g2 + 1) * cg)
                for part, val in enumerate((pr * bbr - pi * bbi, pr * bbi + pi * bbr)):
                    halves = [val, zero_half] if g2 == 0 else [zero_half, val]
                    ends[d][part][0, rows, :] = jnp.concatenate(halves, axis=1).astype(BF16)
                car, cai = cr * pr - ci * pi, cr * pi + ci * pr
                kt = (lax.dot_general(bbr, car, NT_DIMS, precision=lax.Precision.HIGHEST, preferred_element_type=F32)
                      - lax.dot_general(bbi, cai, NT_DIMS, precision=lax.Precision.HIGHEST, preferred_element_type=F32))
                slot = tc - 1 + j if d == 0 else tc - 1 - j
                lanes = slice(slot * S5_PAIR + g2 * cg, slot * S5_PAIR + (g2 + 1) * cg)
                if d == 1 and j == 0:
                    ext_scr[group_rows, lanes] = ext_scr[group_rows, lanes] + kt
                else:
                    ext_scr[group_rows, lanes] = kt
            for part in range(2):
                decays[d][part][:, g2 * p:(g2 + 1) * p] = pw[tc][part]
            arc, aic, _, _ = _zoh(lrc_ref[d, 0, g2], lic_ref[d, 0, g2], lsc_ref[d, 0, g2])
            pwc = _powers(arc, aic, tc)
            ctr, cti = ctr_ref[d, 0, g2], cti_ref[d, 0, g2]
            state_rows = slice(g2 * p, (g2 + 1) * p)
            for j in range(1, tc + 1):
                prc, pic = pwc[j]
                t = j - 1 if d == 0 else tc - j
                lanes = slice(t * S5_PAIR + g2 * cg, t * S5_PAIR + (g2 + 1) * cg)
                m_scr[2 * d, state_rows, lanes] = ctr * prc - cti * pic
                m_scr[2 * d + 1, state_rows, lanes] = -(ctr * pic + cti * prc)
    for t in range(tc):
        for g2 in range(2):
            rows = slice(t * S5_PAIR + g2 * cg, t * S5_PAIR + (g2 + 1) * cg)
            window = slice((tc - 1 - t) * S5_PAIR, (tc - 1 - t) * S5_PAIR + S5_ROW)
            top_ref[0, rows, :] = ext_scr[g2 * cg:(g2 + 1) * cg, window].astype(BF16)
    for k, m_ref in enumerate((m0, m1, m2, m3)):
        m_ref[0] = m_scr[k].astype(BF16)


def _s5_operators(lam_re, lam_im, log_step, b_re, b_im, c_re, c_im):
    g, p, cg = SSM_GROUPS, SSM_STATE, SSM_GROUP_CH
    ls = jnp.broadcast_to(log_step[:, :, None], (2, g, p))

    def rows(v):
        return v.reshape(2, S5_NPAIR, 2, 1, p)

    def cols(v):
        return v.reshape(2, S5_NPAIR, 2, p, 1)

    def mat(v, a, b):
        return v.reshape(2, S5_NPAIR, 2, a, b)

    def spec(*tail):
        return pl.BlockSpec((2, 1, 2) + tail, lambda n: (0, n, 0, 0, 0))

    out = pl.pallas_call(
        _s5_operator_kernel,
        grid=(S5_NPAIR,),
        in_specs=[spec(1, p)] * 3 + [spec(p, 1)] * 3 + [spec(cg, p)] * 4 + [spec(p, cg)] * 2,
        out_specs=([pl.BlockSpec((1, S5_ROW, S5_ROW), lambda n: (n, 0, 0))]
                   + [pl.BlockSpec((1, S5_ROW, S5_PSTATE), lambda n: (n, 0, 0))] * 4
                   + [pl.BlockSpec((1, S5_PSTATE, S5_ROW), lambda n: (n, 0, 0))] * 4
                   + [pl.BlockSpec((1, S5_PSTATE), lambda n: (0, n))] * 4),
        out_shape=([jax.ShapeDtypeStruct((S5_NPAIR, S5_ROW, S5_ROW), BF16)]
                   + [jax.ShapeDtypeStruct((S5_NPAIR, S5_ROW, S5_PSTATE), BF16)] * 4
                   + [jax.ShapeDtypeStruct((S5_NPAIR, S5_PSTATE, S5_ROW), BF16)] * 4
                   + [jax.ShapeDtypeStruct((1, N_STATE), F32)] * 4),
        scratch_shapes=[pltpu.VMEM((S5_PAIR, 2 * S5_ROW), F32), pltpu.VMEM((4, S5_PSTATE, S5_ROW), F32)],
        compiler_params=_cparams(1),
        name="s5_operators",
    )(rows(lam_re), rows(lam_im), rows(ls), cols(lam_re), cols(lam_im), cols(ls),
      mat(jnp.swapaxes(b_re, -1, -2), cg, p), mat(jnp.swapaxes(b_im, -1, -2), cg, p), mat(c_re, cg, p), mat(c_im, cg, p),
      mat(jnp.swapaxes(c_re, -1, -2), p, cg), mat(jnp.swapaxes(c_im, -1, -2), p, cg))
    return out[0], out[1:5], out[5:9], out[9:13]


def _s5_ends_kernel(u_ref, e0, e1, e2, e3, o0, o1, o2, o3):
    u = u_ref[0].astype(BF16)
    for e_ref, o_ref in ((e0, o0), (e1, o1), (e2, o2), (e3, o3)):
        o_ref[...] = _dot(u, e_ref[0])


def _s5_ends(u_rows, ends, rt):
    n_rows = u_rows.shape[1]
    e_spec = pl.BlockSpec((1, S5_ROW, S5_PSTATE), lambda n, r: (n, 0, 0))
    o_spec = pl.BlockSpec((rt, S5_PSTATE), lambda n, r: (r, n))
    return pl.pallas_call(
        _s5_ends_kernel,
        grid=(S5_NPAIR, n_rows // rt),
        in_specs=[pl.BlockSpec((1, rt, S5_ROW), lambda n, r: (n, r, 0))] + [e_spec] * 4,
        out_specs=[o_spec] * 4,
        out_shape=[jax.ShapeDtypeStruct((n_rows, N_STATE), F32)] * 4,
        compiler_params=_cparams(2),
        name="s5_chunk_ends",
    )(u_rows, *ends)


def _s5_scan_kernel(efr, efi, ebr, ebi, afr, afi, abr, abi, sfr, sfi, sbr, sbi,
                    pfr, pfi, nbr, nbi, ofr, ofi, obr, obi):
    nc = efr.shape[1]
    a_fr, a_fi, a_br, a_bi = afr[...], afi[...], abr[...], abi[...]

    def body(c, carry):
        fr, fi, br, bi = carry
        rc, rb = pl.ds(c, 1), pl.ds(nc - 1 - c, 1)
        pfr[0, rc, :] = fr
        pfi[0, rc, :] = fi
        nbr[0, rb, :] = br
        nbi[0, rb, :] = bi
        nfr = a_fr * fr - a_fi * fi + efr[0, rc, :]
        nfi = a_fr * fi + a_fi * fr + efi[0, rc, :]
        nbr_ = a_br * br - a_bi * bi + ebr[0, rb, :]
        nbi_ = a_br * bi + a_bi * br + ebi[0, rb, :]
        return nfr, nfi, nbr_, nbi_

    fr, fi, br, bi = lax.fori_loop(0, nc, body, (sfr[0], sfi[0], sbr[0], sbi[0]))
    ofr[0] = fr
    ofi[0] = fi
    obr[0] = br
    obi[0] = bi


def _s5_scan(se, decay, s0, nc, nb):
    cw = 1024
    seq = pl.BlockSpec((1, nc, cw), lambda b, i: (b, 0, i))
    vec = pl.BlockSpec((1, cw), lambda b, i: (0, i))
    st = pl.BlockSpec((1, 1, cw), lambda b, i: (b, 0, i))
    return pl.pallas_call(
        _s5_scan_kernel,
        grid=(nb, N_STATE // cw),
        in_specs=[seq] * 4 + [vec] * 4 + [st] * 4,
        out_specs=[seq] * 4 + [st] * 4,
        out_shape=[jax.ShapeDtypeStruct((nb, nc, N_STATE), F32)] * 4 + [jax.ShapeDtypeStruct((nb, 1, N_STATE), F32)] * 4,
        compiler_params=_cparams(2),
        name="s5_chunk_scan",
    )(*[s.reshape(nb, nc, N_STATE) for s in se], *decay, *[s.reshape(nb, 1, N_STATE) for s in s0])


def _s5_out_kernel(u_ref, top_ref, d_ref, pfr, pfi, nbr, nbi, m0, m1, m2, m3, o_ref):
    u = u_ref[0]
    y = _dot(u.astype(BF16), top_ref[0]) + d_ref[0] * u
    for s_ref, m_ref in ((pfr, m0), (pfi, m1), (nbr, m2), (nbi, m3)):
        y = y + _dot(s_ref[...].astype(BF16), m_ref[0])
    o_ref[0] = jax.nn.gelu(y)


def _s5_out(u_rows, top, d_rows, states, carries, rt):
    n_rows = u_rows.shape[1]
    s_spec = pl.BlockSpec((rt, S5_PSTATE), lambda n, r: (r, n))
    m_spec = pl.BlockSpec((1, S5_PSTATE, S5_ROW), lambda n, r: (n, 0, 0))
    return pl.pallas_call(
        _s5_out_kernel,
        grid=(S5_NPAIR, n_rows // rt),
        in_specs=[pl.BlockSpec((1, rt, S5_ROW), lambda n, r: (n, r, 0)),
                  pl.BlockSpec((1, S5_ROW, S5_ROW), lambda n, r: (n, 0, 0)),
                  pl.BlockSpec((1, 1, S5_ROW), lambda n, r: (n, 0, 0))] + [s_spec] * 4 + [m_spec] * 4,
        out_specs=pl.BlockSpec((1, rt, S5_ROW), lambda n, r: (n, r, 0)),
        out_shape=jax.ShapeDtypeStruct(u_rows.shape, F32),
        compiler_params=_cparams(2),
        name="s5_outputs",
    )(u_rows, top, d_rows, *states, *carries)


def _s5_branch(u_rows, ops, d_skip, s0, nb, seq_len):
    top, ends, carries, decay = ops
    nc = seq_len // S5_CHUNK
    n_rows = nc * nb
    rt = min(n_rows, 512)
    d_rows = jnp.tile(d_skip.reshape(S5_NPAIR, 1, S5_PAIR), (1, S5_CHUNK, 1)).reshape(S5_NPAIR, 1, S5_ROW)
    se = _s5_ends(u_rows, ends, rt)
    scanned = _s5_scan(se, decay, s0, nc, nb)
    states = [s.reshape(n_rows, N_STATE) for s in scanned[:4]]
    y_rows = _s5_out(u_rows, top, d_rows, states, carries, rt)
    return y_rows, [s.reshape(nb, N_STATE) for s in scanned[4:]]


Q_SLOT_ORDER = tuple(h for j in range(GRP) for h in (j, GRP + j))


def _stack_group_queries(q_ref, h, tq):
    lane = lax.broadcasted_iota(jnp.int32, (tq, LANES), 1)
    keep = (lane >= h * HEAD_DIM) & (lane < (h + 1) * HEAD_DIM)
    zero = jnp.zeros((tq, LANES), BF16)
    return jnp.concatenate([jnp.where(keep, q_ref[:, j * LANES:(j + 1) * LANES], zero) for j in range(GRP)], axis=0)


def _sink_lanes(sink_ref, h, tq):
    return jnp.concatenate([jnp.full((1, tq), sink_ref[h * GRP + j] * LOG2E, F32) for j in range(GRP)], axis=1)


def _values_with_ones(vt):
    return jnp.concatenate([vt, jnp.ones((V_ROWS - HEAD_DIM, vt.shape[1]), BF16)], axis=0)


def _store_heads(o_ref, h, o_t, tq):
    for j in range(GRP):
        head = h * GRP + j
        o_ref[0, head * HEAD_DIM:(head + 1) * HEAD_DIM, :] = o_t[:, j * tq:(j + 1) * tq].astype(o_ref.dtype)


def _attn_full_kernel(*refs, tq, use_sink):
    if use_sink:
        sink_ref, q_ref, k_ref, vt_ref, o_ref, m_ref, acc_ref, sa_ref, sb_ref = refs
    else:
        q_ref, k_ref, vt_ref, o_ref, m_ref, acc_ref, sa_ref, sb_ref = refs
    n_chunks = k_ref.shape[1]
    assert n_chunks % 2 == 1
    rows = GRP * tq
    qs = [_stack_group_queries(q_ref, h, tq) for h in range(N_KV)]
    for h in range(N_KV):
        if use_sink:
            m_ref[h] = _sink_lanes(sink_ref, h, tq)
            acc_ref[h] = jnp.concatenate([jnp.zeros((HEAD_DIM, rows), F32), jnp.ones((V_ROWS - HEAD_DIM, rows), F32)], axis=0)
        else:
            m_ref[h] = jnp.full((1, rows), NEG_INF, F32)
            acc_ref[h] = jnp.zeros((V_ROWS, rows), F32)

    def scores(c, s_ref):
        kc = k_ref[0, c]
        for h in range(N_KV):
            s_ref[h] = lax.dot_general(kc, qs[h], NT_DIMS, preferred_element_type=F32)

    def consume(c, s_ref):
        for h in range(N_KV):
            s = s_ref[h]
            m_old = m_ref[h]
            m_new = jnp.maximum(m_old, jnp.max(s, axis=0, keepdims=True))
            p = jnp.exp2(s - m_new).astype(BF16)
            alpha = jnp.exp2(m_old - m_new)
            lhs = _values_with_ones(vt_ref[0, c, h * HEAD_DIM:(h + 1) * HEAD_DIM, :])
            acc_ref[h] = alpha * acc_ref[h] + _dot(lhs, p)
            m_ref[h] = m_new

    scores(0, sa_ref)

    def body(k, carry):
        c = 2 * k
        scores(c + 1, sb_ref)
        consume(c, sa_ref)
        scores(c + 2, sa_ref)
        consume(c + 1, sb_ref)
        return carry

    lax.fori_loop(0, n_chunks // 2, body, 0, unroll=True)
    consume(n_chunks - 1, sa_ref)
    for h in range(N_KV):
        acc = acc_ref[h]
        _store_heads(o_ref, h, acc[:HEAD_DIM] / acc[HEAD_DIM:HEAD_DIM + 1], tq)


def _attn_band_kernel(sink_ref, q_ref, k_ref, vt_ref, o_ref, *, tq, n_lat, n_ctx):
    i = pl.program_id(1)
    n_band = tq // WINDOW + 2
    lat_chunks, ctx_chunks = n_lat // WINDOW, n_ctx // WINDOW
    c0 = jnp.clip(i * (tq // WINDOW) - 1, 0, lat_chunks - n_band)
    rows = GRP * tq
    row = lax.broadcasted_iota(jnp.int32, (n_band * WINDOW, rows), 0)
    col = lax.broadcasted_iota(jnp.int32, (n_band * WINDOW, rows), 1)
    in_band = jnp.abs((c0 * WINDOW + row) - (i * tq + (col & (tq - 1)))) <= WINDOW
    k_band = k_ref[0, pl.ds(c0, n_band)].reshape(n_band * WINDOW, LANES)
    k_ctx = k_ref[0, lat_chunks:lat_chunks + ctx_chunks].reshape(n_ctx, LANES)
    qs = [_stack_group_queries(q_ref, h, tq) for h in range(N_KV)]
    s_band = [jnp.where(in_band, lax.dot_general(k_band, qs[h], NT_DIMS, preferred_element_type=F32), NEG_INF)
              for h in range(N_KV)]
    s_ctx = [lax.dot_general(k_ctx, qs[h], NT_DIMS, preferred_element_type=F32) for h in range(N_KV)]
    for h in range(N_KV):
        sink = _sink_lanes(sink_ref, h, tq)
        m = jnp.maximum(jnp.maximum(jnp.max(s_band[h], axis=0, keepdims=True),
                                    jnp.max(s_ctx[h], axis=0, keepdims=True)), sink)
        p_band = jnp.exp2(s_band[h] - m).astype(BF16)
        p_ctx = jnp.exp2(s_ctx[h] - m).astype(BF16)
        hd = slice(h * HEAD_DIM, (h + 1) * HEAD_DIM)
        acc = jnp.zeros((V_ROWS, rows), F32)
        for t in range(n_band):
            acc = acc + _dot(_values_with_ones(vt_ref[0, c0 + t, hd, :]), p_band[t * WINDOW:(t + 1) * WINDOW])
        for t in range(ctx_chunks):
            acc = acc + _dot(_values_with_ones(vt_ref[0, lat_chunks + t, hd, :]), p_ctx[t * WINDOW:(t + 1) * WINDOW])
        den = acc[HEAD_DIM:HEAD_DIM + 1] + jnp.exp2(sink - m)
        _store_heads(o_ref, h, acc[:HEAD_DIM] / den, tq)


def _attention(q, k, v, sink, nb, seq_len, tq, tk, band_ctx=None):
    n_k = k.shape[1]
    kc = k.reshape(nb, n_k // tk, tk, D_KV)
    vt = v.reshape(nb, n_k // tk, tk, D_KV).transpose(0, 1, 3, 2)
    tiles = seq_len // tq
    q_spec = pl.BlockSpec((tq, D_ATT), lambda b, i: (b * tiles + i, 0))
    k_spec = pl.BlockSpec((1, n_k // tk, tk, D_KV), lambda b, i: (b, 0, 0, 0))
    vt_spec = pl.BlockSpec((1, n_k // tk, D_KV, tk), lambda b, i: (b, 0, 0, 0))
    in_specs, args = [q_spec, k_spec, vt_spec], [q, kc, vt]
    if sink is not None:
        in_specs, args = [pl.BlockSpec(memory_space=pltpu.SMEM)] + in_specs, [sink] + args
    rows = GRP * tq
    if band_ctx is None:
        body = functools.partial(_attn_full_kernel, tq=tq, use_sink=sink is not None)
        scratch = [pltpu.VMEM((N_KV, 1, rows), F32), pltpu.VMEM((N_KV, V_ROWS, rows), F32),
                   pltpu.VMEM((N_KV, tk, rows), F32), pltpu.VMEM((N_KV, tk, rows), F32)]
        name = "attention_full"
    else:
        body = functools.partial(_attn_band_kernel, tq=tq, n_lat=seq_len, n_ctx=band_ctx)
        scratch = []
        name = "attention_band"
    o_t = pl.pallas_call(
        body,
        grid=(nb, tiles),
        in_specs=in_specs,
        out_specs=pl.BlockSpec((1, D_ATT, tq), lambda b, i: (b, 0, i)),
        out_shape=jax.ShapeDtypeStruct((nb, D_ATT, seq_len), BF16),
        scratch_shapes=scratch,
        compiler_params=_cparams(2),
        name=name,
    )(*args)
    return o_t.transpose(0, 2, 1).reshape(nb * seq_len, D_ATT)


def _layer_norm(z, g, b):
    mu = jnp.mean(z, axis=-1, keepdims=True)
    zc = z - mu
    var = jnp.mean(zc * zc, axis=-1, keepdims=True)
    return zc * lax.rsqrt(var + LN_EPS) * g + b


def _merge_kernel(x_ref, g1_ref, ya_ref, yw_ref, yg_ref, gate_ref,
                  wglu_ref, wa_ref, ww_ref, wg_ref, wout_ref, lng_ref, lnb_ref, o_ref, ya_scr):
    ya = _chunk_rows_to_tokens(ya_ref, ya_scr)
    ya = ya * jax.nn.sigmoid(_dot(ya.astype(BF16), wglu_ref[...]))

    def gate(c):
        return gate_ref[:, c * D_MODEL:(c + 1) * D_MODEL].astype(F32)

    m = (gate(0) * _dot(ya.astype(BF16), wa_ref[...])
         + gate(1) * _dot(yw_ref[...], ww_ref[...])
         + gate(2) * _dot(yg_ref[...], wg_ref[...]))
    f = _dot(m.astype(BF16), wout_ref[...])
    o_ref[...] = _layer_norm(DEEPNORM_ALPHA * x_ref[...] + g1_ref[0] * f, lng_ref[...], lnb_ref[...])


def _merge(x, mod_l, row_of_tile, ya_rows, yw, yg, gates, w_glu, w_a, w_w, w_g, w_out, ln_g, ln_b):
    n_tok = x.shape[0]
    tm = ROW_TILE
    row = lambda w: pl.BlockSpec((tm, w), lambda i: (i, 0))
    return pl.pallas_call(
        _merge_kernel,
        grid=(n_tok // tm,),
        in_specs=[row(D_MODEL), _mod_spec(2, row_of_tile),
                  pl.BlockSpec((S5_NPAIR, CHUNKS_PER_TILE, S5_ROW), lambda i: (0, i, 0)),
                  row(D_ATT), row(D_ATT), row(3 * D_MODEL),
                  _resident((D_SSM, D_SSM)), _resident((D_SSM, D_MODEL)), _resident((D_ATT, D_MODEL)),
                  _resident((D_ATT, D_MODEL)), _resident((D_MODEL, D_MODEL)),
                  _resident((1, D_MODEL)), _resident((1, D_MODEL))],
        out_specs=row(D_MODEL),
        out_shape=jax.ShapeDtypeStruct((n_tok, D_MODEL), F32),
        scratch_shapes=[pltpu.VMEM((LANE_SLABS, tm, LANES), F32)],
        compiler_params=_cparams(1),
        name="merge_residual",
    )(x, mod_l, ya_rows, yw, yg, gates, w_glu, w_a, w_w, w_g, w_out, ln_g, ln_b)


def _mlp_kernel(x_ref, sh_ref, sc_ref, g2_ref, wup_ref, wdn_ref, lng_ref, lnb_ref, o_ref):
    x = x_ref[...]
    h = (x * (1.0 + sc_ref[0]) + sh_ref[0]).astype(BF16)
    ff_tile = 1024
    acc = jnp.zeros(x.shape, F32)
    for c in range(D_FF // ff_tile):
        up = jnp.maximum(_dot(h, wup_ref[:, c * ff_tile:(c + 1) * ff_tile]), 0.0)
        acc = acc + _dot((up * up).astype(BF16), wdn_ref[c * ff_tile:(c + 1) * ff_tile, :])
    o_ref[...] = _layer_norm(DEEPNORM_ALPHA * x + g2_ref[0] * acc, lng_ref[...], lnb_ref[...])


def _mlp(x, mod_l, row_of_tile, w_up, w_down, ln_g, ln_b):
    n_tok = x.shape[0]
    tm = ROW_TILE
    row = pl.BlockSpec((tm, D_MODEL), lambda i: (i, 0))
    return pl.pallas_call(
        _mlp_kernel,
        grid=(n_tok // tm,),
        in_specs=[row, _mod_spec(3, row_of_tile), _mod_spec(4, row_of_tile), _mod_spec(5, row_of_tile),
                  _resident((D_MODEL, D_FF)), _resident((D_FF, D_MODEL)),
                  _resident((1, D_MODEL)), _resident((1, D_MODEL))],
        out_specs=row,
        out_shape=jax.ShapeDtypeStruct((n_tok, D_MODEL), F32),
        compiler_params=_cparams(1),
        name="mlp_residual",
    )(x, mod_l, mod_l, mod_l, w_up, w_down, ln_g, ln_b)


def _rope_tables(n_tok):
    rows = n_tok // GRID_W
    row = jnp.repeat(jnp.arange(rows, dtype=F32), GRID_W)
    col = jnp.tile(jnp.arange(GRID_W, dtype=F32), rows)
    n_freq = HEAD_DIM // 4
    inv = ROPE_BASE ** (-jnp.arange(n_freq, dtype=F32) / n_freq)
    ang = jnp.concatenate([row[:, None] * inv, col[:, None] * inv], axis=-1)
    cos, sin = jnp.cos(ang), jnp.sin(ang)
    cos_t = jnp.tile(jnp.concatenate([cos, cos], axis=-1), (1, LANES // HEAD_DIM))
    sin_t = jnp.tile(jnp.concatenate([-sin, sin], axis=-1), (1, LANES // HEAD_DIM))
    return cos_t, sin_t


def _block_diag_ones(n):
    idx = np.arange(n) // HEAD_DIM
    return jnp.asarray(idx[:, None] == idx[None, :], dtype=BF16)


def _permute_q_columns(w):
    def perm(block):
        return block.reshape(D_MODEL, N_HEADS, HEAD_DIM)[:, np.array(Q_SLOT_ORDER), :].reshape(D_MODEL, D_ATT)
    return jnp.concatenate([w[:, :O_QW], perm(w[:, O_QW:O_KW]), w[:, O_KW:O_QG], perm(w[:, O_QG:O_KG]), w[:, O_KG:]], axis=1)


def _layer(x, lw, mod_l, row_of_tile, nb, seq_len, ctx, rope_tabs):
    u_rows, qw, qg, gates, *kv = _in_projection(
        x, mod_l, row_of_tile, lw['w_in'], lw['qn'], lw['kn'], lw['bdq'], lw['bdk'], rope_tabs, seq_len,
        keep_f32_kv=ctx is None)
    if ctx is None:
        s0 = [jnp.zeros((nb, N_STATE), F32)] * 4
    else:
        s0 = [ctx[0][:, d, part].reshape(nb, N_STATE) for d in (0, 1) for part in (0, 1)]
    ya_rows, s_fin = _s5_branch(u_rows, lw['s5_ops'], lw['d_skip'], s0, nb, seq_len)

    def seq(t, n):
        return t.reshape(nb, n, D_KV)

    kw, vw, kg, vg = [seq(t, seq_len) for t in kv[:4]]
    if ctx is None:
        yw = _attention(qw, kw, vw, lw['sink'], nb, seq_len, seq_len, seq_len)
        yg = _attention(qg, kg, vg, None, nb, seq_len, seq_len, seq_len)
        new_ctx = (s_fin,) + tuple(kv[4:])
    else:
        _, k_wc, v_wc, k_gc, v_gc = ctx
        n_ctx = k_wc.shape[1]

        def with_ctx(new, cached):
            return jnp.concatenate([new, seq(cached, n_ctx).astype(BF16)], axis=1)

        yw = _attention(qw, with_ctx(kw, k_wc), with_ctx(vw, v_wc), lw['sink'], nb, seq_len, 256, WINDOW, band_ctx=n_ctx)
        yg = _attention(qg, with_ctx(kg, k_gc), with_ctx(vg, v_gc), None, nb, seq_len, 256, 256)
        new_ctx = None
    x1 = _merge(x, mod_l, row_of_tile, ya_rows, yw, yg, gates,
                lw['w_glu'], lw['w_br_ssm'], lw['w_br_win'], lw['w_br_glb'], lw['w_out'], lw['ln1_g'], lw['ln1_b'])
    x2 = _mlp(x1, mod_l, row_of_tile, lw['w_up'], lw['w_down'], lw['ln2_g'], lw['ln2_b'])
    return x2, new_ctx


def kernel(x_prompt, x_sample, state_ssm, cache_k_win, cache_v_win, cache_k_glb, cache_v_glb, c, c_ctx, w_mod, b_mod, w_in, ssm_lam_re, ssm_lam_im, ssm_log_step, ssm_b_re, ssm_b_im, ssm_c_re, ssm_c_im, ssm_d, w_glu, sink_win, q_norm_glb, k_norm_glb, w_br_ssm, w_br_win, w_br_glb, w_out, ln1_g, ln1_b, w_up, w_down, ln2_g, ln2_b):
    n_ctx_b, ctx_len, _ = x_prompt.shape
    n_lat_b, lat_len, _ = x_sample.shape
    assert ctx_len % ROW_TILE == 0 or ROW_TILE % ctx_len == 0
    assert lat_len % ROW_TILE == 0 and (n_ctx_b * ctx_len) % ROW_TILE == 0

    cond8 = jnp.zeros((8, D_MODEL), F32).at[0].set(c_ctx).at[1:1 + n_lat_b].set(c)
    mod = _modulation(cond8, w_mod, b_mod).reshape(DEPTH, 8, 1, 6 * D_MODEL)
    rope_tabs = _rope_tables(lat_len)
    bdq, bdk = _block_diag_ones(D_ATT), _block_diag_ones(D_KV)
    lat_tiles = lat_len // ROW_TILE
    ctx_row = lambda i: 0
    lat_row = lambda i: 1 + i // lat_tiles

    xp = x_prompt.reshape(n_ctx_b * ctx_len, D_MODEL)
    xs = x_sample.reshape(n_lat_b * lat_len, D_MODEL)
    new_ssm, new_kw, new_vw, new_kg, new_vg = [], [], [], [], []
    for l in range(DEPTH):
        lw = dict(
            w_in=_permute_q_columns(w_in[l]).astype(BF16),
            qn=jnp.tile(q_norm_glb[l], N_HEADS).reshape(1, D_ATT), kn=jnp.tile(k_norm_glb[l], N_KV).reshape(1, D_KV),
            bdq=bdq, bdk=bdk,
            s5_ops=_s5_operators(ssm_lam_re[l], ssm_lam_im[l], ssm_log_step[l],
                                 ssm_b_re[l], ssm_b_im[l], ssm_c_re[l], ssm_c_im[l]),
            d_skip=ssm_d[l],
            sink=sink_win[l],
            w_glu=w_glu[l].astype(BF16), w_br_ssm=w_br_ssm[l].astype(BF16), w_br_win=w_br_win[l].astype(BF16),
            w_br_glb=w_br_glb[l].astype(BF16), w_out=w_out[l].astype(BF16),
            ln1_g=ln1_g[l].reshape(1, D_MODEL), ln1_b=ln1_b[l].reshape(1, D_MODEL),
            w_up=w_up[l].astype(BF16), w_down=w_down[l].astype(BF16),
            ln2_g=ln2_g[l].reshape(1, D_MODEL), ln2_b=ln2_b[l].reshape(1, D_MODEL),
        )
        xp, (s_fin, kw, vw, kg, vg) = _layer(xp, lw, mod[l], ctx_row, n_ctx_b, ctx_len, None, None)
        new_ssm.append(jnp.stack(s_fin, axis=1).reshape(n_ctx_b, 2, 2, SSM_GROUPS, SSM_STATE))
        for acc, t in ((new_kw, kw), (new_vw, vw), (new_kg, kg), (new_vg, vg)):
            acc.append(t.reshape(n_ctx_b, ctx_len, N_KV, HEAD_DIM))
        ctx = (state_ssm[:, l], cache_k_win[:, l], cache_v_win[:, l], cache_k_glb[:, l], cache_v_glb[:, l])
        xs, _ = _layer(xs, lw, mod[l], lat_row, n_lat_b, lat_len, ctx, rope_tabs)
    return (xp.reshape(x_prompt.shape), xs.reshape(x_sample.shape),
            jnp.stack(new_ssm, axis=1), jnp.stack(new_kw, axis=1), jnp.stack(new_vw, axis=1),
            jnp.stack(new_kg, axis=1), jnp.stack(new_vg, axis=1))
```

```python
import functools

import jax
import jax.numpy as jnp
import numpy as np
from jax import lax
from jax.experimental import pallas as pl
from jax.experimental.pallas import tpu as pltpu

F32 = jnp.float32
BF16 = jnp.bfloat16

D_MODEL = 1024
DEPTH = 2
GRID_W = 64
HEAD_DIM = 64
D_SSM = 512
SSM_GROUP_CH = 16
SSM_GROUPS = 32
SSM_STATE = 64
N_HEADS = 8
N_KV = 2
GRP = N_HEADS // N_KV
D_ATT = N_HEADS * HEAD_DIM
D_KV = N_KV * HEAD_DIM
WINDOW = 128
ROPE_BASE = 10000.0
D_FF = 4 * D_MODEL
LN_EPS = 1e-5
RMS_EPS = 1e-6
ATTN_SCALE = HEAD_DIM ** -0.5
DEEPNORM_ALPHA = (2.0 * DEPTH) ** 0.25
NEG_INF = -1e30
LOG2E = 1.4426950408889634
Q_SCALE = ATTN_SCALE * LOG2E
V_ROWS = HEAD_DIM + 16
N_IN = D_SSM + 2 * (D_ATT + 2 * D_KV) + 3 * D_MODEL
O_U = 0
O_QW = O_U + D_SSM
O_KW = O_QW + D_ATT
O_VW = O_KW + D_KV
O_QG = O_VW + D_KV
O_KG = O_QG + D_ATT
O_VG = O_KG + D_KV
O_GATE = O_VG + D_KV

S5_CHUNK = 16
S5_PAIR = 2 * SSM_GROUP_CH
S5_NPAIR = SSM_GROUPS // 2
S5_ROW = S5_CHUNK * S5_PAIR
S5_PSTATE = 2 * SSM_STATE
N_STATE = SSM_GROUPS * SSM_STATE

LANES = 128
ROW_TILE = 512
FF_TILE = 1024
CHUNKS_PER_TILE = ROW_TILE // S5_CHUNK
LANE_SLABS = D_SSM // LANES
PAIRS_PER_SLAB = LANES // S5_PAIR
VMEM_LIMIT = 56 * 1024 * 1024
NT_DIMS = (((1,), (1,)), ((), ()))


def _cparams(n_axes):
    return pltpu.CompilerParams(dimension_semantics=("arbitrary",) * n_axes, vmem_limit_bytes=VMEM_LIMIT)


def _resident(shape):
    nd = len(shape)
    return pl.BlockSpec(shape, lambda *_: (0,) * nd, pipeline_mode=pl.Buffered(1))


def _dot(a, b):
    return jnp.dot(a, b, preferred_element_type=F32)


def _mod_kernel(c_ref, w_ref, b_ref, o_ref):
    c = c_ref[...]
    a = (c * jax.nn.sigmoid(c)).astype(BF16)
    o_ref[0] = _dot(a, w_ref[0].astype(BF16)) + b_ref[0]


def _modulation(cond8, w_mod, b_mod):
    tn = 512
    n_out = w_mod.shape[-1]
    return pl.pallas_call(
        _mod_kernel,
        grid=(DEPTH, n_out // tn),
        in_specs=[
            pl.BlockSpec((8, D_MODEL), lambda l, n: (0, 0)),
            pl.BlockSpec((1, D_MODEL, tn), lambda l, n: (l, 0, n)),
            pl.BlockSpec((1, 1, tn), lambda l, n: (l, 0, n)),
        ],
        out_specs=pl.BlockSpec((1, 8, tn), lambda l, n: (l, 0, n)),
        out_shape=jax.ShapeDtypeStruct((DEPTH, 8, n_out), F32),
        compiler_params=_cparams(2),
        name="modulation",
    )(cond8, w_mod, b_mod.reshape(DEPTH, 1, n_out))


def _mod_spec(chunk, row_of_tile):
    return pl.BlockSpec((1, 1, D_MODEL), lambda i: (row_of_tile(i), 0, chunk))


def _tokens_to_chunk_rows(u, scr, out_ref):
    n_chunks = u.shape[0] // S5_CHUNK
    for j in range(LANE_SLABS):
        scr[j] = u[:, j * LANES:(j + 1) * LANES]
    pieces = [[None] * S5_CHUNK for _ in range(S5_NPAIR)]
    for j in range(LANE_SLABS):
        for t in range(S5_CHUNK):
            step_rows = scr[j, pl.ds(t, n_chunks, stride=S5_CHUNK), :]
            for m in range(PAIRS_PER_SLAB):
                pieces[j * PAIRS_PER_SLAB + m][t] = step_rows[:, m * S5_PAIR:(m + 1) * S5_PAIR]
    for n in range(S5_NPAIR):
        out_ref[n] = jnp.concatenate(pieces[n], axis=1)


def _chunk_rows_to_tokens(y_ref, scr):
    n_chunks = y_ref.shape[1]
    for j in range(LANE_SLABS):
        for t in range(S5_CHUNK):
            piece = jnp.concatenate(
                [y_ref[j * PAIRS_PER_SLAB + m, :, t * S5_PAIR:(t + 1) * S5_PAIR] for m in range(PAIRS_PER_SLAB)], axis=1)
            scr[j, pl.ds(t, n_chunks, stride=S5_CHUNK), :] = piece
    return jnp.concatenate([scr[j] for j in range(LANE_SLABS)], axis=1)


def _head_rms(x, gain, ones_bd):
    ss = _dot((x * x).astype(BF16), ones_bd)
    return x * lax.rsqrt(ss * (1.0 / HEAD_DIM) + RMS_EPS) * gain


def _rope_chunk(xc, cos_t, sin_t):
    lane = lax.broadcasted_iota(jnp.int32, xc.shape, 1)
    first_half = (lane & (HEAD_DIM - 1)) < (HEAD_DIM // 2)
    partner = jnp.where(first_half, pltpu.roll(xc, LANES - HEAD_DIM // 2, 1), pltpu.roll(xc, HEAD_DIM // 2, 1))
    return xc * cos_t + partner * sin_t


def _inproj_kernel(*refs, rope):
    n_in = 10 if rope else 8
    x_ref, sh_ref, sc_ref, w_ref, qn_ref, kn_ref, bdq_ref, bdk_ref = refs[:8]
    u_ref, qw_ref, qg_ref, gate_ref = refs[n_in:n_in + 4]
    kv_refs, u_scr = refs[n_in + 4:-1], refs[-1]
    h = (x_ref[...] * (1.0 + sc_ref[0]) + sh_ref[0]).astype(BF16)

    def proj(lo, width):
        return _dot(h, w_ref[:, lo:lo + width])

    def finish(x, out_refs, scale=1.0, rotate=rope):
        n = x.shape[-1]
        for c in range(n // LANES):
            xc = x[:, c * LANES:(c + 1) * LANES]
            if rotate:
                xc = _rope_chunk(xc, refs[8][...], refs[9][...])
            if scale != 1.0:
                xc = xc * scale
            for out_ref in out_refs:
                out_ref[:, c * LANES:(c + 1) * LANES] = xc.astype(out_ref.dtype)

    _tokens_to_chunk_rows(proj(O_U, D_SSM), u_scr, u_ref)
    finish(proj(O_QW, D_ATT), [qw_ref], Q_SCALE)
    finish(proj(O_KW, D_KV), kv_refs[0::4])
    finish(proj(O_VW, D_KV), kv_refs[1::4], rotate=False)
    finish(_head_rms(proj(O_QG, D_ATT), qn_ref[...], bdq_ref[...]), [qg_ref], Q_SCALE)
    finish(_head_rms(proj(O_KG, D_KV), kn_ref[...], bdk_ref[...]), kv_refs[2::4])
    finish(proj(O_VG, D_KV), kv_refs[3::4], rotate=False)
    for c in range(3):
        gate = jax.nn.sigmoid(proj(O_GATE + c * D_MODEL, D_MODEL))
        gate_ref[:, c * D_MODEL:(c + 1) * D_MODEL] = gate.astype(gate_ref.dtype)


def _in_projection(x, mod_l, row_of_tile, w_in, qn, kn, bdq, bdk, rope_tabs, seq_len, keep_f32_kv):
    n_tok = x.shape[0]
    tm = ROW_TILE
    rope = rope_tabs is not None
    row = lambda w: pl.BlockSpec((tm, w), lambda i: (i, 0))
    in_specs = [
        row(D_MODEL),
        _mod_spec(0, row_of_tile), _mod_spec(1, row_of_tile),
        _resident((D_MODEL, N_IN)),
        _resident((1, D_ATT)), _resident((1, D_KV)),
        _resident((D_ATT, D_ATT)), _resident((D_KV, D_KV)),
    ]
    args = [x, mod_l, mod_l, w_in, qn, kn, bdq, bdk]
    if rope:
        tiles_per_seq = seq_len // tm
        tab = pl.BlockSpec((tm, LANES), lambda i: (i % tiles_per_seq, 0))
        in_specs += [tab, tab]
        args += list(rope_tabs)
    kv_dtypes = [BF16] * 4 + ([F32] * 4 if keep_f32_kv else [])
    out_shape = ([jax.ShapeDtypeStruct((S5_NPAIR, n_tok // S5_CHUNK, S5_ROW), F32),
                  jax.ShapeDtypeStruct((n_tok, D_ATT), BF16),
                  jax.ShapeDtypeStruct((n_tok, D_ATT), BF16),
                  jax.ShapeDtypeStruct((n_tok, 3 * D_MODEL), BF16)]
                 + [jax.ShapeDtypeStruct((n_tok, D_KV), dt) for dt in kv_dtypes])
    out_specs = ([pl.BlockSpec((S5_NPAIR, CHUNKS_PER_TILE, S5_ROW), lambda i: (0, i, 0))]
                 + [row(s.shape[1]) for s in out_shape[1:]])
    return pl.pallas_call(
        functools.partial(_inproj_kernel, rope=rope),
        grid=(n_tok // tm,),
        in_specs=in_specs,
        out_specs=out_specs,
        out_shape=out_shape,
        scratch_shapes=[pltpu.VMEM((LANE_SLABS, tm, LANES), F32)],
        compiler_params=_cparams(1),
        name="in_projection",
    )(*args)


def _zoh(lr, li, ls):
    dt = jnp.exp(ls)
    mag = jnp.exp(lr * dt)
    ar, ai = mag * jnp.cos(li * dt), mag * jnp.sin(li * dt)
    den = lr * lr + li * li
    fr = ((ar - 1.0) * lr + ai * li) / den
    fi = (ai * lr - (ar - 1.0) * li) / den
    return ar, ai, fr, fi


def _powers(ar, ai, n):
    out = [(jnp.ones_like(ar), jnp.zeros_like(ar))]
    for _ in range(n):
        pr, pi = out[-1]
        out.append((pr * ar - pi * ai, pr * ai + pi * ar))
    return out


def _s5_operator_kernel(lr_ref, li_ref, ls_ref, btr_ref, bti_ref, cr_ref, ci_ref,
                        top_ref, e0, e1, e2, e3, m0, m1, m2, m3, d0, d1, d2, d3, ext_scr):
    tc, cg, p = S5_CHUNK, SSM_GROUP_CH, SSM_STATE
    ends = ((e0, e1), (e2, e3))
    carries = ((m0, m1), (m2, m3))
    decays = ((d0, d1), (d2, d3))
    ext_scr[...] = jnp.zeros(ext_scr.shape, F32)
    zero_half = jnp.zeros((cg, p), F32)

    def place(out_ref, t, g2, val):
        halves = [val, zero_half] if g2 == 0 else [zero_half, val]
        out_ref[0, t * S5_PAIR + g2 * cg:t * S5_PAIR + (g2 + 1) * cg, :] = jnp.concatenate(halves, axis=1).astype(BF16)

    for d in range(2):
        for g2 in range(2):
            ar, ai, fr, fi = _zoh(lr_ref[d, 0, g2], li_ref[d, 0, g2], ls_ref[d, 0, g2])
            btr, bti = btr_ref[d, 0, g2], bti_ref[d, 0, g2]
            bbr, bbi = fr * btr - fi * bti, fr * bti + fi * btr
            cr, ci = cr_ref[d, 0, g2], ci_ref[d, 0, g2]
            pw = _powers(ar, ai, tc)
            group_rows = slice(g2 * cg, (g2 + 1) * cg)
            for j in range(tc + 1):
                pr, pi = pw[j]
                car, cai = cr * pr - ci * pi, cr * pi + ci * pr
                if j >= 1:
                    t = j - 1 if d == 0 else tc - j
                    place(carries[d][0], t, g2, car)
                    place(carries[d][1], t, g2, -cai)
                if j == tc:
                    break
                t = tc - 1 - j if d == 0 else j
                place(ends[d][0], t, g2, pr * bbr - pi * bbi)
                place(ends[d][1], t, g2, pr * bbi + pi * bbr)
                kt = (lax.dot_general(bbr, car, NT_DIMS, precision=lax.Precision.HIGHEST, preferred_element_type=F32)
                      - lax.dot_general(bbi, cai, NT_DIMS, precision=lax.Precision.HIGHEST, preferred_element_type=F32))
                slot = tc - 1 + j if d == 0 else tc - 1 - j
                lanes = slice(slot * S5_PAIR + g2 * cg, slot * S5_PAIR + (g2 + 1) * cg)
                if d == 1 and j == 0:
                    ext_scr[group_rows, lanes] = ext_scr[group_rows, lanes] + kt
                else:
                    ext_scr[group_rows, lanes] = kt
            for part in range(2):
                decays[d][part][:, g2 * p:(g2 + 1) * p] = pw[tc][part]
    for t in range(tc):
        for g2 in range(2):
            rows = slice(t * S5_PAIR + g2 * cg, t * S5_PAIR + (g2 + 1) * cg)
            window = slice((tc - 1 - t) * S5_PAIR, (tc - 1 - t) * S5_PAIR + S5_ROW)
            top_ref[0, rows, :] = ext_scr[g2 * cg:(g2 + 1) * cg, window].astype(BF16)


def _s5_operators(lam_re, lam_im, log_step, b_re, b_im, c_re, c_im):
    g, p, cg = SSM_GROUPS, SSM_STATE, SSM_GROUP_CH
    ls = jnp.broadcast_to(log_step[:, :, None], (2, g, p))

    def rows(v):
        return v.reshape(2, S5_NPAIR, 2, 1, p)

    def mat(v):
        return v.reshape(2, S5_NPAIR, 2, cg, p)

    def spec(*tail):
        return pl.BlockSpec((2, 1, 2) + tail, lambda n: (0, n, 0, 0, 0))

    state_op = pl.BlockSpec((1, S5_ROW, S5_PSTATE), lambda n: (n, 0, 0))
    out = pl.pallas_call(
        _s5_operator_kernel,
        grid=(S5_NPAIR,),
        in_specs=[spec(1, p)] * 3 + [spec(cg, p)] * 4,
        out_specs=([pl.BlockSpec((1, S5_ROW, S5_ROW), lambda n: (n, 0, 0))] + [state_op] * 8
                   + [pl.BlockSpec((1, S5_PSTATE), lambda n: (0, n))] * 4),
        out_shape=([jax.ShapeDtypeStruct((S5_NPAIR, S5_ROW, S5_ROW), BF16)]
                   + [jax.ShapeDtypeStruct((S5_NPAIR, S5_ROW, S5_PSTATE), BF16)] * 8
                   + [jax.ShapeDtypeStruct((1, N_STATE), F32)] * 4),
        scratch_shapes=[pltpu.VMEM((S5_PAIR, 2 * S5_ROW), F32)],
        compiler_params=_cparams(1),
        name="s5_operators",
    )(rows(lam_re), rows(lam_im), rows(ls),
      mat(jnp.swapaxes(b_re, -1, -2)), mat(jnp.swapaxes(b_im, -1, -2)), mat(c_re), mat(c_im))
    return out[0], out[1:5], out[5:9], out[9:13]


def _s5_ends_kernel(u_ref, e0, e1, e2, e3, o0, o1, o2, o3):
    u = u_ref[0].astype(BF16)
    for e_ref, o_ref in ((e0, o0), (e1, o1), (e2, o2), (e3, o3)):
        o_ref[...] = _dot(u, e_ref[0])


def _s5_ends(u_rows, ends, rt):
    n_rows = u_rows.shape[1]
    e_spec = pl.BlockSpec((1, S5_ROW, S5_PSTATE), lambda n, r: (n, 0, 0))
    o_spec = pl.BlockSpec((rt, S5_PSTATE), lambda n, r: (r, n))
    return pl.pallas_call(
        _s5_ends_kernel,
        grid=(S5_NPAIR, n_rows // rt),
        in_specs=[pl.BlockSpec((1, rt, S5_ROW), lambda n, r: (n, r, 0))] + [e_spec] * 4,
        out_specs=[o_spec] * 4,
        out_shape=[jax.ShapeDtypeStruct((n_rows, N_STATE), F32)] * 4,
        compiler_params=_cparams(2),
        name="s5_chunk_ends",
    )(u_rows, *ends)


def _s5_scan_kernel(efr, efi, ebr, ebi, afr, afi, abr, abi, sfr, sfi, sbr, sbi,
                    pfr, pfi, nbr, nbi, ofr, ofi, obr, obi):
    nc = efr.shape[1]
    a_fr, a_fi, a_br, a_bi = afr[...], afi[...], abr[...], abi[...]

    def body(c, carry):
        fr, fi, br, bi = carry
        rc, rb = pl.ds(c, 1), pl.ds(nc - 1 - c, 1)
        pfr[0, rc, :] = fr
        pfi[0, rc, :] = fi
        nbr[0, rb, :] = br
        nbi[0, rb, :] = bi
        nfr = a_fr * fr - a_fi * fi + efr[0, rc, :]
        nfi = a_fr * fi + a_fi * fr + efi[0, rc, :]
        nbr_ = a_br * br - a_bi * bi + ebr[0, rb, :]
        nbi_ = a_br * bi + a_bi * br + ebi[0, rb, :]
        return nfr, nfi, nbr_, nbi_

    fr, fi, br, bi = lax.fori_loop(0, nc, body, (sfr[0], sfi[0], sbr[0], sbi[0]))
    ofr[0] = fr
    ofi[0] = fi
    obr[0] = br
    obi[0] = bi


def _s5_scan(se, decay, s0, nc, nb):
    cw = 1024
    seq = pl.BlockSpec((1, nc, cw), lambda b, i: (b, 0, i))
    vec = pl.BlockSpec((1, cw), lambda b, i: (0, i))
    st = pl.BlockSpec((1, 1, cw), lambda b, i: (b, 0, i))
    return pl.pallas_call(
        _s5_scan_kernel,
        grid=(nb, N_STATE // cw),
        in_specs=[seq] * 4 + [vec] * 4 + [st] * 4,
        out_specs=[seq] * 4 + [st] * 4,
        out_shape=[jax.ShapeDtypeStruct((nb, nc, N_STATE), F32)] * 4 + [jax.ShapeDtypeStruct((nb, 1, N_STATE), F32)] * 4,
        compiler_params=_cparams(2),
        name="s5_chunk_scan",
    )(*[s.reshape(nb, nc, N_STATE) for s in se], *decay, *[s.reshape(nb, 1, N_STATE) for s in s0])


def _s5_out_kernel(u_ref, top_ref, d_ref, pfr, pfi, nbr, nbi, m0, m1, m2, m3, o_ref):
    u = u_ref[0]
    y = _dot(u.astype(BF16), top_ref[0]) + d_ref[0] * u
    for s_ref, m_ref in ((pfr, m0), (pfi, m1), (nbr, m2), (nbi, m3)):
        y = y + lax.dot_general(s_ref[...].astype(BF16), m_ref[0], NT_DIMS, preferred_element_type=F32)
    o_ref[0] = jax.nn.gelu(y)


def _s5_out(u_rows, top, d_rows, states, carries, rt):
    n_rows = u_rows.shape[1]
    s_spec = pl.BlockSpec((rt, S5_PSTATE), lambda n, r: (r, n))
    m_spec = pl.BlockSpec((1, S5_ROW, S5_PSTATE), lambda n, r: (n, 0, 0))
    return pl.pallas_call(
        _s5_out_kernel,
        grid=(S5_NPAIR, n_rows // rt),
        in_specs=[pl.BlockSpec((1, rt, S5_ROW), lambda n, r: (n, r, 0)),
                  pl.BlockSpec((1, S5_ROW, S5_ROW), lambda n, r: (n, 0, 0)),
                  pl.BlockSpec((1, 1, S5_ROW), lambda n, r: (n, 0, 0))] + [s_spec] * 4 + [m_spec] * 4,
        out_specs=pl.BlockSpec((1, rt, S5_ROW), lambda n, r: (n, r, 0)),
        out_shape=jax.ShapeDtypeStruct(u_rows.shape, F32),
        compiler_params=_cparams(2),
        name="s5_outputs",
    )(u_rows, top, d_rows, *states, *carries)


def _s5_branch(u_rows, ops, d_skip, s0, nb, seq_len):
    top, ends, carries, decay = ops
    nc = seq_len // S5_CHUNK
    n_rows = nc * nb
    rt = min(n_rows, 512)
    d_rows = jnp.tile(d_skip.reshape(S5_NPAIR, 1, S5_PAIR), (1, S5_CHUNK, 1)).reshape(S5_NPAIR, 1, S5_ROW)
    se = _s5_ends(u_rows, ends, rt)
    scanned = _s5_scan(se, decay, s0, nc, nb)
    states = [s.reshape(n_rows, N_STATE) for s in scanned[:4]]
    y_rows = _s5_out(u_rows, top, d_rows, states, carries, rt)
    return y_rows, [s.reshape(nb, N_STATE) for s in scanned[4:]]


Q_SLOT_ORDER = tuple(h for j in range(GRP) for h in (j, GRP + j))


def _stack_group_queries(q_ref, h, tq):
    lane = lax.broadcasted_iota(jnp.int32, (tq, LANES), 1)
    keep = (lane >= h * HEAD_DIM) & (lane < (h + 1) * HEAD_DIM)
    zero = jnp.zeros((tq, LANES), BF16)
    return jnp.concatenate([jnp.where(keep, q_ref[:, j * LANES:(j + 1) * LANES], zero) for j in range(GRP)], axis=0)


def _sink_lanes(sink_ref, h, tq):
    return jnp.concatenate([jnp.full((1, tq), sink_ref[h * GRP + j] * LOG2E, F32) for j in range(GRP)], axis=1)


def _values_with_ones(vt):
    return jnp.concatenate([vt, jnp.ones((V_ROWS - HEAD_DIM, vt.shape[1]), BF16)], axis=0)


def _store_heads(o_ref, h, o_t, tq):
    for j in range(GRP):
        head = h * GRP + j
        o_ref[0, head * HEAD_DIM:(head + 1) * HEAD_DIM, :] = o_t[:, j * tq:(j + 1) * tq].astype(o_ref.dtype)


def _attn_full_kernel(*refs, tq, use_sink):
    if use_sink:
        sink_ref, q_ref, k_ref, vt_ref, o_ref, m_ref, acc_ref, sa_ref, sb_ref = refs
    else:
        q_ref, k_ref, vt_ref, o_ref, m_ref, acc_ref, sa_ref, sb_ref = refs
    n_chunks = k_ref.shape[1]
    assert n_chunks % 2 == 1
    rows = GRP * tq
    qs = [_stack_group_queries(q_ref, h, tq) for h in range(N_KV)]
    for h in range(N_KV):
        if use_sink:
            m_ref[h] = _sink_lanes(sink_ref, h, tq)
            acc_ref[h] = jnp.concatenate([jnp.zeros((HEAD_DIM, rows), F32), jnp.ones((V_ROWS - HEAD_DIM, rows), F32)], axis=0)
        else:
            m_ref[h] = jnp.full((1, rows), NEG_INF, F32)
            acc_ref[h] = jnp.zeros((V_ROWS, rows), F32)

    def scores(c, s_ref):
        kc = k_ref[0, c]
        for h in range(N_KV):
            s_ref[h] = lax.dot_general(kc, qs[h], NT_DIMS, preferred_element_type=F32)

    def consume(c, s_ref):
        for h in range(N_KV):
            s = s_ref[h]
            m_old = m_ref[h]
            m_new = jnp.maximum(m_old, jnp.max(s, axis=0, keepdims=True))
            p = jnp.exp2(s - m_new).astype(BF16)
            alpha = jnp.exp2(m_old - m_new)
            lhs = _values_with_ones(vt_ref[0, c, h * HEAD_DIM:(h + 1) * HEAD_DIM, :])
            acc_ref[h] = alpha * acc_ref[h] + _dot(lhs, p)
            m_ref[h] = m_new

    scores(0, sa_ref)

    def body(k, carry):
        c = 2 * k
        scores(c + 1, sb_ref)
        consume(c, sa_ref)
        scores(c + 2, sa_ref)
        consume(c + 1, sb_ref)
        return carry

    lax.fori_loop(0, n_chunks // 2, body, 0, unroll=True)
    consume(n_chunks - 1, sa_ref)
    for h in range(N_KV):
        acc = acc_ref[h]
        _store_heads(o_ref, h, acc[:HEAD_DIM] / acc[HEAD_DIM:HEAD_DIM + 1], tq)


def _attn_band_kernel(sink_ref, q_ref, k_ref, vt_ref, o_ref, *, tq, n_lat, n_ctx):
    i = pl.program_id(1)
    n_band = tq // WINDOW + 2
    lat_chunks, ctx_chunks = n_lat // WINDOW, n_ctx // WINDOW
    c0 = jnp.clip(i * (tq // WINDOW) - 1, 0, lat_chunks - n_band)
    rows = GRP * tq
    row = lax.broadcasted_iota(jnp.int32, (n_band * WINDOW, rows), 0)
    col = lax.broadcasted_iota(jnp.int32, (n_band * WINDOW, rows), 1)
    in_band = jnp.abs((c0 * WINDOW + row) - (i * tq + (col & (tq - 1)))) <= WINDOW
    k_band = k_ref[0, pl.ds(c0, n_band)].reshape(n_band * WINDOW, LANES)
    k_ctx = k_ref[0, lat_chunks:lat_chunks + ctx_chunks].reshape(n_ctx, LANES)
    qs = [_stack_group_queries(q_ref, h, tq) for h in range(N_KV)]
    s_band = [jnp.where(in_band, lax.dot_general(k_band, qs[h], NT_DIMS, preferred_element_type=F32), NEG_INF)
              for h in range(N_KV)]
    s_ctx = [lax.dot_general(k_ctx, qs[h], NT_DIMS, preferred_element_type=F32) for h in range(N_KV)]
    for h in range(N_KV):
        sink = _sink_lanes(sink_ref, h, tq)
        m = jnp.maximum(jnp.maximum(jnp.max(s_band[h], axis=0, keepdims=True),
                                    jnp.max(s_ctx[h], axis=0, keepdims=True)), sink)
        p_band = jnp.exp2(s_band[h] - m).astype(BF16)
        p_ctx = jnp.exp2(s_ctx[h] - m).astype(BF16)
        hd = slice(h * HEAD_DIM, (h + 1) * HEAD_DIM)
        acc = jnp.zeros((V_ROWS, rows), F32)
        for t in range(n_band):
            acc = acc + _dot(_values_with_ones(vt_ref[0, c0 + t, hd, :]), p_band[t * WINDOW:(t + 1) * WINDOW])
        for t in range(ctx_chunks):
            acc = acc + _dot(_values_with_ones(vt_ref[0, lat_chunks + t, hd, :]), p_ctx[t * WINDOW:(t + 1) * WINDOW])
        den = acc[HEAD_DIM:HEAD_DIM + 1] + jnp.exp2(sink - m)
        _store_heads(o_ref, h, acc[:HEAD_DIM] / den, tq)


def _attention(q, k, v, sink, nb, seq_len, tq, tk, band_ctx=None):
    n_k = k.shape[1]
    kc = k.reshape(nb, n_k // tk, tk, D_KV)
    vt = v.reshape(nb, n_k // tk, tk, D_KV).transpose(0, 1, 3, 2)
    tiles = seq_len // tq
    q_spec = pl.BlockSpec((tq, D_ATT), lambda b, i: (b * tiles + i, 0))
    k_spec = pl.BlockSpec((1, n_k // tk, tk, D_KV), lambda b, i: (b, 0, 0, 0))
    vt_spec = pl.BlockSpec((1, n_k // tk, D_KV, tk), lambda b, i: (b, 0, 0, 0))
    in_specs, args = [q_spec, k_spec, vt_spec], [q, kc, vt]
    if sink is not None:
        in_specs, args = [pl.BlockSpec(memory_space=pltpu.SMEM)] + in_specs, [sink] + args
    rows = GRP * tq
    if band_ctx is None:
        body = functools.partial(_attn_full_kernel, tq=tq, use_sink=sink is not None)
        scratch = [pltpu.VMEM((N_KV, 1, rows), F32), pltpu.VMEM((N_KV, V_ROWS, rows), F32),
                   pltpu.VMEM((N_KV, tk, rows), F32), pltpu.VMEM((N_KV, tk, rows), F32)]
        name = "attention_full"
    else:
        body = functools.partial(_attn_band_kernel, tq=tq, n_lat=seq_len, n_ctx=band_ctx)
        scratch = []
        name = "attention_band"
    o_t = pl.pallas_call(
        body,
        grid=(nb, tiles),
        in_specs=in_specs,
        out_specs=pl.BlockSpec((1, D_ATT, tq), lambda b, i: (b, 0, i)),
        out_shape=jax.ShapeDtypeStruct((nb, D_ATT, seq_len), BF16),
        scratch_shapes=scratch,
        compiler_params=_cparams(2),
        name=name,
    )(*args)
    return o_t.transpose(0, 2, 1).reshape(nb * seq_len, D_ATT)


def _layer_norm(z, g, b):
    mu = jnp.mean(z, axis=-1, keepdims=True)
    zc = z - mu
    var = jnp.mean(zc * zc, axis=-1, keepdims=True)
    return zc * lax.rsqrt(var + LN_EPS) * g + b


def _merge_mlp_kernel(x_ref, g1_ref, sh2_ref, sc2_ref, g2_ref, ya_ref, yw_ref, yg_ref, gate_ref,
                      wglu_ref, wa_ref, ww_ref, wg_ref, wout_ref, ln1g_ref, ln1b_ref,
                      wup_ref, wdn_ref, ln2g_ref, ln2b_ref, o_ref, ya_scr):
    ya = _chunk_rows_to_tokens(ya_ref, ya_scr)
    ya = ya * jax.nn.sigmoid(_dot(ya.astype(BF16), wglu_ref[...]))

    def gate(c):
        return gate_ref[:, c * D_MODEL:(c + 1) * D_MODEL].astype(F32)

    m = (gate(0) * _dot(ya.astype(BF16), wa_ref[...])
         + gate(1) * _dot(yw_ref[...], ww_ref[...])
         + gate(2) * _dot(yg_ref[...], wg_ref[...]))
    f = _dot(m.astype(BF16), wout_ref[...])
    x1 = _layer_norm(DEEPNORM_ALPHA * x_ref[...] + g1_ref[0] * f, ln1g_ref[...], ln1b_ref[...])

    h = (x1 * (1.0 + sc2_ref[0]) + sh2_ref[0]).astype(BF16)
    acc = jnp.zeros(x1.shape, F32)
    for c in range(D_FF // FF_TILE):
        up = jnp.maximum(_dot(h, wup_ref[:, c * FF_TILE:(c + 1) * FF_TILE]), 0.0)
        acc = acc + _dot((up * up).astype(BF16), wdn_ref[c * FF_TILE:(c + 1) * FF_TILE, :])
    o_ref[...] = _layer_norm(DEEPNORM_ALPHA * x1 + g2_ref[0] * acc, ln2g_ref[...], ln2b_ref[...])


def _merge_mlp(x, mod_l, row_of_tile, ya_rows, yw, yg, gates, lw):
    n_tok = x.shape[0]
    tm = ROW_TILE
    row = lambda w: pl.BlockSpec((tm, w), lambda i: (i, 0))
    vec = _resident((1, D_MODEL))
    return pl.pallas_call(
        _merge_mlp_kernel,
        grid=(n_tok // tm,),
        in_specs=[row(D_MODEL)] + [_mod_spec(c, row_of_tile) for c in (2, 3, 4, 5)]
                 + [pl.BlockSpec((S5_NPAIR, CHUNKS_PER_TILE, S5_ROW), lambda i: (0, i, 0)),
                    row(D_ATT), row(D_ATT), row(3 * D_MODEL),
                    _resident((D_SSM, D_SSM)), _resident((D_SSM, D_MODEL)), _resident((D_ATT, D_MODEL)),
                    _resident((D_ATT, D_MODEL)), _resident((D_MODEL, D_MODEL)), vec, vec,
                    _resident((D_MODEL, D_FF)), _resident((D_FF, D_MODEL)), vec, vec],
        out_specs=row(D_MODEL),
        out_shape=jax.ShapeDtypeStruct((n_tok, D_MODEL), F32),
        scratch_shapes=[pltpu.VMEM((LANE_SLABS, tm, LANES), F32)],
        compiler_params=_cparams(1),
        name="merge_mlp_residual",
    )(x, mod_l, mod_l, mod_l, mod_l, ya_rows, yw, yg, gates,
      lw['w_glu'], lw['w_br_ssm'], lw['w_br_win'], lw['w_br_glb'], lw['w_out'], lw['ln1_g'], lw['ln1_b'],
      lw['w_up'], lw['w_down'], lw['ln2_g'], lw['ln2_b'])


def _rope_tables(n_tok):
    rows = n_tok // GRID_W
    row = jnp.repeat(jnp.arange(rows, dtype=F32), GRID_W)
    col = jnp.tile(jnp.arange(GRID_W, dtype=F32), rows)
    n_freq = HEAD_DIM // 4
    inv = ROPE_BASE ** (-jnp.arange(n_freq, dtype=F32) / n_freq)
    ang = jnp.concatenate([row[:, None] * inv, col[:, None] * inv], axis=-1)
    cos, sin = jnp.cos(ang), jnp.sin(ang)
    cos_t = jnp.tile(jnp.concatenate([cos, cos], axis=-1), (1, LANES // HEAD_DIM))
    sin_t = jnp.tile(jnp.concatenate([-sin, sin], axis=-1), (1, LANES // HEAD_DIM))
    return cos_t, sin_t


def _block_diag_ones(n):
    idx = np.arange(n) // HEAD_DIM
    return jnp.asarray(idx[:, None] == idx[None, :], dtype=BF16)


def _permute_q_columns(w):
    def perm(block):
        return block.reshape(D_MODEL, N_HEADS, HEAD_DIM)[:, np.array(Q_SLOT_ORDER), :].reshape(D_MODEL, D_ATT)
    return jnp.concatenate([w[:, :O_QW], perm(w[:, O_QW:O_KW]), w[:, O_KW:O_QG], perm(w[:, O_QG:O_KG]), w[:, O_KG:]], axis=1)


def _layer(x, lw, mod_l, row_of_tile, nb, seq_len, ctx, rope_tabs):
    u_rows, qw, qg, gates, *kv = _in_projection(
        x, mod_l, row_of_tile, lw['w_in'], lw['qn'], lw['kn'], lw['bdq'], lw['bdk'], rope_tabs, seq_len,
        keep_f32_kv=ctx is None)
    if ctx is None:
        s0 = [jnp.zeros((nb, N_STATE), F32)] * 4
    else:
        s0 = [ctx[0][:, d, part].reshape(nb, N_STATE) for d in (0, 1) for part in (0, 1)]
    ya_rows, s_fin = _s5_branch(u_rows, lw['s5_ops'], lw['d_skip'], s0, nb, seq_len)

    def seq(t, n):
        return t.reshape(nb, n, D_KV)

    kw, vw, kg, vg = [seq(t, seq_len) for t in kv[:4]]
    if ctx is None:
        yw = _attention(qw, kw, vw, lw['sink'], nb, seq_len, seq_len, seq_len)
        yg = _attention(qg, kg, vg, None, nb, seq_len, seq_len, seq_len)
        new_ctx = (s_fin,) + tuple(kv[4:])
    else:
        _, k_wc, v_wc, k_gc, v_gc = ctx
        n_ctx = k_wc.shape[1]

        def with_ctx(new, cached):
            return jnp.concatenate([new, seq(cached, n_ctx).astype(BF16)], axis=1)

        yw = _attention(qw, with_ctx(kw, k_wc), with_ctx(vw, v_wc), lw['sink'], nb, seq_len, 256, WINDOW, band_ctx=n_ctx)
        yg = _attention(qg, with_ctx(kg, k_gc), with_ctx(vg, v_gc), None, nb, seq_len, 256, 256)
        new_ctx = None
    x2 = _merge_mlp(x, mod_l, row_of_tile, ya_rows, yw, yg, gates, lw)
    return x2, new_ctx


def kernel(x_prompt, x_sample, state_ssm, cache_k_win, cache_v_win, cache_k_glb, cache_v_glb, c, c_ctx, w_mod, b_mod, w_in, ssm_lam_re, ssm_lam_im, ssm_log_step, ssm_b_re, ssm_b_im, ssm_c_re, ssm_c_im, ssm_d, w_glu, sink_win, q_norm_glb, k_norm_glb, w_br_ssm, w_br_win, w_br_glb, w_out, ln1_g, ln1_b, w_up, w_down, ln2_g, ln2_b):
    n_ctx_b, ctx_len, _ = x_prompt.shape
    n_lat_b, lat_len, _ = x_sample.shape
    assert ctx_len % ROW_TILE == 0 or ROW_TILE % ctx_len == 0
    assert lat_len % ROW_TILE == 0 and (n_ctx_b * ctx_len) % ROW_TILE == 0

    cond8 = jnp.zeros((8, D_MODEL), F32).at[0].set(c_ctx).at[1:1 + n_lat_b].set(c)
    mod = _modulation(cond8, w_mod, b_mod).reshape(DEPTH, 8, 1, 6 * D_MODEL)
    rope_tabs = _rope_tables(lat_len)
    bdq, bdk = _block_diag_ones(D_ATT), _block_diag_ones(D_KV)
    lat_tiles = lat_len // ROW_TILE
    ctx_row = lambda i: 0
    lat_row = lambda i: 1 + i // lat_tiles

    xp = x_prompt.reshape(n_ctx_b * ctx_len, D_MODEL)
    xs = x_sample.reshape(n_lat_b * lat_len, D_MODEL)
    new_ssm, new_kw, new_vw, new_kg, new_vg = [], [], [], [], []
    for l in range(DEPTH):
        lw = dict(
            w_in=_permute_q_columns(w_in[l]).astype(BF16),
            qn=jnp.tile(q_norm_glb[l], N_HEADS).reshape(1, D_ATT), kn=jnp.tile(k_norm_glb[l], N_KV).reshape(1, D_KV),
            bdq=bdq, bdk=bdk,
            s5_ops=_s5_operators(ssm_lam_re[l], ssm_lam_im[l], ssm_log_step[l],
                                 ssm_b_re[l], ssm_b_im[l], ssm_c_re[l], ssm_c_im[l]),
            d_skip=ssm_d[l],
            sink=sink_win[l],
            w_glu=w_glu[l].astype(BF16), w_br_ssm=w_br_ssm[l].astype(BF16), w_br_win=w_br_win[l].astype(BF16),
            w_br_glb=w_br_glb[l].astype(BF16), w_out=w_out[l].astype(BF16),
            ln1_g=ln1_g[l].reshape(1, D_MODEL), ln1_b=ln1_b[l].reshape(1, D_MODEL),
            w_up=w_up[l].astype(BF16), w_down=w_down[l].astype(BF16),
            ln2_g=ln2_g[l].reshape(1, D_MODEL), ln2_b=ln2_b[l].reshape(1, D_MODEL),
        )
        xp, (s_fin, kw, vw, kg, vg) = _layer(xp, lw, mod[l], ctx_row, n_ctx_b, ctx_len, None, None)
        new_ssm.append(jnp.stack(s_fin, axis=1).reshape(n_ctx_b, 2, 2, SSM_GROUPS, SSM_STATE))
        for acc, t in ((new_kw, kw), (new_vw, vw), (new_kg, kg), (new_vg, vg)):
            acc.append(t.reshape(n_ctx_b, ctx_len, N_KV, HEAD_DIM))
        ctx = (state_ssm[:, l], cache_k_win[:, l], cache_v_win[:, l], cache_k_glb[:, l], cache_v_glb[:, l])
        xs, _ = _layer(xs, lw, mod[l], lat_row, n_lat_b, lat_len, ctx, rope_tabs)
    return (xp.reshape(x_prompt.shape), xs.reshape(x_sample.shape),
            jnp.stack(new_ssm, axis=1), jnp.stack(new_kw, axis=1), jnp.stack(new_vw, axis=1),
            jnp.stack(new_kg, axis=1), jnp.stack(new_vg, axis=1))
```

```python
import functools

import jax
import jax.numpy as jnp
import numpy as np
from jax import lax
from jax.experimental import pallas as pl
from jax.experimental.pallas import tpu as pltpu

F32 = jnp.float32
BF16 = jnp.bfloat16

D_MODEL = 1024
DEPTH = 2
GRID_W = 64
HEAD_DIM = 64
D_SSM = 512
SSM_GROUP_CH = 16
SSM_GROUPS = 32
SSM_STATE = 64
N_HEADS = 8
N_KV = 2
GRP = N_HEADS // N_KV
D_ATT = N_HEADS * HEAD_DIM
D_KV = N_KV * HEAD_DIM
WINDOW = 128
ROPE_BASE = 10000.0
D_FF = 4 * D_MODEL
LN_EPS = 1e-5
RMS_EPS = 1e-6
ATTN_SCALE = HEAD_DIM ** -0.5
DEEPNORM_ALPHA = (2.0 * DEPTH) ** 0.25
NEG_INF = -1e30
LOG2E = 1.4426950408889634
Q_SCALE = ATTN_SCALE * LOG2E
V_ROWS = HEAD_DIM + 16
N_IN = D_SSM + 2 * (D_ATT + 2 * D_KV) + 3 * D_MODEL
O_U = 0
O_QW = O_U + D_SSM
O_KW = O_QW + D_ATT
O_VW = O_KW + D_KV
O_QG = O_VW + D_KV
O_KG = O_QG + D_ATT
O_VG = O_KG + D_KV
O_GATE = O_VG + D_KV

S5_CHUNK = 16
S5_PAIR = 2 * SSM_GROUP_CH
S5_NPAIR = SSM_GROUPS // 2
S5_ROW = S5_CHUNK * S5_PAIR
S5_PSTATE = 2 * SSM_STATE
N_STATE = SSM_GROUPS * SSM_STATE

LANES = 128
ROW_TILE = 512
FF_TILE = 1024
CHUNKS_PER_TILE = ROW_TILE // S5_CHUNK
LANE_SLABS = D_SSM // LANES
PAIRS_PER_SLAB = LANES // S5_PAIR
VMEM_LIMIT = 56 * 1024 * 1024
NT_DIMS = (((1,), (1,)), ((), ()))
TN_DIMS = (((0,), (0,)), ((), ()))


def _cparams(n_axes):
    return pltpu.CompilerParams(dimension_semantics=("arbitrary",) * n_axes, vmem_limit_bytes=VMEM_LIMIT)


def _resident(shape):
    nd = len(shape)
    return pl.BlockSpec(shape, lambda *_: (0,) * nd, pipeline_mode=pl.Buffered(1))


def _dot(a, b):
    return jnp.dot(a, b, preferred_element_type=F32)


def _mod_kernel(c_ref, w_ref, b_ref, o_ref):
    c = c_ref[...]
    a = (c * jax.nn.sigmoid(c)).astype(BF16)
    o_ref[0] = _dot(a, w_ref[0].astype(BF16)) + b_ref[0]


def _modulation(cond8, w_mod, b_mod):
    tn = 512
    n_out = w_mod.shape[-1]
    return pl.pallas_call(
        _mod_kernel,
        grid=(DEPTH, n_out // tn),
        in_specs=[
            pl.BlockSpec((8, D_MODEL), lambda l, n: (0, 0)),
            pl.BlockSpec((1, D_MODEL, tn), lambda l, n: (l, 0, n)),
            pl.BlockSpec((1, 1, tn), lambda l, n: (l, 0, n)),
        ],
        out_specs=pl.BlockSpec((1, 8, tn), lambda l, n: (l, 0, n)),
        out_shape=jax.ShapeDtypeStruct((DEPTH, 8, n_out), F32),
        compiler_params=_cparams(2),
        name="modulation",
    )(cond8, w_mod, b_mod.reshape(DEPTH, 1, n_out))


def _mod_spec(chunk, row_of_tile):
    return pl.BlockSpec((1, 1, D_MODEL), lambda i: (row_of_tile(i), 0, chunk))


def _tokens_to_chunk_rows(u, scr, out_ref):
    n_chunks = u.shape[0] // S5_CHUNK
    for j in range(LANE_SLABS):
        scr[j] = u[:, j * LANES:(j + 1) * LANES]
    pieces = [[None] * S5_CHUNK for _ in range(S5_NPAIR)]
    for j in range(LANE_SLABS):
        for t in range(S5_CHUNK):
            step_rows = scr[j, pl.ds(t, n_chunks, stride=S5_CHUNK), :]
            for m in range(PAIRS_PER_SLAB):
                pieces[j * PAIRS_PER_SLAB + m][t] = step_rows[:, m * S5_PAIR:(m + 1) * S5_PAIR]
    for n in range(S5_NPAIR):
        out_ref[n] = jnp.concatenate(pieces[n], axis=1)


def _chunk_rows_to_tokens(y_ref, scr):
    n_chunks = y_ref.shape[1]
    for j in range(LANE_SLABS):
        for t in range(S5_CHUNK):
            piece = jnp.concatenate(
                [y_ref[j * PAIRS_PER_SLAB + m, :, t * S5_PAIR:(t + 1) * S5_PAIR] for m in range(PAIRS_PER_SLAB)], axis=1)
            scr[j, pl.ds(t, n_chunks, stride=S5_CHUNK), :] = piece
    return jnp.concatenate([scr[j] for j in range(LANE_SLABS)], axis=1)


def _head_rms(x, gain, ones_bd):
    ss = _dot((x * x).astype(BF16), ones_bd)
    return x * lax.rsqrt(ss * (1.0 / HEAD_DIM) + RMS_EPS) * gain


def _rope_chunk(xc, cos_t, sin_t):
    lane = lax.broadcasted_iota(jnp.int32, xc.shape, 1)
    first_half = (lane & (HEAD_DIM - 1)) < (HEAD_DIM // 2)
    partner = jnp.where(first_half, pltpu.roll(xc, LANES - HEAD_DIM // 2, 1), pltpu.roll(xc, HEAD_DIM // 2, 1))
    return xc * cos_t + partner * sin_t


def _inproj_kernel(*refs, rope, vw_transposed, keep_f32_kv):
    n_in = 10 if rope else 8
    x_ref, sh_ref, sc_ref, w_ref, qn_ref, kn_ref, bdq_ref, bdk_ref = refs[:8]
    u_ref, qw_ref, qg_ref, gate_ref, kw_ref, kg_ref, vgt_ref, vw_ref = refs[n_in:n_in + 8]
    f32_refs, u_scr = refs[n_in + 8:-1], refs[-1]
    h = (x_ref[...] * (1.0 + sc_ref[0]) + sh_ref[0]).astype(BF16)

    def proj(lo, width):
        return _dot(h, w_ref[:, lo:lo + width])

    def rotate(x):
        if not rope:
            return x
        chunks = [_rope_chunk(x[:, c * LANES:(c + 1) * LANES], refs[8][...], refs[9][...]) for c in range(x.shape[-1] // LANES)]
        return chunks[0] if len(chunks) == 1 else jnp.concatenate(chunks, axis=1)

    _tokens_to_chunk_rows(proj(O_U, D_SSM), u_scr, u_ref)
    qw_ref[...] = (rotate(proj(O_QW, D_ATT)) * Q_SCALE).astype(BF16)
    qg_ref[...] = (rotate(_head_rms(proj(O_QG, D_ATT), qn_ref[...], bdq_ref[...])) * Q_SCALE).astype(BF16)
    kw = rotate(proj(O_KW, D_KV))
    kg = rotate(_head_rms(proj(O_KG, D_KV), kn_ref[...], bdk_ref[...]))
    vw, vg = proj(O_VW, D_KV), proj(O_VG, D_KV)
    kw_ref[...] = kw.astype(BF16)
    kg_ref[...] = kg.astype(BF16)
    vgt_ref[...] = vg.T.astype(BF16)
    vw_ref[...] = (vw.T if vw_transposed else vw).astype(BF16)
    if keep_f32_kv:
        for out_ref, val in zip(f32_refs, (kw, vw, kg, vg)):
            out_ref[...] = val
    for c in range(3):
        gate = jax.nn.sigmoid(proj(O_GATE + c * D_MODEL, D_MODEL))
        gate_ref[:, c * D_MODEL:(c + 1) * D_MODEL] = gate.astype(gate_ref.dtype)


def _in_projection(x, mod_l, row_of_tile, w_in, qn, kn, bdq, bdk, rope_tabs, seq_len, vw_transposed, keep_f32_kv):
    n_tok = x.shape[0]
    tm = ROW_TILE
    rope = rope_tabs is not None
    row = lambda w: pl.BlockSpec((tm, w), lambda i: (i, 0))
    col = pl.BlockSpec((D_KV, tm), lambda i: (0, i))
    in_specs = [
        row(D_MODEL),
        _mod_spec(0, row_of_tile), _mod_spec(1, row_of_tile),
        _resident((D_MODEL, N_IN)),
        _resident((1, D_ATT)), _resident((1, D_KV)),
        _resident((D_ATT, D_ATT)), _resident((D_KV, D_KV)),
    ]
    args = [x, mod_l, mod_l, w_in, qn, kn, bdq, bdk]
    if rope:
        tiles_per_seq = seq_len // tm
        tab = pl.BlockSpec((tm, LANES), lambda i: (i % tiles_per_seq, 0))
        in_specs += [tab, tab]
        args += list(rope_tabs)
    tok = lambda w, dt: (jax.ShapeDtypeStruct((n_tok, w), dt), row(w))
    tr = (jax.ShapeDtypeStruct((D_KV, n_tok), BF16), col)
    outs = [(jax.ShapeDtypeStruct((S5_NPAIR, n_tok // S5_CHUNK, S5_ROW), F32),
             pl.BlockSpec((S5_NPAIR, CHUNKS_PER_TILE, S5_ROW), lambda i: (0, i, 0))),
            tok(D_ATT, BF16), tok(D_ATT, BF16), tok(3 * D_MODEL, BF16),
            tok(D_KV, BF16), tok(D_KV, BF16), tr, tr if vw_transposed else tok(D_KV, BF16)]
    if keep_f32_kv:
        outs += [tok(D_KV, F32)] * 4
    return pl.pallas_call(
        functools.partial(_inproj_kernel, rope=rope, vw_transposed=vw_transposed, keep_f32_kv=keep_f32_kv),
        grid=(n_tok // tm,),
        in_specs=in_specs,
        out_specs=[o[1] for o in outs],
        out_shape=[o[0] for o in outs],
        scratch_shapes=[pltpu.VMEM((LANE_SLABS, tm, LANES), F32)],
        compiler_params=_cparams(1),
        name="in_projection",
    )(*args)


def _zoh(lr, li, ls):
    dt = jnp.exp(ls)
    mag = jnp.exp(lr * dt)
    ar, ai = mag * jnp.cos(li * dt), mag * jnp.sin(li * dt)
    den = lr * lr + li * li
    fr = ((ar - 1.0) * lr + ai * li) / den
    fi = (ai * lr - (ar - 1.0) * li) / den
    return ar, ai, fr, fi


def _powers(ar, ai, n):
    out = [(jnp.ones_like(ar), jnp.zeros_like(ar))]
    for _ in range(n):
        pr, pi = out[-1]
        out.append((pr * ar - pi * ai, pr * ai + pi * ar))
    return out


def _s5_operator_kernel(lr_ref, li_ref, ls_ref, btr_ref, bti_ref, cr_ref, ci_ref,
                        top_ref, e0, e1, e2, e3, m0, m1, m2, m3, d0, d1, d2, d3, ext_scr):
    tc, cg, p = S5_CHUNK, SSM_GROUP_CH, SSM_STATE
    ends = ((e0, e1), (e2, e3))
    carries = ((m0, m1), (m2, m3))
    decays = ((d0, d1), (d2, d3))
    ext_scr[...] = jnp.zeros(ext_scr.shape, F32)
    zero_half = jnp.zeros((cg, p), F32)

    def place(out_ref, t, g2, val):
        halves = [val, zero_half] if g2 == 0 else [zero_half, val]
        out_ref[0, t * S5_PAIR + g2 * cg:t * S5_PAIR + (g2 + 1) * cg, :] = jnp.concatenate(halves, axis=1).astype(BF16)

    for d in range(2):
        for g2 in range(2):
            ar, ai, fr, fi = _zoh(lr_ref[d, 0, g2], li_ref[d, 0, g2], ls_ref[d, 0, g2])
            btr, bti = btr_ref[d, 0, g2], bti_ref[d, 0, g2]
            bbr, bbi = fr * btr - fi * bti, fr * bti + fi * btr
            cr, ci = cr_ref[d, 0, g2], ci_ref[d, 0, g2]
            pw = _powers(ar, ai, tc)
            group_rows = slice(g2 * cg, (g2 + 1) * cg)
            for j in range(tc + 1):
                pr, pi = pw[j]
                car, cai = cr * pr - ci * pi, cr * pi + ci * pr
                if j >= 1:
                    t = j - 1 if d == 0 else tc - j
                    place(carries[d][0], t, g2, car)
                    place(carries[d][1], t, g2, -cai)
                if j == tc:
                    break
                t = tc - 1 - j if d == 0 else j
                place(ends[d][0], t, g2, pr * bbr - pi * bbi)
                place(ends[d][1], t, g2, pr * bbi + pi * bbr)
                kt = (lax.dot_general(bbr, car, NT_DIMS, precision=lax.Precision.HIGHEST, preferred_element_type=F32)
                      - lax.dot_general(bbi, cai, NT_DIMS, precision=lax.Precision.HIGHEST, preferred_element_type=F32))
                slot = tc - 1 + j if d == 0 else tc - 1 - j
                lanes = slice(slot * S5_PAIR + g2 * cg, slot * S5_PAIR + (g2 + 1) * cg)
                if d == 1 and j == 0:
                    ext_scr[group_rows, lanes] = ext_scr[group_rows, lanes] + kt
                else:
                    ext_scr[group_rows, lanes] = kt
            for part in range(2):
                decays[d][part][:, g2 * p:(g2 + 1) * p] = pw[tc][part]
    for t in range(tc):
        for g2 in range(2):
            rows = slice(t * S5_PAIR + g2 * cg, t * S5_PAIR + (g2 + 1) * cg)
            window = slice((tc - 1 - t) * S5_PAIR, (tc - 1 - t) * S5_PAIR + S5_ROW)
            top_ref[0, rows, :] = ext_scr[g2 * cg:(g2 + 1) * cg, window].astype(BF16)


def _s5_operators(lam_re, lam_im, log_step, b_re, b_im, c_re, c_im):
    g, p, cg = SSM_GROUPS, SSM_STATE, SSM_GROUP_CH
    ls = jnp.broadcast_to(log_step[:, :, None], (2, g, p))

    def rows(v):
        return v.reshape(2, S5_NPAIR, 2, 1, p)

    def mat(v):
        return v.reshape(2, S5_NPAIR, 2, cg, p)

    def spec(*tail):
        return pl.BlockSpec((2, 1, 2) + tail, lambda n: (0, n, 0, 0, 0))

    state_op = pl.BlockSpec((1, S5_ROW, S5_PSTATE), lambda n: (n, 0, 0))
    out = pl.pallas_call(
        _s5_operator_kernel,
        grid=(S5_NPAIR,),
        in_specs=[spec(1, p)] * 3 + [spec(cg, p)] * 4,
        out_specs=([pl.BlockSpec((1, S5_ROW, S5_ROW), lambda n: (n, 0, 0))] + [state_op] * 8
                   + [pl.BlockSpec((1, S5_PSTATE), lambda n: (0, n))] * 4),
        out_shape=([jax.ShapeDtypeStruct((S5_NPAIR, S5_ROW, S5_ROW), BF16)]
                   + [jax.ShapeDtypeStruct((S5_NPAIR, S5_ROW, S5_PSTATE), BF16)] * 8
                   + [jax.ShapeDtypeStruct((1, N_STATE), F32)] * 4),
        scratch_shapes=[pltpu.VMEM((S5_PAIR, 2 * S5_ROW), F32)],
        compiler_params=_cparams(1),
        name="s5_operators",
    )(rows(lam_re), rows(lam_im), rows(ls),
      mat(jnp.swapaxes(b_re, -1, -2)), mat(jnp.swapaxes(b_im, -1, -2)), mat(c_re), mat(c_im))
    return out[0], out[1:5], out[5:9], out[9:13]


def _s5_ends_kernel(u_ref, e0, e1, e2, e3, o0, o1, o2, o3):
    u = u_ref[0].astype(BF16)
    for e_ref, o_ref in ((e0, o0), (e1, o1), (e2, o2), (e3, o3)):
        o_ref[...] = _dot(u, e_ref[0])


def _s5_ends(u_rows, ends, rt):
    n_rows = u_rows.shape[1]
    e_spec = pl.BlockSpec((1, S5_ROW, S5_PSTATE), lambda n, r: (n, 0, 0))
    o_spec = pl.BlockSpec((rt, S5_PSTATE), lambda n, r: (r, n))
    return pl.pallas_call(
        _s5_ends_kernel,
        grid=(S5_NPAIR, n_rows // rt),
        in_specs=[pl.BlockSpec((1, rt, S5_ROW), lambda n, r: (n, r, 0))] + [e_spec] * 4,
        out_specs=[o_spec] * 4,
        out_shape=[jax.ShapeDtypeStruct((n_rows, N_STATE), F32)] * 4,
        compiler_params=_cparams(2),
        name="s5_chunk_ends",
    )(u_rows, *ends)


def _s5_scan_kernel(efr, efi, ebr, ebi, afr, afi, abr, abi, sfr, sfi, sbr, sbi,
                    pfr, pfi, nbr, nbi, ofr, ofi, obr, obi):
    nc = efr.shape[1]
    a_fr, a_fi, a_br, a_bi = afr[...], afi[...], abr[...], abi[...]

    def body(c, carry):
        fr, fi, br, bi = carry
        rc, rb = pl.ds(c, 1), pl.ds(nc - 1 - c, 1)
        pfr[0, rc, :] = fr
        pfi[0, rc, :] = fi
        nbr[0, rb, :] = br
        nbi[0, rb, :] = bi
        nfr = a_fr * fr - a_fi * fi + efr[0, rc, :]
        nfi = a_fr * fi + a_fi * fr + efi[0, rc, :]
        nbr_ = a_br * br - a_bi * bi + ebr[0, rb, :]
        nbi_ = a_br * bi + a_bi * br + ebi[0, rb, :]
        return nfr, nfi, nbr_, nbi_

    fr, fi, br, bi = lax.fori_loop(0, nc, body, (sfr[0], sfi[0], sbr[0], sbi[0]))
    ofr[0] = fr
    ofi[0] = fi
    obr[0] = br
    obi[0] = bi


def _s5_scan(se, decay, s0, nc, nb):
    cw = 1024
    seq = pl.BlockSpec((1, nc, cw), lambda b, i: (b, 0, i))
    vec = pl.BlockSpec((1, cw), lambda b, i: (0, i))
    st = pl.BlockSpec((1, 1, cw), lambda b, i: (b, 0, i))
    return pl.pallas_call(
        _s5_scan_kernel,
        grid=(nb, N_STATE // cw),
        in_specs=[seq] * 4 + [vec] * 4 + [st] * 4,
        out_specs=[seq] * 4 + [st] * 4,
        out_shape=[jax.ShapeDtypeStruct((nb, nc, N_STATE), F32)] * 4 + [jax.ShapeDtypeStruct((nb, 1, N_STATE), F32)] * 4,
        compiler_params=_cparams(2),
        name="s5_chunk_scan",
    )(*[s.reshape(nb, nc, N_STATE) for s in se], *decay, *[s.reshape(nb, 1, N_STATE) for s in s0])


def _s5_out_kernel(u_ref, top_ref, d_ref, pfr, pfi, nbr, nbi, m0, m1, m2, m3, o_ref):
    u = u_ref[0]
    y = _dot(u.astype(BF16), top_ref[0]) + d_ref[0] * u
    for s_ref, m_ref in ((pfr, m0), (pfi, m1), (nbr, m2), (nbi, m3)):
        y = y + lax.dot_general(s_ref[...].astype(BF16), m_ref[0], NT_DIMS, preferred_element_type=F32)
    o_ref[0] = jax.nn.gelu(y)


def _s5_out(u_rows, top, d_rows, states, carries, rt):
    n_rows = u_rows.shape[1]
    s_spec = pl.BlockSpec((rt, S5_PSTATE), lambda n, r: (r, n))
    m_spec = pl.BlockSpec((1, S5_ROW, S5_PSTATE), lambda n, r: (n, 0, 0))
    return pl.pallas_call(
        _s5_out_kernel,
        grid=(S5_NPAIR, n_rows // rt),
        in_specs=[pl.BlockSpec((1, rt, S5_ROW), lambda n, r: (n, r, 0)),
                  pl.BlockSpec((1, S5_ROW, S5_ROW), lambda n, r: (n, 0, 0)),
                  pl.BlockSpec((1, 1, S5_ROW), lambda n, r: (n, 0, 0))] + [s_spec] * 4 + [m_spec] * 4,
        out_specs=pl.BlockSpec((1, rt, S5_ROW), lambda n, r: (n, r, 0)),
        out_shape=jax.ShapeDtypeStruct(u_rows.shape, F32),
        compiler_params=_cparams(2),
        name="s5_outputs",
    )(u_rows, top, d_rows, *states, *carries)


def _s5_branch(u_rows, ops, d_skip, s0, nb, seq_len):
    top, ends, carries, decay = ops
    nc = seq_len // S5_CHUNK
    n_rows = nc * nb
    rt = min(n_rows, 512)
    d_rows = jnp.tile(d_skip.reshape(S5_NPAIR, 1, S5_PAIR), (1, S5_CHUNK, 1)).reshape(S5_NPAIR, 1, S5_ROW)
    se = _s5_ends(u_rows, ends, rt)
    scanned = _s5_scan(se, decay, s0, nc, nb)
    states = [s.reshape(n_rows, N_STATE) for s in scanned[:4]]
    y_rows = _s5_out(u_rows, top, d_rows, states, carries, rt)
    return y_rows, [s.reshape(nb, N_STATE) for s in scanned[4:]]


Q_SLOT_ORDER = tuple(h for j in range(GRP) for h in (j, GRP + j))


def _stack_group_queries(q_ref, h, tq):
    lane = lax.broadcasted_iota(jnp.int32, (tq, LANES), 1)
    keep = (lane >= h * HEAD_DIM) & (lane < (h + 1) * HEAD_DIM)
    zero = jnp.zeros((tq, LANES), BF16)
    return jnp.concatenate([jnp.where(keep, q_ref[:, j * LANES:(j + 1) * LANES], zero) for j in range(GRP)], axis=0)


def _sink_lanes(sink_ref, h, tq):
    return jnp.concatenate([jnp.full((1, tq), sink_ref[h * GRP + j] * LOG2E, F32) for j in range(GRP)], axis=1)


def _values_with_ones(vt):
    return jnp.concatenate([vt, jnp.ones((V_ROWS - HEAD_DIM, vt.shape[1]), BF16)], axis=0)


def _store_heads(o_ref, h, o_t, tq):
    for j in range(GRP):
        head = h * GRP + j
        o_ref[0, head * HEAD_DIM:(head + 1) * HEAD_DIM, :] = o_t[:, j * tq:(j + 1) * tq].astype(o_ref.dtype)


def _transposed_out(nb, seq_len, tq):
    return (pl.BlockSpec((1, D_ATT, tq), lambda b, i: (b, 0, i)), jax.ShapeDtypeStruct((nb, D_ATT, seq_len), BF16))


def _tokens_major(o_t):
    nb, _, seq_len = o_t.shape
    return o_t.transpose(0, 2, 1).reshape(nb * seq_len, D_ATT)


def _attn_full_kernel(*refs, tq, tk, use_sink, has_cache):
    refs = list(refs)
    sink_ref = refs.pop(0) if use_sink else None
    q_ref, k_ref, vt_ref = refs[:3]
    kc_ref, vtc_ref = refs[3:5] if has_cache else (None, None)
    o_ref, m_ref, acc_ref, sa_ref, sb_ref = refs[-5:]
    n_own = k_ref.shape[0] // tk
    n_chunks = n_own + (kc_ref.shape[1] // tk if has_cache else 0)
    rows = GRP * tq
    qs = [_stack_group_queries(q_ref, h, tq) for h in range(N_KV)]
    for h in range(N_KV):
        if use_sink:
            m_ref[h] = _sink_lanes(sink_ref, h, tq)
            acc_ref[h] = jnp.concatenate([jnp.zeros((HEAD_DIM, rows), F32), jnp.ones((V_ROWS - HEAD_DIM, rows), F32)], axis=0)
        else:
            m_ref[h] = jnp.full((1, rows), NEG_INF, F32)
            acc_ref[h] = jnp.zeros((V_ROWS, rows), F32)

    def keys(c):
        if c < n_own:
            return k_ref[c * tk:(c + 1) * tk, :]
        return kc_ref[0, (c - n_own) * tk:(c - n_own + 1) * tk, :]

    def values_t(c, h):
        hd = slice(h * HEAD_DIM, (h + 1) * HEAD_DIM)
        if c < n_own:
            return vt_ref[hd, c * tk:(c + 1) * tk]
        return vtc_ref[0, hd, (c - n_own) * tk:(c - n_own + 1) * tk]

    def scores(c, s_ref):
        kc = keys(c)
        for h in range(N_KV):
            s_ref[h] = lax.dot_general(kc, qs[h], NT_DIMS, preferred_element_type=F32)

    def consume(c, s_ref):
        for h in range(N_KV):
            s = s_ref[h]
            m_old = m_ref[h]
            m_new = jnp.maximum(m_old, jnp.max(s, axis=0, keepdims=True))
            p = jnp.exp2(s - m_new).astype(BF16)
            alpha = jnp.exp2(m_old - m_new)
            acc_ref[h] = alpha * acc_ref[h] + _dot(_values_with_ones(values_t(c, h)), p)
            m_ref[h] = m_new

    bufs = (sa_ref, sb_ref)
    scores(0, bufs[0])
    for c in range(n_chunks):
        if c + 1 < n_chunks:
            scores(c + 1, bufs[(c + 1) % 2])
        consume(c, bufs[c % 2])
    for h in range(N_KV):
        acc = acc_ref[h]
        _store_heads(o_ref, h, acc[:HEAD_DIM] / acc[HEAD_DIM:HEAD_DIM + 1], tq)


def _attn_band_kernel(sink_ref, q_ref, k_ref, v_ref, kc_ref, vc_ref, o_ref, *, tq):
    i = pl.program_id(1)
    n_lat = k_ref.shape[0]
    span = tq + 2 * WINDOW
    start = pl.multiple_of(jnp.clip(i * tq - WINDOW, 0, n_lat - span), WINDOW)
    rows = GRP * tq
    row = lax.broadcasted_iota(jnp.int32, (span, rows), 0)
    col = lax.broadcasted_iota(jnp.int32, (span, rows), 1)
    in_band = jnp.abs((start + row) - (i * tq + (col & (tq - 1)))) <= WINDOW
    k_band, v_band = k_ref[pl.ds(start, span), :], v_ref[pl.ds(start, span), :]
    k_ctx, v_ctx = kc_ref[0], vc_ref[0]
    qs = [_stack_group_queries(q_ref, h, tq) for h in range(N_KV)]
    s_band = [jnp.where(in_band, lax.dot_general(k_band, qs[h], NT_DIMS, preferred_element_type=F32), NEG_INF)
              for h in range(N_KV)]
    s_ctx = [lax.dot_general(k_ctx, qs[h], NT_DIMS, preferred_element_type=F32) for h in range(N_KV)]
    for h in range(N_KV):
        sink = _sink_lanes(sink_ref, h, tq)
        m = jnp.maximum(jnp.maximum(jnp.max(s_band[h], axis=0, keepdims=True),
                                    jnp.max(s_ctx[h], axis=0, keepdims=True)), sink)
        p_band = jnp.exp2(s_band[h] - m)
        p_ctx = jnp.exp2(s_ctx[h] - m)
        den = jnp.sum(p_band, axis=0, keepdims=True) + jnp.sum(p_ctx, axis=0, keepdims=True) + jnp.exp2(sink - m)
        acc = (lax.dot_general(v_band, p_band.astype(BF16), TN_DIMS, preferred_element_type=F32)
               + lax.dot_general(v_ctx, p_ctx.astype(BF16), TN_DIMS, preferred_element_type=F32))
        _store_heads(o_ref, h, acc[h * HEAD_DIM:(h + 1) * HEAD_DIM] / den, tq)


def _attention_full(q, k, vt, sink, nb, seq_len, tq, tk, cache=None):
    tiles = seq_len // tq
    q_spec = pl.BlockSpec((tq, D_ATT), lambda b, i: (b * tiles + i, 0))
    in_specs = [q_spec, pl.BlockSpec((seq_len, D_KV), lambda b, i: (b, 0)), pl.BlockSpec((D_KV, seq_len), lambda b, i: (0, b))]
    args = [q, k, vt]
    if cache is not None:
        n_c = cache[0].shape[1]
        in_specs += [pl.BlockSpec((1, n_c, D_KV), lambda b, i: (b, 0, 0)), pl.BlockSpec((1, D_KV, n_c), lambda b, i: (b, 0, 0))]
        args += list(cache)
    if sink is not None:
        in_specs, args = [pl.BlockSpec(memory_space=pltpu.SMEM)] + in_specs, [sink] + args
    rows = GRP * tq
    out_spec, out_shape = _transposed_out(nb, seq_len, tq)
    return _tokens_major(pl.pallas_call(
        functools.partial(_attn_full_kernel, tq=tq, tk=tk, use_sink=sink is not None, has_cache=cache is not None),
        grid=(nb, tiles),
        in_specs=in_specs,
        out_specs=out_spec,
        out_shape=out_shape,
        scratch_shapes=[pltpu.VMEM((N_KV, 1, rows), F32), pltpu.VMEM((N_KV, V_ROWS, rows), F32),
                        pltpu.VMEM((N_KV, tk, rows), F32), pltpu.VMEM((N_KV, tk, rows), F32)],
        compiler_params=_cparams(2),
        name="attention_full",
    )(*args))


def _attention_band(q, k, v, sink, nb, seq_len, tq, cache):
    tiles = seq_len // tq
    n_c = cache[0].shape[1]
    q_spec = pl.BlockSpec((tq, D_ATT), lambda b, i: (b * tiles + i, 0))
    kv_spec = pl.BlockSpec((seq_len, D_KV), lambda b, i: (b, 0))
    c_spec = pl.BlockSpec((1, n_c, D_KV), lambda b, i: (b, 0, 0))
    out_spec, out_shape = _transposed_out(nb, seq_len, tq)
    return _tokens_major(pl.pallas_call(
        functools.partial(_attn_band_kernel, tq=tq),
        grid=(nb, tiles),
        in_specs=[pl.BlockSpec(memory_space=pltpu.SMEM), q_spec, kv_spec, kv_spec, c_spec, c_spec],
        out_specs=out_spec,
        out_shape=out_shape,
        compiler_params=_cparams(2),
        name="attention_band",
    )(sink, q, k, v, *cache))


def _layer_norm(z, g, b):
    mu = jnp.mean(z, axis=-1, keepdims=True)
    zc = z - mu
    var = jnp.mean(zc * zc, axis=-1, keepdims=True)
    return zc * lax.rsqrt(var + LN_EPS) * g + b


def _merge_mlp_kernel(x_ref, g1_ref, sh2_ref, sc2_ref, g2_ref, ya_ref, yw_ref, yg_ref, gate_ref,
                      wglu_ref, wa_ref, ww_ref, wg_ref, wout_ref, ln1g_ref, ln1b_ref,
                      wup_ref, wdn_ref, ln2g_ref, ln2b_ref, o_ref, ya_scr):
    ya = _chunk_rows_to_tokens(ya_ref, ya_scr)
    ya = ya * jax.nn.sigmoid(_dot(ya.astype(BF16), wglu_ref[...]))

    def gate(c):
        return gate_ref[:, c * D_MODEL:(c + 1) * D_MODEL].astype(F32)

    m = (gate(0) * _dot(ya.astype(BF16), wa_ref[...])
         + gate(1) * _dot(yw_ref[...], ww_ref[...])
         + gate(2) * _dot(yg_ref[...], wg_ref[...]))
    f = _dot(m.astype(BF16), wout_ref[...])
    x1 = _layer_norm(DEEPNORM_ALPHA * x_ref[...] + g1_ref[0] * f, ln1g_ref[...], ln1b_ref[...])

    h = (x1 * (1.0 + sc2_ref[0]) + sh2_ref[0]).astype(BF16)
    acc = jnp.zeros(x1.shape, F32)
    for c in range(D_FF // FF_TILE):
        up = jnp.maximum(_dot(h, wup_ref[:, c * FF_TILE:(c + 1) * FF_TILE]), 0.0)
        acc = acc + _dot((up * up).astype(BF16), wdn_ref[c * FF_TILE:(c + 1) * FF_TILE, :])
    o_ref[...] = _layer_norm(DEEPNORM_ALPHA * x1 + g2_ref[0] * acc, ln2g_ref[...], ln2b_ref[...])


def _merge_mlp(x, mod_l, row_of_tile, ya_rows, yw, yg, gates, lw):
    n_tok = x.shape[0]
    tm = ROW_TILE
    row = lambda w: pl.BlockSpec((tm, w), lambda i: (i, 0))
    vec = _resident((1, D_MODEL))
    return pl.pallas_call(
        _merge_mlp_kernel,
        grid=(n_tok // tm,),
        in_specs=[row(D_MODEL)] + [_mod_spec(c, row_of_tile) for c in (2, 3, 4, 5)]
                 + [pl.BlockSpec((S5_NPAIR, CHUNKS_PER_TILE, S5_ROW), lambda i: (0, i, 0)),
                    row(D_ATT), row(D_ATT), row(3 * D_MODEL),
                    _resident((D_SSM, D_SSM)), _resident((D_SSM, D_MODEL)), _resident((D_ATT, D_MODEL)),
                    _resident((D_ATT, D_MODEL)), _resident((D_MODEL, D_MODEL)), vec, vec,
                    _resident((D_MODEL, D_FF)), _resident((D_FF, D_MODEL)), vec, vec],
        out_specs=row(D_MODEL),
        out_shape=jax.ShapeDtypeStruct((n_tok, D_MODEL), F32),
        scratch_shapes=[pltpu.VMEM((LANE_SLABS, tm, LANES), F32)],
        compiler_params=_cparams(1),
        name="merge_mlp_residual",
    )(x, mod_l, mod_l, mod_l, mod_l, ya_rows, yw, yg, gates,
      lw['w_glu'], lw['w_br_ssm'], lw['w_br_win'], lw['w_br_glb'], lw['w_out'], lw['ln1_g'], lw['ln1_b'],
      lw['w_up'], lw['w_down'], lw['ln2_g'], lw['ln2_b'])


def _rope_tables(n_tok):
    rows = n_tok // GRID_W
    row = jnp.repeat(jnp.arange(rows, dtype=F32), GRID_W)
    col = jnp.tile(jnp.arange(GRID_W, dtype=F32), rows)
    n_freq = HEAD_DIM // 4
    inv = ROPE_BASE ** (-jnp.arange(n_freq, dtype=F32) / n_freq)
    ang = jnp.concatenate([row[:, None] * inv, col[:, None] * inv], axis=-1)
    cos, sin = jnp.cos(ang), jnp.sin(ang)
    cos_t = jnp.tile(jnp.concatenate([cos, cos], axis=-1), (1, LANES // HEAD_DIM))
    sin_t = jnp.tile(jnp.concatenate([-sin, sin], axis=-1), (1, LANES // HEAD_DIM))
    return cos_t, sin_t


def _block_diag_ones(n):
    idx = np.arange(n) // HEAD_DIM
    return jnp.asarray(idx[:, None] == idx[None, :], dtype=BF16)


def _permute_q_columns(w):
    def perm(block):
        return block.reshape(D_MODEL, N_HEADS, HEAD_DIM)[:, np.array(Q_SLOT_ORDER), :].reshape(D_MODEL, D_ATT)
    return jnp.concatenate([w[:, :O_QW], perm(w[:, O_QW:O_KW]), w[:, O_KW:O_QG], perm(w[:, O_QG:O_KG]), w[:, O_KG:]], axis=1)


def _layer(x, lw, mod_l, row_of_tile, nb, seq_len, ctx, rope_tabs):
    latent = ctx is not None
    u_rows, qw, qg, gates, kw, kg, vg_t, vw, *kv_f32 = _in_projection(
        x, mod_l, row_of_tile, lw['w_in'], lw['qn'], lw['kn'], lw['bdq'], lw['bdk'], rope_tabs, seq_len,
        vw_transposed=not latent, keep_f32_kv=not latent)
    if latent:
        s0 = [ctx[0][:, d, part].reshape(nb, N_STATE) for d in (0, 1) for part in (0, 1)]
    else:
        s0 = [jnp.zeros((nb, N_STATE), F32)] * 4
    ya_rows, s_fin = _s5_branch(u_rows, lw['s5_ops'], lw['d_skip'], s0, nb, seq_len)

    if latent:
        n_ctx = ctx[1].shape[1]
        k_wc, v_wc, k_gc, v_gc = [t.reshape(nb, n_ctx, D_KV).astype(BF16) for t in ctx[1:]]
        yw = _attention_band(qw, kw, vw, lw['sink'], nb, seq_len, 256, (k_wc, v_wc))
        yg = _attention_full(qg, kg, vg_t, None, nb, seq_len, 256, 256, (k_gc, jnp.swapaxes(v_gc, 1, 2)))
        new_ctx = None
    else:
        yw = _attention_full(qw, kw, vw, lw['sink'], nb, seq_len, seq_len, seq_len)
        yg = _attention_full(qg, kg, vg_t, None, nb, seq_len, seq_len, seq_len)
        new_ctx = (s_fin,) + tuple(kv_f32)
    x2 = _merge_mlp(x, mod_l, row_of_tile, ya_rows, yw, yg, gates, lw)
    return x2, new_ctx


def kernel(x_prompt, x_sample, state_ssm, cache_k_win, cache_v_win, cache_k_glb, cache_v_glb, c, c_ctx, w_mod, b_mod, w_in, ssm_lam_re, ssm_lam_im, ssm_log_step, ssm_b_re, ssm_b_im, ssm_c_re, ssm_c_im, ssm_d, w_glu, sink_win, q_norm_glb, k_norm_glb, w_br_ssm, w_br_win, w_br_glb, w_out, ln1_g, ln1_b, w_up, w_down, ln2_g, ln2_b):
    n_ctx_b, ctx_len, _ = x_prompt.shape
    n_lat_b, lat_len, _ = x_sample.shape
    assert ctx_len % ROW_TILE == 0 or ROW_TILE % ctx_len == 0
    assert lat_len % ROW_TILE == 0 and (n_ctx_b * ctx_len) % ROW_TILE == 0

    cond8 = jnp.zeros((8, D_MODEL), F32).at[0].set(c_ctx).at[1:1 + n_lat_b].set(c)
    mod = _modulation(cond8, w_mod, b_mod).reshape(DEPTH, 8, 1, 6 * D_MODEL)
    rope_tabs = _rope_tables(lat_len)
    bdq, bdk = _block_diag_ones(D_ATT), _block_diag_ones(D_KV)
    lat_tiles = lat_len // ROW_TILE
    ctx_row = lambda i: 0
    lat_row = lambda i: 1 + i // lat_tiles

    xp = x_prompt.reshape(n_ctx_b * ctx_len, D_MODEL)
    xs = x_sample.reshape(n_lat_b * lat_len, D_MODEL)
    new_ssm, new_kw, new_vw, new_kg, new_vg = [], [], [], [], []
    for l in range(DEPTH):
        lw = dict(
            w_in=_permute_q_columns(w_in[l]).astype(BF16),
            qn=jnp.tile(q_norm_glb[l], N_HEADS).reshape(1, D_ATT), kn=jnp.tile(k_norm_glb[l], N_KV).reshape(1, D_KV),
            bdq=bdq, bdk=bdk,
            s5_ops=_s5_operators(ssm_lam_re[l], ssm_lam_im[l], ssm_log_step[l],
                                 ssm_b_re[l], ssm_b_im[l], ssm_c_re[l], ssm_c_im[l]),
            d_skip=ssm_d[l],
            sink=sink_win[l],
            w_glu=w_glu[l].astype(BF16), w_br_ssm=w_br_ssm[l].astype(BF16), w_br_win=w_br_win[l].astype(BF16),
            w_br_glb=w_br_glb[l].astype(BF16), w_out=w_out[l].astype(BF16),
            ln1_g=ln1_g[l].reshape(1, D_MODEL), ln1_b=ln1_b[l].reshape(1, D_MODEL),
            w_up=w_up[l].astype(BF16), w_down=w_down[l].astype(BF16),
            ln2_g=ln2_g[l].reshape(1, D_MODEL), ln2_b=ln2_b[l].reshape(1, D_MODEL),
        )
        xp, (s_fin, kw, vw, kg, vg) = _layer(xp, lw, mod[l], ctx_row, n_ctx_b, ctx_len, None, None)
        new_ssm.append(jnp.stack(s_fin, axis=1).reshape(n_ctx_b, 2, 2, SSM_GROUPS, SSM_STATE))
        for acc, t in ((new_kw, kw), (new_vw, vw), (new_kg, kg), (new_vg, vg)):
            acc.append(t.reshape(n_ctx_b, ctx_len, N_KV, HEAD_DIM))
        ctx = (state_ssm[:, l], cache_k_win[:, l], cache_v_win[:, l], cache_k_glb[:, l], cache_v_glb[:, l])
        xs, _ = _layer(xs, lw, mod[l], lat_row, n_lat_b, lat_len, ctx, rope_tabs)
    return (xp.reshape(x_prompt.shape), xs.reshape(x_sample.shape),
            jnp.stack(new_ssm, axis=1), jnp.stack(new_kw, axis=1), jnp.stack(new_vw, axis=1),
            jnp.stack(new_kg, axis=1), jnp.stack(new_vg, axis=1))
```

```python
import functools

import jax
import jax.numpy as jnp
import numpy as np
from jax import lax
from jax.experimental import pallas as pl
from jax.experimental.pallas import tpu as pltpu

F32 = jnp.float32
BF16 = jnp.bfloat16

D_MODEL = 1024
DEPTH = 2
GRID_W = 64
HEAD_DIM = 64
D_SSM = 512
SSM_GROUP_CH = 16
SSM_GROUPS = 32
SSM_STATE = 64
N_HEADS = 8
N_KV = 2
GRP = N_HEADS // N_KV
D_ATT = N_HEADS * HEAD_DIM
D_KV = N_KV * HEAD_DIM
WINDOW = 128
ROPE_BASE = 10000.0
D_FF = 4 * D_MODEL
LN_EPS = 1e-5
RMS_EPS = 1e-6
ATTN_SCALE = HEAD_DIM ** -0.5
DEEPNORM_ALPHA = (2.0 * DEPTH) ** 0.25
NEG_INF = -1e30
LOG2E = 1.4426950408889634
Q_SCALE = ATTN_SCALE * LOG2E
V_ROWS = HEAD_DIM + 16
N_IN = D_SSM + 2 * (D_ATT + 2 * D_KV) + 3 * D_MODEL
O_U = 0
O_QW = O_U + D_SSM
O_KW = O_QW + D_ATT
O_VW = O_KW + D_KV
O_QG = O_VW + D_KV
O_KG = O_QG + D_ATT
O_VG = O_KG + D_KV
O_GATE = O_VG + D_KV

S5_CHUNK = 16
S5_PAIR = 2 * SSM_GROUP_CH
S5_NPAIR = SSM_GROUPS // 2
S5_PAIR_BLOCK = 4
S5_ROW = S5_CHUNK * S5_PAIR
S5_PSTATE = 2 * SSM_STATE
N_STATE = SSM_GROUPS * SSM_STATE

LANES = 128
ROW_TILE = 512
FF_TILE = 1024
CHUNKS_PER_TILE = ROW_TILE // S5_CHUNK
LANE_SLABS = D_SSM // LANES
PAIRS_PER_SLAB = LANES // S5_PAIR
VMEM_LIMIT = 56 * 1024 * 1024
NT_DIMS = (((1,), (1,)), ((), ()))
TN_DIMS = (((0,), (0,)), ((), ()))


def _cparams(n_axes):
    return pltpu.CompilerParams(dimension_semantics=("arbitrary",) * n_axes, vmem_limit_bytes=VMEM_LIMIT)


def _resident(shape):
    nd = len(shape)
    return pl.BlockSpec(shape, lambda *_: (0,) * nd, pipeline_mode=pl.Buffered(1))


def _dot(a, b):
    return jnp.dot(a, b, preferred_element_type=F32)


def _mod_kernel(c_ref, w_ref, b_ref, o_ref):
    c = c_ref[...]
    a = (c * jax.nn.sigmoid(c)).astype(BF16)
    o_ref[0] = _dot(a, w_ref[0].astype(BF16)) + b_ref[0]


def _modulation(cond8, w_mod, b_mod):
    tn = 512
    n_out = w_mod.shape[-1]
    return pl.pallas_call(
        _mod_kernel,
        grid=(DEPTH, n_out // tn),
        in_specs=[
            pl.BlockSpec((8, D_MODEL), lambda l, n: (0, 0)),
            pl.BlockSpec((1, D_MODEL, tn), lambda l, n: (l, 0, n)),
            pl.BlockSpec((1, 1, tn), lambda l, n: (l, 0, n)),
        ],
        out_specs=pl.BlockSpec((1, 8, tn), lambda l, n: (l, 0, n)),
        out_shape=jax.ShapeDtypeStruct((DEPTH, 8, n_out), F32),
        compiler_params=_cparams(2),
        name="modulation",
    )(cond8, w_mod, b_mod.reshape(DEPTH, 1, n_out))


def _mod_spec(chunk, row_of_tile):
    return pl.BlockSpec((1, 1, D_MODEL), lambda i: (row_of_tile(i), 0, chunk))


def _tokens_to_chunk_rows(u, scr, out_ref):
    n_chunks = u.shape[0] // S5_CHUNK
    for j in range(LANE_SLABS):
        scr[j] = u[:, j * LANES:(j + 1) * LANES]
    pieces = [[None] * S5_CHUNK for _ in range(S5_NPAIR)]
    for j in range(LANE_SLABS):
        for t in range(S5_CHUNK):
            step_rows = scr[j, pl.ds(t, n_chunks, stride=S5_CHUNK), :]
            for m in range(PAIRS_PER_SLAB):
                pieces[j * PAIRS_PER_SLAB + m][t] = step_rows[:, m * S5_PAIR:(m + 1) * S5_PAIR]
    for n in range(S5_NPAIR):
        out_ref[n] = jnp.concatenate(pieces[n], axis=1)


def _chunk_rows_to_tokens(y_ref, scr):
    n_chunks = y_ref.shape[1]
    for j in range(LANE_SLABS):
        for t in range(S5_CHUNK):
            piece = jnp.concatenate(
                [y_ref[j * PAIRS_PER_SLAB + m, :, t * S5_PAIR:(t + 1) * S5_PAIR] for m in range(PAIRS_PER_SLAB)], axis=1)
            scr[j, pl.ds(t, n_chunks, stride=S5_CHUNK), :] = piece
    return jnp.concatenate([scr[j] for j in range(LANE_SLABS)], axis=1)


def _head_rms(x, gain, ones_bd):
    ss = _dot((x * x).astype(BF16), ones_bd)
    return x * lax.rsqrt(ss * (1.0 / HEAD_DIM) + RMS_EPS) * gain


def _rope_chunk(xc, cos_t, sin_t):
    lane = lax.broadcasted_iota(jnp.int32, xc.shape, 1)
    first_half = (lane & (HEAD_DIM - 1)) < (HEAD_DIM // 2)
    partner = jnp.where(first_half, pltpu.roll(xc, LANES - HEAD_DIM // 2, 1), pltpu.roll(xc, HEAD_DIM // 2, 1))
    return xc * cos_t + partner * sin_t


def _inproj_kernel(*refs, rope, vw_transposed, keep_f32_kv):
    n_in = 10 if rope else 8
    x_ref, sh_ref, sc_ref, w_ref, qn_ref, kn_ref, bdq_ref, bdk_ref = refs[:8]
    u_ref, qw_ref, qg_ref, gate_ref, kw_ref, kg_ref, vgt_ref, vw_ref = refs[n_in:n_in + 8]
    f32_refs, u_scr = refs[n_in + 8:-1], refs[-1]
    h = (x_ref[...] * (1.0 + sc_ref[0]) + sh_ref[0]).astype(BF16)

    def proj(lo, width):
        return _dot(h, w_ref[:, lo:lo + width])

    def rotate(x):
        if not rope:
            return x
        chunks = [_rope_chunk(x[:, c * LANES:(c + 1) * LANES], refs[8][...], refs[9][...]) for c in range(x.shape[-1] // LANES)]
        return chunks[0] if len(chunks) == 1 else jnp.concatenate(chunks, axis=1)

    _tokens_to_chunk_rows(proj(O_U, D_SSM), u_scr, u_ref)
    qw_ref[...] = (rotate(proj(O_QW, D_ATT)) * Q_SCALE).astype(BF16)
    qg_ref[...] = (rotate(_head_rms(proj(O_QG, D_ATT), qn_ref[...], bdq_ref[...])) * Q_SCALE).astype(BF16)
    kw = rotate(proj(O_KW, D_KV))
    kg = rotate(_head_rms(proj(O_KG, D_KV), kn_ref[...], bdk_ref[...]))
    vw, vg = proj(O_VW, D_KV), proj(O_VG, D_KV)
    kw_ref[...] = kw.astype(BF16)
    kg_ref[...] = kg.astype(BF16)
    vgt_ref[...] = vg.T.astype(BF16)
    vw_ref[...] = (vw.T if vw_transposed else vw).astype(BF16)
    if keep_f32_kv:
        for out_ref, val in zip(f32_refs, (kw, vw, kg, vg)):
            out_ref[...] = val
    for c in range(3):
        gate = jax.nn.sigmoid(proj(O_GATE + c * D_MODEL, D_MODEL))
        gate_ref[:, c * D_MODEL:(c + 1) * D_MODEL] = gate.astype(gate_ref.dtype)


def _in_projection(x, mod_l, row_of_tile, w_in, qn, kn, bdq, bdk, rope_tabs, seq_len, vw_transposed, keep_f32_kv):
    n_tok = x.shape[0]
    tm = ROW_TILE
    rope = rope_tabs is not None
    row = lambda w: pl.BlockSpec((tm, w), lambda i: (i, 0))
    col = pl.BlockSpec((D_KV, tm), lambda i: (0, i))
    in_specs = [
        row(D_MODEL),
        _mod_spec(0, row_of_tile), _mod_spec(1, row_of_tile),
        _resident((D_MODEL, N_IN)),
        _resident((1, D_ATT)), _resident((1, D_KV)),
        _resident((D_ATT, D_ATT)), _resident((D_KV, D_KV)),
    ]
    args = [x, mod_l, mod_l, w_in, qn, kn, bdq, bdk]
    if rope:
        tiles_per_seq = seq_len // tm
        tab = pl.BlockSpec((tm, LANES), lambda i: (i % tiles_per_seq, 0))
        in_specs += [tab, tab]
        args += list(rope_tabs)
    tok = lambda w, dt: (jax.ShapeDtypeStruct((n_tok, w), dt), row(w))
    tr = (jax.ShapeDtypeStruct((D_KV, n_tok), BF16), col)
    outs = [(jax.ShapeDtypeStruct((S5_NPAIR, n_tok // S5_CHUNK, S5_ROW), F32),
             pl.BlockSpec((S5_NPAIR, CHUNKS_PER_TILE, S5_ROW), lambda i: (0, i, 0))),
            tok(D_ATT, BF16), tok(D_ATT, BF16), tok(3 * D_MODEL, BF16),
            tok(D_KV, BF16), tok(D_KV, BF16), tr, tr if vw_transposed else tok(D_KV, BF16)]
    if keep_f32_kv:
        outs += [tok(D_KV, F32)] * 4
    return pl.pallas_call(
        functools.partial(_inproj_kernel, rope=rope, vw_transposed=vw_transposed, keep_f32_kv=keep_f32_kv),
        grid=(n_tok // tm,),
        in_specs=in_specs,
        out_specs=[o[1] for o in outs],
        out_shape=[o[0] for o in outs],
        scratch_shapes=[pltpu.VMEM((LANE_SLABS, tm, LANES), F32)],
        compiler_params=_cparams(1),
        name="in_projection",
    )(*args)


def _zoh(lr, li, ls):
    dt = jnp.exp(ls)
    mag = jnp.exp(lr * dt)
    ar, ai = mag * jnp.cos(li * dt), mag * jnp.sin(li * dt)
    den = lr * lr + li * li
    fr = ((ar - 1.0) * lr + ai * li) / den
    fi = (ai * lr - (ar - 1.0) * li) / den
    return ar, ai, fr, fi


def _powers(ar, ai, n):
    out = [(jnp.ones_like(ar), jnp.zeros_like(ar))]
    for _ in range(n):
        pr, pi = out[-1]
        out.append((pr * ar - pi * ai, pr * ai + pi * ar))
    return out


def _s5_operator_kernel(lr_ref, li_ref, ls_ref, btr_ref, bti_ref, cr_ref, ci_ref,
                        top_ref, e0, e1, e2, e3, m0, m1, m2, m3, d0, d1, d2, d3, ext_scr):
    tc, cg, p = S5_CHUNK, SSM_GROUP_CH, SSM_STATE
    ends = ((e0, e1), (e2, e3))
    carries = ((m0, m1), (m2, m3))
    decays = ((d0, d1), (d2, d3))
    ext_scr[...] = jnp.zeros(ext_scr.shape, F32)
    zero_half = jnp.zeros((cg, p), F32)

    def place(out_ref, t, g2, val):
        halves = [val, zero_half] if g2 == 0 else [zero_half, val]
        out_ref[0, t * S5_PAIR + g2 * cg:t * S5_PAIR + (g2 + 1) * cg, :] = jnp.concatenate(halves, axis=1).astype(BF16)

    for d in range(2):
        for g2 in range(2):
            ar, ai, fr, fi = _zoh(lr_ref[d, 0, g2], li_ref[d, 0, g2], ls_ref[d, 0, g2])
            btr, bti = btr_ref[d, 0, g2], bti_ref[d, 0, g2]
            bbr, bbi = fr * btr - fi * bti, fr * bti + fi * btr
            cr, ci = cr_ref[d, 0, g2], ci_ref[d, 0, g2]
            pw = _powers(ar, ai, tc)
            group_rows = slice(g2 * cg, (g2 + 1) * cg)
            for j in range(tc + 1):
                pr, pi = pw[j]
                car, cai = cr * pr - ci * pi, cr * pi + ci * pr
                if j >= 1:
                    t = j - 1 if d == 0 else tc - j
                    place(carries[d][0], t, g2, car)
                    place(carries[d][1], t, g2, -cai)
                if j == tc:
                    break
                t = tc - 1 - j if d == 0 else j
                place(ends[d][0], t, g2, pr * bbr - pi * bbi)
                place(ends[d][1], t, g2, pr * bbi + pi * bbr)
                kt = (lax.dot_general(bbr, car, NT_DIMS, precision=lax.Precision.HIGHEST, preferred_element_type=F32)
                      - lax.dot_general(bbi, cai, NT_DIMS, precision=lax.Precision.HIGHEST, preferred_element_type=F32))
                slot = tc - 1 + j if d == 0 else tc - 1 - j
                lanes = slice(slot * S5_PAIR + g2 * cg, slot * S5_PAIR + (g2 + 1) * cg)
                if d == 1 and j == 0:
                    ext_scr[group_rows, lanes] = ext_scr[group_rows, lanes] + kt
                else:
                    ext_scr[group_rows, lanes] = kt
            for part in range(2):
                decays[d][part][:, g2 * p:(g2 + 1) * p] = pw[tc][part]
    for t in range(tc):
        for g2 in range(2):
            rows = slice(t * S5_PAIR + g2 * cg, t * S5_PAIR + (g2 + 1) * cg)
            window = slice((tc - 1 - t) * S5_PAIR, (tc - 1 - t) * S5_PAIR + S5_ROW)
            top_ref[0, rows, :] = ext_scr[g2 * cg:(g2 + 1) * cg, window].astype(BF16)


def _s5_operators(lam_re, lam_im, log_step, b_re, b_im, c_re, c_im):
    g, p, cg = SSM_GROUPS, SSM_STATE, SSM_GROUP_CH
    ls = jnp.broadcast_to(log_step[:, :, None], (2, g, p))

    def rows(v):
        return v.reshape(2, S5_NPAIR, 2, 1, p)

    def mat(v):
        return v.reshape(2, S5_NPAIR, 2, cg, p)

    def spec(*tail):
        return pl.BlockSpec((2, 1, 2) + tail, lambda n: (0, n, 0, 0, 0))

    state_op = pl.BlockSpec((1, S5_ROW, S5_PSTATE), lambda n: (n, 0, 0))
    out = pl.pallas_call(
        _s5_operator_kernel,
        grid=(S5_NPAIR,),
        in_specs=[spec(1, p)] * 3 + [spec(cg, p)] * 4,
        out_specs=([pl.BlockSpec((1, S5_ROW, S5_ROW), lambda n: (n, 0, 0))] + [state_op] * 8
                   + [pl.BlockSpec((1, S5_PSTATE), lambda n: (0, n))] * 4),
        out_shape=([jax.ShapeDtypeStruct((S5_NPAIR, S5_ROW, S5_ROW), BF16)]
                   + [jax.ShapeDtypeStruct((S5_NPAIR, S5_ROW, S5_PSTATE), BF16)] * 8
                   + [jax.ShapeDtypeStruct((1, N_STATE), F32)] * 4),
        scratch_shapes=[pltpu.VMEM((S5_PAIR, 2 * S5_ROW), F32)],
        compiler_params=_cparams(1),
        name="s5_operators",
    )(rows(lam_re), rows(lam_im), rows(ls),
      mat(jnp.swapaxes(b_re, -1, -2)), mat(jnp.swapaxes(b_im, -1, -2)), mat(c_re), mat(c_im))
    return out[0], out[1:5], out[5:9], out[9:13]


def _s5_kernel(u_ref, top_ref, d_ref, e0, e1, e2, e3, m0, m1, m2, m3, a0, a1, a2, a3, i0, i1, i2, i3,
               y_ref, f0, f1, f2, f3, se0, se1, se2, se3, st0, st1, st2, st3, *, nb, nc):
    ends, carries, decays, inits, finals = (e0, e1, e2, e3), (m0, m1, m2, m3), (a0, a1, a2, a3), (i0, i1, i2, i3), (f0, f1, f2, f3)
    se, st = (se0, se1, se2, se3), (st0, st1, st2, st3)
    for p in range(S5_PAIR_BLOCK):
        lanes = slice(p * S5_PSTATE, (p + 1) * S5_PSTATE)
        u = u_ref[p].astype(BF16)
        for k in range(4):
            se[k][:, lanes] = _dot(u, ends[k][p])

    a_fr, a_fi, a_br, a_bi = [a[...] for a in decays]
    for b in range(nb):
        base = b * nc

        def body(c, carry, base=base):
            fr, fi, br, bi = carry
            rc, rb = pl.ds(base + c, 1), pl.ds(base + nc - 1 - c, 1)
            st0[rc, :] = fr
            st1[rc, :] = fi
            st2[rb, :] = br
            st3[rb, :] = bi
            nfr = a_fr * fr - a_fi * fi + se0[rc, :]
            nfi = a_fr * fi + a_fi * fr + se1[rc, :]
            nbr = a_br * br - a_bi * bi + se2[rb, :]
            nbi = a_br * bi + a_bi * br + se3[rb, :]
            return nfr, nfi, nbr, nbi

        last = lax.fori_loop(0, nc, body, tuple(i_ref[b] for i_ref in inits))
        for f_ref, val in zip(finals, last):
            f_ref[b] = val

    for p in range(S5_PAIR_BLOCK):
        lanes = slice(p * S5_PSTATE, (p + 1) * S5_PSTATE)
        u = u_ref[p]
        y = _dot(u.astype(BF16), top_ref[p]) + d_ref[p] * u
        for k in range(4):
            y = y + lax.dot_general(st[k][:, lanes].astype(BF16), carries[k][p], NT_DIMS, preferred_element_type=F32)
        y_ref[p] = jax.nn.gelu(y)


def _s5_branch(u_rows, ops, d_skip, s0, nb, seq_len):
    top, ends, carries, decay = ops
    nc = seq_len // S5_CHUNK
    n_rows = nc * nb
    pg = S5_PAIR_BLOCK
    lanes = pg * S5_PSTATE
    d_rows = jnp.tile(d_skip.reshape(S5_NPAIR, 1, S5_PAIR), (1, S5_CHUNK, 1)).reshape(S5_NPAIR, 1, S5_ROW)
    rows_spec = pl.BlockSpec((pg, n_rows, S5_ROW), lambda n: (n, 0, 0))
    op_spec = pl.BlockSpec((pg, S5_ROW, S5_PSTATE), lambda n: (n, 0, 0))
    state_spec = pl.BlockSpec((nb, 1, lanes), lambda n: (0, 0, n))
    out = pl.pallas_call(
        functools.partial(_s5_kernel, nb=nb, nc=nc),
        grid=(S5_NPAIR // pg,),
        in_specs=[rows_spec, pl.BlockSpec((pg, S5_ROW, S5_ROW), lambda n: (n, 0, 0)),
                  pl.BlockSpec((pg, 1, S5_ROW), lambda n: (n, 0, 0))]
                 + [op_spec] * 8 + [pl.BlockSpec((1, lanes), lambda n: (0, n))] * 4 + [state_spec] * 4,
        out_specs=[rows_spec] + [state_spec] * 4,
        out_shape=[jax.ShapeDtypeStruct(u_rows.shape, F32)] + [jax.ShapeDtypeStruct((nb, 1, N_STATE), F32)] * 4,
        scratch_shapes=[pltpu.VMEM((n_rows, lanes), F32)] * 8,
        compiler_params=_cparams(1),
        name="s5_chunked",
    )(u_rows, top, d_rows, *ends, *carries, *decay, *[s.reshape(nb, 1, N_STATE) for s in s0])
    return out[0], [s.reshape(nb, N_STATE) for s in out[1:]]


Q_SLOT_ORDER = tuple(h for j in range(GRP) for h in (j, GRP + j))


def _stack_group_queries(q_ref, h, tq):
    lane = lax.broadcasted_iota(jnp.int32, (tq, LANES), 1)
    keep = (lane >= h * HEAD_DIM) & (lane < (h + 1) * HEAD_DIM)
    zero = jnp.zeros((tq, LANES), BF16)
    return jnp.concatenate([jnp.where(keep, q_ref[:, j * LANES:(j + 1) * LANES], zero) for j in range(GRP)], axis=0)


def _sink_lanes(sink_ref, h, tq):
    return jnp.concatenate([jnp.full((1, tq), sink_ref[h * GRP + j] * LOG2E, F32) for j in range(GRP)], axis=1)


def _values_with_ones(vt):
    return jnp.concatenate([vt, jnp.ones((V_ROWS - HEAD_DIM, vt.shape[1]), BF16)], axis=0)


def _store_heads(o_ref, h, o_t, tq):
    for j in range(GRP):
        head = h * GRP + j
        o_ref[0, head * HEAD_DIM:(head + 1) * HEAD_DIM, :] = o_t[:, j * tq:(j + 1) * tq].astype(o_ref.dtype)


def _transposed_out(nb, seq_len, tq):
    return (pl.BlockSpec((1, D_ATT, tq), lambda b, i: (b, 0, i)), jax.ShapeDtypeStruct((nb, D_ATT, seq_len), BF16))


def _tokens_major(o_t):
    nb, _, seq_len = o_t.shape
    return o_t.transpose(0, 2, 1).reshape(nb * seq_len, D_ATT)


def _attn_full_kernel(*refs, tq, tk, use_sink, has_cache):
    refs = list(refs)
    sink_ref = refs.pop(0) if use_sink else None
    q_ref, k_ref, vt_ref = refs[:3]
    kc_ref, vtc_ref = refs[3:5] if has_cache else (None, None)
    o_ref, m_ref, acc_ref, sa_ref, sb_ref = refs[-5:]
    n_own = k_ref.shape[0] // tk
    n_chunks = n_own + (kc_ref.shape[1] // tk if has_cache else 0)
    rows = GRP * tq
    qs = [_stack_group_queries(q_ref, h, tq) for h in range(N_KV)]
    for h in range(N_KV):
        if use_sink:
            m_ref[h] = _sink_lanes(sink_ref, h, tq)
            acc_ref[h] = jnp.concatenate([jnp.zeros((HEAD_DIM, rows), F32), jnp.ones((V_ROWS - HEAD_DIM, rows), F32)], axis=0)
        else:
            m_ref[h] = jnp.full((1, rows), NEG_INF, F32)
            acc_ref[h] = jnp.zeros((V_ROWS, rows), F32)

    def keys(c):
        if c < n_own:
            return k_ref[c * tk:(c + 1) * tk, :]
        return kc_ref[0, (c - n_own) * tk:(c - n_own + 1) * tk, :]

    def values_t(c, h):
        hd = slice(h * HEAD_DIM, (h + 1) * HEAD_DIM)
        if c < n_own:
            return vt_ref[hd, c * tk:(c + 1) * tk]
        return vtc_ref[0, hd, (c - n_own) * tk:(c - n_own + 1) * tk]

    def scores(c, s_ref):
        kc = keys(c)
        for h in range(N_KV):
            s_ref[h] = lax.dot_general(kc, qs[h], NT_DIMS, preferred_element_type=F32)

    def consume(c, s_ref):
        for h in range(N_KV):
            s = s_ref[h]
            m_old = m_ref[h]
            m_new = jnp.maximum(m_old, jnp.max(s, axis=0, keepdims=True))
            p = jnp.exp2(s - m_new).astype(BF16)
            alpha = jnp.exp2(m_old - m_new)
            acc_ref[h] = alpha * acc_ref[h] + _dot(_values_with_ones(values_t(c, h)), p)
            m_ref[h] = m_new

    bufs = (sa_ref, sb_ref)
    scores(0, bufs[0])
    for c in range(n_chunks):
        if c + 1 < n_chunks:
            scores(c + 1, bufs[(c + 1) % 2])
        consume(c, bufs[c % 2])
    for h in range(N_KV):
        acc = acc_ref[h]
        _store_heads(o_ref, h, acc[:HEAD_DIM] / acc[HEAD_DIM:HEAD_DIM + 1], tq)


def _attn_band_kernel(sink_ref, q_ref, k_ref, v_ref, kc_ref, vc_ref, o_ref, *, tq):
    i = pl.program_id(1)
    n_lat = k_ref.shape[0]
    span = tq + 2 * WINDOW
    start = pl.multiple_of(jnp.clip(i * tq - WINDOW, 0, n_lat - span), WINDOW)
    rows = GRP * tq
    row = lax.broadcasted_iota(jnp.int32, (span, rows), 0)
    col = lax.broadcasted_iota(jnp.int32, (span, rows), 1)
    in_band = jnp.abs((start + row) - (i * tq + (col & (tq - 1)))) <= WINDOW
    k_band, v_band = k_ref[pl.ds(start, span), :], v_ref[pl.ds(start, span), :]
    k_ctx, v_ctx = kc_ref[0], vc_ref[0]
    qs = [_stack_group_queries(q_ref, h, tq) for h in range(N_KV)]
    s_band = [jnp.where(in_band, lax.dot_general(k_band, qs[h], NT_DIMS, preferred_element_type=F32), NEG_INF)
              for h in range(N_KV)]
    s_ctx = [lax.dot_general(k_ctx, qs[h], NT_DIMS, preferred_element_type=F32) for h in range(N_KV)]
    for h in range(N_KV):
        sink = _sink_lanes(sink_ref, h, tq)
        m = jnp.maximum(jnp.maximum(jnp.max(s_band[h], axis=0, keepdims=True),
                                    jnp.max(s_ctx[h], axis=0, keepdims=True)), sink)
        p_band = jnp.exp2(s_band[h] - m)
        p_ctx = jnp.exp2(s_ctx[h] - m)
        den = jnp.sum(p_band, axis=0, keepdims=True) + jnp.sum(p_ctx, axis=0, keepdims=True) + jnp.exp2(sink - m)
        acc = (lax.dot_general(v_band, p_band.astype(BF16), TN_DIMS, preferred_element_type=F32)
               + lax.dot_general(v_ctx, p_ctx.astype(BF16), TN_DIMS, preferred_element_type=F32))
        _store_heads(o_ref, h, acc[h * HEAD_DIM:(h + 1) * HEAD_DIM] / den, tq)


def _attention_full(q, k, vt, sink, nb, seq_len, tq, tk, cache=None):
    tiles = seq_len // tq
    q_spec = pl.BlockSpec((tq, D_ATT), lambda b, i: (b * tiles + i, 0))
    in_specs = [q_spec, pl.BlockSpec((seq_len, D_KV), lambda b, i: (b, 0)), pl.BlockSpec((D_KV, seq_len), lambda b, i: (0, b))]
    args = [q, k, vt]
    if cache is not None:
        n_c = cache[0].shape[1]
        in_specs += [pl.BlockSpec((1, n_c, D_KV), lambda b, i: (b, 0, 0)), pl.BlockSpec((1, D_KV, n_c), lambda b, i: (b, 0, 0))]
        args += list(cache)
    if sink is not None:
        in_specs, args = [pl.BlockSpec(memory_space=pltpu.SMEM)] + in_specs, [sink] + args
    rows = GRP * tq
    out_spec, out_shape = _transposed_out(nb, seq_len, tq)
    return _tokens_major(pl.pallas_call(
        functools.partial(_attn_full_kernel, tq=tq, tk=tk, use_sink=sink is not None, has_cache=cache is not None),
        grid=(nb, tiles),
        in_specs=in_specs,
        out_specs=out_spec,
        out_shape=out_shape,
        scratch_shapes=[pltpu.VMEM((N_KV, 1, rows), F32), pltpu.VMEM((N_KV, V_ROWS, rows), F32),
                        pltpu.VMEM((N_KV, tk, rows), F32), pltpu.VMEM((N_KV, tk, rows), F32)],
        compiler_params=_cparams(2),
        name="attention_full",
    )(*args))


def _attention_band(q, k, v, sink, nb, seq_len, tq, cache):
    tiles = seq_len // tq
    n_c = cache[0].shape[1]
    q_spec = pl.BlockSpec((tq, D_ATT), lambda b, i: (b * tiles + i, 0))
    kv_spec = pl.BlockSpec((seq_len, D_KV), lambda b, i: (b, 0))
    c_spec = pl.BlockSpec((1, n_c, D_KV), lambda b, i: (b, 0, 0))
    out_spec, out_shape = _transposed_out(nb, seq_len, tq)
    return _tokens_major(pl.pallas_call(
        functools.partial(_attn_band_kernel, tq=tq),
        grid=(nb, tiles),
        in_specs=[pl.BlockSpec(memory_space=pltpu.SMEM), q_spec, kv_spec, kv_spec, c_spec, c_spec],
        out_specs=out_spec,
        out_shape=out_shape,
        compiler_params=_cparams(2),
        name="attention_band",
    )(sink, q, k, v, *cache))


def _layer_norm(z, g, b):
    mu = jnp.mean(z, axis=-1, keepdims=True)
    zc = z - mu
    var = jnp.mean(zc * zc, axis=-1, keepdims=True)
    return zc * lax.rsqrt(var + LN_EPS) * g + b


def _merge_mlp_kernel(x_ref, g1_ref, sh2_ref, sc2_ref, g2_ref, ya_ref, yw_ref, yg_ref, gate_ref,
                      wglu_ref, wa_ref, ww_ref, wg_ref, wout_ref, ln1g_ref, ln1b_ref,
                      wup_ref, wdn_ref, ln2g_ref, ln2b_ref, o_ref, ya_scr):
    ya = _chunk_rows_to_tokens(ya_ref, ya_scr)
    ya = ya * jax.nn.sigmoid(_dot(ya.astype(BF16), wglu_ref[...]))

    def gate(c):
        return gate_ref[:, c * D_MODEL:(c + 1) * D_MODEL].astype(F32)

    m = (gate(0) * _dot(ya.astype(BF16), wa_ref[...])
         + gate(1) * _dot(yw_ref[...], ww_ref[...])
         + gate(2) * _dot(yg_ref[...], wg_ref[...]))
    f = _dot(m.astype(BF16), wout_ref[...])
    x1 = _layer_norm(DEEPNORM_ALPHA * x_ref[...] + g1_ref[0] * f, ln1g_ref[...], ln1b_ref[...])

    h = (x1 * (1.0 + sc2_ref[0]) + sh2_ref[0]).astype(BF16)
    acc = jnp.zeros(x1.shape, F32)
    for c in range(D_FF // FF_TILE):
        up = jnp.maximum(_dot(h, wup_ref[:, c * FF_TILE:(c + 1) * FF_TILE]), 0.0)
        acc = acc + _dot((up * up).astype(BF16), wdn_ref[c * FF_TILE:(c + 1) * FF_TILE, :])
    o_ref[...] = _layer_norm(DEEPNORM_ALPHA * x1 + g2_ref[0] * acc, ln2g_ref[...], ln2b_ref[...])


def _merge_mlp(x, mod_l, row_of_tile, ya_rows, yw, yg, gates, lw):
    n_tok = x.shape[0]
    tm = ROW_TILE
    row = lambda w: pl.BlockSpec((tm, w), lambda i: (i, 0))
    vec = _resident((1, D_MODEL))
    return pl.pallas_call(
        _merge_mlp_kernel,
        grid=(n_tok // tm,),
        in_specs=[row(D_MODEL)] + [_mod_spec(c, row_of_tile) for c in (2, 3, 4, 5)]
                 + [pl.BlockSpec((S5_NPAIR, CHUNKS_PER_TILE, S5_ROW), lambda i: (0, i, 0)),
                    row(D_ATT), row(D_ATT), row(3 * D_MODEL),
                    _resident((D_SSM, D_SSM)), _resident((D_SSM, D_MODEL)), _resident((D_ATT, D_MODEL)),
                    _resident((D_ATT, D_MODEL)), _resident((D_MODEL, D_MODEL)), vec, vec,
                    _resident((D_MODEL, D_FF)), _resident((D_FF, D_MODEL)), vec, vec],
        out_specs=row(D_MODEL),
        out_shape=jax.ShapeDtypeStruct((n_tok, D_MODEL), F32),
        scratch_shapes=[pltpu.VMEM((LANE_SLABS, tm, LANES), F32)],
        compiler_params=_cparams(1),
        name="merge_mlp_residual",
    )(x, mod_l, mod_l, mod_l, mod_l, ya_rows, yw, yg, gates,
      lw['w_glu'], lw['w_br_ssm'], lw['w_br_win'], lw['w_br_glb'], lw['w_out'], lw['ln1_g'], lw['ln1_b'],
      lw['w_up'], lw['w_down'], lw['ln2_g'], lw['ln2_b'])


def _rope_tables(n_tok):
    rows = n_tok // GRID_W
    row = jnp.repeat(jnp.arange(rows, dtype=F32), GRID_W)
    col = jnp.tile(jnp.arange(GRID_W, dtype=F32), rows)
    n_freq = HEAD_DIM // 4
    inv = ROPE_BASE ** (-jnp.arange(n_freq, dtype=F32) / n_freq)
    ang = jnp.concatenate([row[:, None] * inv, col[:, None] * inv], axis=-1)
    cos, sin = jnp.cos(ang), jnp.sin(ang)
    cos_t = jnp.tile(jnp.concatenate([cos, cos], axis=-1), (1, LANES // HEAD_DIM))
    sin_t = jnp.tile(jnp.concatenate([-sin, sin], axis=-1), (1, LANES // HEAD_DIM))
    return cos_t, sin_t


def _block_diag_ones(n):
    idx = np.arange(n) // HEAD_DIM
    return jnp.asarray(idx[:, None] == idx[None, :], dtype=BF16)


def _permute_q_columns(w):
    def perm(block):
        return block.reshape(D_MODEL, N_HEADS, HEAD_DIM)[:, np.array(Q_SLOT_ORDER), :].reshape(D_MODEL, D_ATT)
    return jnp.concatenate([w[:, :O_QW], perm(w[:, O_QW:O_KW]), w[:, O_KW:O_QG], perm(w[:, O_QG:O_KG]), w[:, O_KG:]], axis=1)


def _layer(x, lw, mod_l, row_of_tile, nb, seq_len, ctx, rope_tabs):
    latent = ctx is not None
    u_rows, qw, qg, gates, kw, kg, vg_t, vw, *kv_f32 = _in_projection(
        x, mod_l, row_of_tile, lw['w_in'], lw['qn'], lw['kn'], lw['bdq'], lw['bdk'], rope_tabs, seq_len,
        vw_transposed=not latent, keep_f32_kv=not latent)
    if latent:
        s0 = [ctx[0][:, d, part].reshape(nb, N_STATE) for d in (0, 1) for part in (0, 1)]
    else:
        s0 = [jnp.zeros((nb, N_STATE), F32)] * 4
    ya_rows, s_fin = _s5_branch(u_rows, lw['s5_ops'], lw['d_skip'], s0, nb, seq_len)

    if latent:
        n_ctx = ctx[1].shape[1]
        k_wc, v_wc, k_gc, v_gc = [t.reshape(nb, n_ctx, D_KV).astype(BF16) for t in ctx[1:]]
        yw = _attention_band(qw, kw, vw, lw['sink'], nb, seq_len, 256, (k_wc, v_wc))
        yg = _attention_full(qg, kg, vg_t, None, nb, seq_len, 256, 256, (k_gc, jnp.swapaxes(v_gc, 1, 2)))
        new_ctx = None
    else:
        yw = _attention_full(qw, kw, vw, lw['sink'], nb, seq_len, seq_len, seq_len)
        yg = _attention_full(qg, kg, vg_t, None, nb, seq_len, seq_len, seq_len)
        new_ctx = (s_fin,) + tuple(kv_f32)
    x2 = _merge_mlp(x, mod_l, row_of_tile, ya_rows, yw, yg, gates, lw)
    return x2, new_ctx


def kernel(x_prompt, x_sample, state_ssm, cache_k_win, cache_v_win, cache_k_glb, cache_v_glb, c, c_ctx, w_mod, b_mod, w_in, ssm_lam_re, ssm_lam_im, ssm_log_step, ssm_b_re, ssm_b_im, ssm_c_re, ssm_c_im, ssm_d, w_glu, sink_win, q_norm_glb, k_norm_glb, w_br_ssm, w_br_win, w_br_glb, w_out, ln1_g, ln1_b, w_up, w_down, ln2_g, ln2_b):
    n_ctx_b, ctx_len, _ = x_prompt.shape
    n_lat_b, lat_len, _ = x_sample.shape
    assert ctx_len % ROW_TILE == 0 or ROW_TILE % ctx_len == 0
    assert lat_len % ROW_TILE == 0 and (n_ctx_b * ctx_len) % ROW_TILE == 0

    cond8 = jnp.zeros((8, D_MODEL), F32).at[0].set(c_ctx).at[1:1 + n_lat_b].set(c)
    mod = _modulation(cond8, w_mod, b_mod).reshape(DEPTH, 8, 1, 6 * D_MODEL)
    rope_tabs = _rope_tables(lat_len)
    bdq, bdk = _block_diag_ones(D_ATT), _block_diag_ones(D_KV)
    lat_tiles = lat_len // ROW_TILE
    ctx_row = lambda i: 0
    lat_row = lambda i: 1 + i // lat_tiles

    xp = x_prompt.reshape(n_ctx_b * ctx_len, D_MODEL)
    xs = x_sample.reshape(n_lat_b * lat_len, D_MODEL)
    new_ssm, new_kw, new_vw, new_kg, new_vg = [], [], [], [], []
    for l in range(DEPTH):
        lw = dict(
            w_in=_permute_q_columns(w_in[l]).astype(BF16),
            qn=jnp.tile(q_norm_glb[l], N_HEADS).reshape(1, D_ATT), kn=jnp.tile(k_norm_glb[l], N_KV).reshape(1, D_KV),
            bdq=bdq, bdk=bdk,
            s5_ops=_s5_operators(ssm_lam_re[l], ssm_lam_im[l], ssm_log_step[l],
                                 ssm_b_re[l], ssm_b_im[l], ssm_c_re[l], ssm_c_im[l]),
            d_skip=ssm_d[l],
            sink=sink_win[l],
            w_glu=w_glu[l].astype(BF16), w_br_ssm=w_br_ssm[l].astype(BF16), w_br_win=w_br_win[l].astype(BF16),
            w_br_glb=w_br_glb[l].astype(BF16), w_out=w_out[l].astype(BF16),
            ln1_g=ln1_g[l].reshape(1, D_MODEL), ln1_b=ln1_b[l].reshape(1, D_MODEL),
            w_up=w_up[l].astype(BF16), w_down=w_down[l].astype(BF16),
            ln2_g=ln2_g[l].reshape(1, D_MODEL), ln2_b=ln2_b[l].reshape(1, D_MODEL),
        )
        xp, (s_fin, kw, vw, kg, vg) = _layer(xp, lw, mod[l], ctx_row, n_ctx_b, ctx_len, None, None)
        new_ssm.append(jnp.stack(s_fin, axis=1).reshape(n_ctx_b, 2, 2, SSM_GROUPS, SSM_STATE))
        for acc, t in ((new_kw, kw), (new_vw, vw), (new_kg, kg), (new_vg, vg)):
            acc.append(t.reshape(n_ctx_b, ctx_len, N_KV, HEAD_DIM))
        ctx = (state_ssm[:, l], cache_k_win[:, l], cache_v_win[:, l], cache_k_glb[:, l], cache_v_glb[:, l])
        xs, _ = _layer(xs, lw, mod[l], lat_row, n_lat_b, lat_len, ctx, rope_tabs)
    return (xp.reshape(x_prompt.shape), xs.reshape(x_sample.shape),
            jnp.stack(new_ssm, axis=1), jnp.stack(new_kw, axis=1), jnp.stack(new_vw, axis=1),
            jnp.stack(new_kg, axis=1), jnp.stack(new_vg, axis=1))
```

```python
import functools

import jax
import jax.numpy as jnp
import numpy as np
from jax import lax
from jax.experimental import pallas as pl
from jax.experimental.pallas import tpu as pltpu

F32 = jnp.float32
BF16 = jnp.bfloat16

D_MODEL = 1024
DEPTH = 2
GRID_W = 64
HEAD_DIM = 64
D_SSM = 512
SSM_GROUP_CH = 16
SSM_GROUPS = 32
SSM_STATE = 64
N_HEADS = 8
N_KV = 2
GRP = N_HEADS // N_KV
D_ATT = N_HEADS * HEAD_DIM
D_KV = N_KV * HEAD_DIM
WINDOW = 128
ROPE_BASE = 10000.0
D_FF = 4 * D_MODEL
LN_EPS = 1e-5
RMS_EPS = 1e-6
ATTN_SCALE = HEAD_DIM ** -0.5
DEEPNORM_ALPHA = (2.0 * DEPTH) ** 0.25
NEG_INF = -1e30
LOG2E = 1.4426950408889634
Q_SCALE = ATTN_SCALE * LOG2E
V_ROWS = HEAD_DIM + 16
N_IN = D_SSM + 2 * (D_ATT + 2 * D_KV) + 3 * D_MODEL
O_U = 0
O_QW = O_U + D_SSM
O_KW = O_QW + D_ATT
O_VW = O_KW + D_KV
O_QG = O_VW + D_KV
O_KG = O_QG + D_ATT
O_VG = O_KG + D_KV
O_GATE = O_VG + D_KV

S5_CHUNK = 16
S5_PAIR = 2 * SSM_GROUP_CH
S5_NPAIR = SSM_GROUPS // 2
S5_PAIR_BLOCK = 4
S5_ROW = S5_CHUNK * S5_PAIR
S5_PSTATE = 2 * SSM_STATE
N_STATE = SSM_GROUPS * SSM_STATE

LANES = 128
ROW_TILE = 512
FF_TILE = 1024
CHUNKS_PER_TILE = ROW_TILE // S5_CHUNK
LANE_SLABS = D_SSM // LANES
PAIRS_PER_SLAB = LANES // S5_PAIR
VMEM_LIMIT = 56 * 1024 * 1024
NT_DIMS = (((1,), (1,)), ((), ()))
TN_DIMS = (((0,), (0,)), ((), ()))


def _cparams(n_axes):
    return pltpu.CompilerParams(dimension_semantics=("arbitrary",) * n_axes, vmem_limit_bytes=VMEM_LIMIT)


def _resident(shape):
    nd = len(shape)
    return pl.BlockSpec(shape, lambda *_: (0,) * nd, pipeline_mode=pl.Buffered(1))


def _dot(a, b):
    return jnp.dot(a, b, preferred_element_type=F32)


def _mod_kernel(c_ref, w_ref, b_ref, o_ref):
    c = c_ref[...]
    a = (c * jax.nn.sigmoid(c)).astype(BF16)
    o_ref[0] = _dot(a, w_ref[0].astype(BF16)) + b_ref[0]


def _modulation(cond8, w_mod, b_mod):
    tn = 512
    n_out = w_mod.shape[-1]
    return pl.pallas_call(
        _mod_kernel,
        grid=(DEPTH, n_out // tn),
        in_specs=[
            pl.BlockSpec((8, D_MODEL), lambda l, n: (0, 0)),
            pl.BlockSpec((1, D_MODEL, tn), lambda l, n: (l, 0, n)),
            pl.BlockSpec((1, 1, tn), lambda l, n: (l, 0, n)),
        ],
        out_specs=pl.BlockSpec((1, 8, tn), lambda l, n: (l, 0, n)),
        out_shape=jax.ShapeDtypeStruct((DEPTH, 8, n_out), F32),
        compiler_params=_cparams(2),
        name="modulation",
    )(cond8, w_mod, b_mod.reshape(DEPTH, 1, n_out))


def _mod_spec(chunk, row_of_tile):
    return pl.BlockSpec((1, 1, D_MODEL), lambda i: (row_of_tile(i), 0, chunk))


def _tokens_to_chunk_rows(u, scr, out_ref):
    n_chunks = u.shape[0] // S5_CHUNK
    for j in range(LANE_SLABS):
        scr[j] = u[:, j * LANES:(j + 1) * LANES]
    pieces = [[None] * S5_CHUNK for _ in range(S5_NPAIR)]
    for j in range(LANE_SLABS):
        for t in range(S5_CHUNK):
            step_rows = scr[j, pl.ds(t, n_chunks, stride=S5_CHUNK), :]
            for m in range(PAIRS_PER_SLAB):
                pieces[j * PAIRS_PER_SLAB + m][t] = step_rows[:, m * S5_PAIR:(m + 1) * S5_PAIR]
    for n in range(S5_NPAIR):
        out_ref[n] = jnp.concatenate(pieces[n], axis=1)


def _chunk_rows_to_tokens(y_ref, scr):
    n_chunks = y_ref.shape[1]
    for j in range(LANE_SLABS):
        for t in range(S5_CHUNK):
            piece = jnp.concatenate(
                [y_ref[j * PAIRS_PER_SLAB + m, :, t * S5_PAIR:(t + 1) * S5_PAIR] for m in range(PAIRS_PER_SLAB)], axis=1)
            scr[j, pl.ds(t, n_chunks, stride=S5_CHUNK), :] = piece
    return jnp.concatenate([scr[j] for j in range(LANE_SLABS)], axis=1)


def _head_rms(x, gain, ones_bd):
    ss = _dot((x * x).astype(BF16), ones_bd)
    return x * lax.rsqrt(ss * (1.0 / HEAD_DIM) + RMS_EPS) * gain


def _rope_chunk(xc, cos_t, sin_t):
    lane = lax.broadcasted_iota(jnp.int32, xc.shape, 1)
    first_half = (lane & (HEAD_DIM - 1)) < (HEAD_DIM // 2)
    partner = jnp.where(first_half, pltpu.roll(xc, LANES - HEAD_DIM // 2, 1), pltpu.roll(xc, HEAD_DIM // 2, 1))
    return xc * cos_t + partner * sin_t


def _inproj_kernel(*refs, rope, vw_transposed, keep_f32_kv):
    n_in = 10 if rope else 8
    x_ref, sh_ref, sc_ref, w_ref, qn_ref, kn_ref, bdq_ref, bdk_ref = refs[:8]
    u_ref, qw_ref, qg_ref, gate_ref, kw_ref, kg_ref, vgt_ref, vw_ref = refs[n_in:n_in + 8]
    f32_refs, u_scr = refs[n_in + 8:-1], refs[-1]
    h = (x_ref[...] * (1.0 + sc_ref[0]) + sh_ref[0]).astype(BF16)

    def proj(lo, width):
        return _dot(h, w_ref[:, lo:lo + width])

    def rotate(x):
        if not rope:
            return x
        chunks = [_rope_chunk(x[:, c * LANES:(c + 1) * LANES], refs[8][...], refs[9][...]) for c in range(x.shape[-1] // LANES)]
        return chunks[0] if len(chunks) == 1 else jnp.concatenate(chunks, axis=1)

    _tokens_to_chunk_rows(proj(O_U, D_SSM), u_scr, u_ref)
    qw_ref[...] = (rotate(proj(O_QW, D_ATT)) * Q_SCALE).astype(BF16)
    qg_ref[...] = (rotate(_head_rms(proj(O_QG, D_ATT), qn_ref[...], bdq_ref[...])) * Q_SCALE).astype(BF16)
    kw = rotate(proj(O_KW, D_KV))
    kg = rotate(_head_rms(proj(O_KG, D_KV), kn_ref[...], bdk_ref[...]))
    vw, vg = proj(O_VW, D_KV), proj(O_VG, D_KV)
    kw_ref[...] = kw.astype(BF16)
    kg_ref[...] = kg.astype(BF16)
    vgt_ref[...] = vg.T.astype(BF16)
    vw_ref[...] = (vw.T if vw_transposed else vw).astype(BF16)
    if keep_f32_kv:
        for out_ref, val in zip(f32_refs, (kw, vw, kg, vg)):
            out_ref[...] = val
    for c in range(3):
        gate = jax.nn.sigmoid(proj(O_GATE + c * D_MODEL, D_MODEL))
        gate_ref[:, c * D_MODEL:(c + 1) * D_MODEL] = gate.astype(gate_ref.dtype)


def _in_projection(x, mod_l, row_of_tile, w_in, qn, kn, bdq, bdk, rope_tabs, seq_len, vw_transposed, keep_f32_kv):
    n_tok = x.shape[0]
    tm = ROW_TILE
    rope = rope_tabs is not None
    row = lambda w: pl.BlockSpec((tm, w), lambda i: (i, 0))
    col = pl.BlockSpec((D_KV, tm), lambda i: (0, i))
    in_specs = [
        row(D_MODEL),
        _mod_spec(0, row_of_tile), _mod_spec(1, row_of_tile),
        _resident((D_MODEL, N_IN)),
        _resident((1, D_ATT)), _resident((1, D_KV)),
        _resident((D_ATT, D_ATT)), _resident((D_KV, D_KV)),
    ]
    args = [x, mod_l, mod_l, w_in, qn, kn, bdq, bdk]
    if rope:
        tiles_per_seq = seq_len // tm
        tab = pl.BlockSpec((tm, LANES), lambda i: (i % tiles_per_seq, 0))
        in_specs += [tab, tab]
        args += list(rope_tabs)
    tok = lambda w, dt: (jax.ShapeDtypeStruct((n_tok, w), dt), row(w))
    tr = (jax.ShapeDtypeStruct((D_KV, n_tok), BF16), col)
    outs = [(jax.ShapeDtypeStruct((S5_NPAIR, n_tok // S5_CHUNK, S5_ROW), F32),
             pl.BlockSpec((S5_NPAIR, CHUNKS_PER_TILE, S5_ROW), lambda i: (0, i, 0))),
            tok(D_ATT, BF16), tok(D_ATT, BF16), tok(3 * D_MODEL, BF16),
            tok(D_KV, BF16), tok(D_KV, BF16), tr, tr if vw_transposed else tok(D_KV, BF16)]
    if keep_f32_kv:
        outs += [tok(D_KV, F32)] * 4
    return pl.pallas_call(
        functools.partial(_inproj_kernel, rope=rope, vw_transposed=vw_transposed, keep_f32_kv=keep_f32_kv),
        grid=(n_tok // tm,),
        in_specs=in_specs,
        out_specs=[o[1] for o in outs],
        out_shape=[o[0] for o in outs],
        scratch_shapes=[pltpu.VMEM((LANE_SLABS, tm, LANES), F32)],
        compiler_params=_cparams(1),
        name="in_projection",
    )(*args)


def _zoh(lr, li, ls):
    dt = jnp.exp(ls)
    mag = jnp.exp(lr * dt)
    ar, ai = mag * jnp.cos(li * dt), mag * jnp.sin(li * dt)
    den = lr * lr + li * li
    fr = ((ar - 1.0) * lr + ai * li) / den
    fi = (ai * lr - (ar - 1.0) * li) / den
    return ar, ai, fr, fi


def _powers(ar, ai, n):
    out = [(jnp.ones_like(ar), jnp.zeros_like(ar))]
    for _ in range(n):
        pr, pi = out[-1]
        out.append((pr * ar - pi * ai, pr * ai + pi * ar))
    return out


def _s5_operator_kernel(lr_ref, li_ref, ls_ref, btr_ref, bti_ref, cr_ref, ci_ref,
                        top_ref, e0, e1, e2, e3, m0, m1, m2, m3, d0, d1, d2, d3, ext_scr):
    tc, cg, p = S5_CHUNK, SSM_GROUP_CH, SSM_STATE
    ends = ((e0, e1), (e2, e3))
    carries = ((m0, m1), (m2, m3))
    decays = ((d0, d1), (d2, d3))
    ext_scr[...] = jnp.zeros(ext_scr.shape, F32)
    zero_half = jnp.zeros((cg, p), F32)

    def place(out_ref, t, g2, val):
        halves = [val, zero_half] if g2 == 0 else [zero_half, val]
        out_ref[0, t * S5_PAIR + g2 * cg:t * S5_PAIR + (g2 + 1) * cg, :] = jnp.concatenate(halves, axis=1).astype(BF16)

    for d in range(2):
        for g2 in range(2):
            ar, ai, fr, fi = _zoh(lr_ref[d, 0, g2], li_ref[d, 0, g2], ls_ref[d, 0, g2])
            btr, bti = btr_ref[d, 0, g2], bti_ref[d, 0, g2]
            bbr, bbi = fr * btr - fi * bti, fr * bti + fi * btr
            cr, ci = cr_ref[d, 0, g2], ci_ref[d, 0, g2]
            pw = _powers(ar, ai, tc)
            group_rows = slice(g2 * cg, (g2 + 1) * cg)
            for j in range(tc + 1):
                pr, pi = pw[j]
                car, cai = cr * pr - ci * pi, cr * pi + ci * pr
                if j >= 1:
                    t = j - 1 if d == 0 else tc - j
                    place(carries[d][0], t, g2, car)
                    place(carries[d][1], t, g2, -cai)
                if j == tc:
                    break
                t = tc - 1 - j if d == 0 else j
                place(ends[d][0], t, g2, pr * bbr - pi * bbi)
                place(ends[d][1], t, g2, pr * bbi + pi * bbr)
                kt = (lax.dot_general(bbr, car, NT_DIMS, precision=lax.Precision.HIGHEST, preferred_element_type=F32)
                      - lax.dot_general(bbi, cai, NT_DIMS, precision=lax.Precision.HIGHEST, preferred_element_type=F32))
                slot = tc - 1 + j if d == 0 else tc - 1 - j
                lanes = slice(slot * S5_PAIR + g2 * cg, slot * S5_PAIR + (g2 + 1) * cg)
                if d == 1 and j == 0:
                    ext_scr[group_rows, lanes] = ext_scr[group_rows, lanes] + kt
                else:
                    ext_scr[group_rows, lanes] = kt
            for part in range(2):
                decays[d][part][:, g2 * p:(g2 + 1) * p] = pw[tc][part]
    for t in range(tc):
        for g2 in range(2):
            rows = slice(t * S5_PAIR + g2 * cg, t * S5_PAIR + (g2 + 1) * cg)
            window = slice((tc - 1 - t) * S5_PAIR, (tc - 1 - t) * S5_PAIR + S5_ROW)
            top_ref[0, rows, :] = ext_scr[g2 * cg:(g2 + 1) * cg, window].astype(BF16)


def _s5_operators(lam_re, lam_im, log_step, b_re, b_im, c_re, c_im):
    g, p, cg = SSM_GROUPS, SSM_STATE, SSM_GROUP_CH
    ls = jnp.broadcast_to(log_step[:, :, None], (2, g, p))

    def rows(v):
        return v.reshape(2, S5_NPAIR, 2, 1, p)

    def mat(v):
        return v.reshape(2, S5_NPAIR, 2, cg, p)

    def spec(*tail):
        return pl.BlockSpec((2, 1, 2) + tail, lambda n: (0, n, 0, 0, 0))

    state_op = pl.BlockSpec((1, S5_ROW, S5_PSTATE), lambda n: (n, 0, 0))
    out = pl.pallas_call(
        _s5_operator_kernel,
        grid=(S5_NPAIR,),
        in_specs=[spec(1, p)] * 3 + [spec(cg, p)] * 4,
        out_specs=([pl.BlockSpec((1, S5_ROW, S5_ROW), lambda n: (n, 0, 0))] + [state_op] * 8
                   + [pl.BlockSpec((1, S5_PSTATE), lambda n: (0, n))] * 4),
        out_shape=([jax.ShapeDtypeStruct((S5_NPAIR, S5_ROW, S5_ROW), BF16)]
                   + [jax.ShapeDtypeStruct((S5_NPAIR, S5_ROW, S5_PSTATE), BF16)] * 8
                   + [jax.ShapeDtypeStruct((1, N_STATE), F32)] * 4),
        scratch_shapes=[pltpu.VMEM((S5_PAIR, 2 * S5_ROW), F32)],
        compiler_params=_cparams(1),
        name="s5_operators",
    )(rows(lam_re), rows(lam_im), rows(ls),
      mat(jnp.swapaxes(b_re, -1, -2)), mat(jnp.swapaxes(b_im, -1, -2)), mat(c_re), mat(c_im))
    return out[0], out[1:5], out[5:9], out[9:13]


def _s5_kernel(u_ref, top_ref, d_ref, e0, e1, e2, e3, m0, m1, m2, m3, a0, a1, a2, a3, i0, i1, i2, i3,
               y_ref, f0, f1, f2, f3, se0, se1, se2, se3, st0, st1, st2, st3, *, nb, nc):
    ends, carries, decays, inits, finals = (e0, e1, e2, e3), (m0, m1, m2, m3), (a0, a1, a2, a3), (i0, i1, i2, i3), (f0, f1, f2, f3)
    se, st = (se0, se1, se2, se3), (st0, st1, st2, st3)
    for p in range(S5_PAIR_BLOCK):
        lanes = slice(p * S5_PSTATE, (p + 1) * S5_PSTATE)
        u = u_ref[p].astype(BF16)
        for k in range(4):
            se[k][:, lanes] = _dot(u, ends[k][p])

    a_fr, a_fi, a_br, a_bi = [a[...] for a in decays]
    for b in range(nb):
        base = b * nc

        def body(c, carry, base=base):
            fr, fi, br, bi = carry
            rc, rb = pl.ds(base + c, 1), pl.ds(base + nc - 1 - c, 1)
            st0[rc, :] = fr
            st1[rc, :] = fi
            st2[rb, :] = br
            st3[rb, :] = bi
            nfr = a_fr * fr - a_fi * fi + se0[rc, :]
            nfi = a_fr * fi + a_fi * fr + se1[rc, :]
            nbr = a_br * br - a_bi * bi + se2[rb, :]
            nbi = a_br * bi + a_bi * br + se3[rb, :]
            return nfr, nfi, nbr, nbi

        last = lax.fori_loop(0, nc, body, tuple(i_ref[b] for i_ref in inits))
        for f_ref, val in zip(finals, last):
            f_ref[b] = val

    for p in range(S5_PAIR_BLOCK):
        lanes = slice(p * S5_PSTATE, (p + 1) * S5_PSTATE)
        u = u_ref[p]
        y = _dot(u.astype(BF16), top_ref[p]) + d_ref[p] * u
        for k in range(4):
            y = y + lax.dot_general(st[k][:, lanes].astype(BF16), carries[k][p], NT_DIMS, preferred_element_type=F32)
        y_ref[p] = jax.nn.gelu(y)


def _s5_branch(u_rows, ops, d_skip, s0, nb, seq_len):
    top, ends, carries, decay = ops
    nc = seq_len // S5_CHUNK
    n_rows = nc * nb
    pg = S5_PAIR_BLOCK
    lanes = pg * S5_PSTATE
    d_rows = jnp.tile(d_skip.reshape(S5_NPAIR, 1, S5_PAIR), (1, S5_CHUNK, 1)).reshape(S5_NPAIR, 1, S5_ROW)
    rows_spec = pl.BlockSpec((pg, n_rows, S5_ROW), lambda n: (n, 0, 0))
    op_spec = pl.BlockSpec((pg, S5_ROW, S5_PSTATE), lambda n: (n, 0, 0))
    state_spec = pl.BlockSpec((nb, 1, lanes), lambda n: (0, 0, n))
    out = pl.pallas_call(
        functools.partial(_s5_kernel, nb=nb, nc=nc),
        grid=(S5_NPAIR // pg,),
        in_specs=[rows_spec, pl.BlockSpec((pg, S5_ROW, S5_ROW), lambda n: (n, 0, 0)),
                  pl.BlockSpec((pg, 1, S5_ROW), lambda n: (n, 0, 0))]
                 + [op_spec] * 8 + [pl.BlockSpec((1, lanes), lambda n: (0, n))] * 4 + [state_spec] * 4,
        out_specs=[rows_spec] + [state_spec] * 4,
        out_shape=[jax.ShapeDtypeStruct(u_rows.shape, F32)] + [jax.ShapeDtypeStruct((nb, 1, N_STATE), F32)] * 4,
        scratch_shapes=[pltpu.VMEM((n_rows, lanes), F32)] * 8,
        compiler_params=_cparams(1),
        name="s5_chunked",
    )(u_rows, top, d_rows, *ends, *carries, *decay, *[s.reshape(nb, 1, N_STATE) for s in s0])
    return out[0], [s.reshape(nb, N_STATE) for s in out[1:]]


Q_SLOT_ORDER = tuple(h for j in range(GRP) for h in (j, GRP + j))


def _stack_group_queries(q_ref, h, tq):
    lane = lax.broadcasted_iota(jnp.int32, (tq, LANES), 1)
    keep = (lane >= h * HEAD_DIM) & (lane < (h + 1) * HEAD_DIM)
    zero = jnp.zeros((tq, LANES), BF16)
    return jnp.concatenate([jnp.where(keep, q_ref[:, j * LANES:(j + 1) * LANES], zero) for j in range(GRP)], axis=0)


def _sink_lanes(sink_ref, h, tq):
    return jnp.concatenate([jnp.full((1, tq), sink_ref[h * GRP + j] * LOG2E, F32) for j in range(GRP)], axis=1)


def _values_with_ones(vt):
    return jnp.concatenate([vt, jnp.ones((V_ROWS - HEAD_DIM, vt.shape[1]), BF16)], axis=0)


def _store_heads(o_ref, h, o_t, tq):
    for j in range(GRP):
        head = h * GRP + j
        o_ref[0, head * HEAD_DIM:(head + 1) * HEAD_DIM, :] = o_t[:, j * tq:(j + 1) * tq].astype(o_ref.dtype)


def _transposed_out(nb, seq_len, tq):
    return (pl.BlockSpec((1, D_ATT, tq), lambda b, i: (b, 0, i)), jax.ShapeDtypeStruct((nb, D_ATT, seq_len), BF16))


def _tokens_major(o_t):
    nb, _, seq_len = o_t.shape
    return o_t.transpose(0, 2, 1).reshape(nb * seq_len, D_ATT)


def _attn_full_kernel(*refs, tq, tk, use_sink, has_cache):
    refs = list(refs)
    sink_ref = refs.pop(0) if use_sink else None
    q_ref, k_ref, vt_ref = refs[:3]
    kc_ref, vtc_ref = refs[3:5] if has_cache else (None, None)
    o_ref, m_ref, acc_ref, sa_ref, sb_ref = refs[-5:]
    n_own = k_ref.shape[0] // tk
    n_chunks = n_own + (kc_ref.shape[1] // tk if has_cache else 0)
    rows = GRP * tq
    qs = [_stack_group_queries(q_ref, h, tq) for h in range(N_KV)]
    for h in range(N_KV):
        if use_sink:
            m_ref[h] = _sink_lanes(sink_ref, h, tq)
            acc_ref[h] = jnp.concatenate([jnp.zeros((HEAD_DIM, rows), F32), jnp.ones((V_ROWS - HEAD_DIM, rows), F32)], axis=0)
        else:
            m_ref[h] = jnp.full((1, rows), NEG_INF, F32)
            acc_ref[h] = jnp.zeros((V_ROWS, rows), F32)

    def keys(c):
        if c < n_own:
            return k_ref[c * tk:(c + 1) * tk, :]
        return kc_ref[0, (c - n_own) * tk:(c - n_own + 1) * tk, :]

    def values_t(c, h):
        hd = slice(h * HEAD_DIM, (h + 1) * HEAD_DIM)
        if c < n_own:
            return vt_ref[hd, c * tk:(c + 1) * tk]
        return vtc_ref[0, hd, (c - n_own) * tk:(c - n_own + 1) * tk]

    def scores(c, s_ref):
        kc = keys(c)
        for h in range(N_KV):
            s_ref[h] = lax.dot_general(kc, qs[h], NT_DIMS, preferred_element_type=F32)

    def consume(c, s_ref):
        for h in range(N_KV):
            s = s_ref[h]
            m_old = m_ref[h]
            m_new = jnp.maximum(m_old, jnp.max(s, axis=0, keepdims=True))
            p = jnp.exp2(s - m_new).astype(BF16)
            alpha = jnp.exp2(m_old - m_new)
            acc_ref[h] = alpha * acc_ref[h] + _dot(_values_with_ones(values_t(c, h)), p)
            m_ref[h] = m_new

    bufs = (sa_ref, sb_ref)
    scores(0, bufs[0])
    for c in range(n_chunks):
        if c + 1 < n_chunks:
            scores(c + 1, bufs[(c + 1) % 2])
        consume(c, bufs[c % 2])
    for h in range(N_KV):
        acc = acc_ref[h]
        _store_heads(o_ref, h, acc[:HEAD_DIM] / acc[HEAD_DIM:HEAD_DIM + 1], tq)


def _attn_band_kernel(sink_ref, q_ref, k_ref, v_ref, kc_ref, vc_ref, *rest, tq, n_tiles):
    bias_refs, (o_ref, sb_ref, sc_ref) = rest[:n_tiles], rest[n_tiles:]
    n_lat = k_ref.shape[0]
    span = tq + 2 * WINDOW
    k_ctx, v_ctx = kc_ref[0], vc_ref[0]

    def tile_start(t):
        i = pl.program_id(1) * n_tiles + t
        return i, pl.multiple_of(jnp.clip(i * tq - WINDOW, 0, n_lat - span), WINDOW)

    def scores(t, h, slot):
        i, start = tile_start(t)
        qs = _stack_group_queries(q_ref.at[t * tq:(t + 1) * tq], h, tq)
        s_band = lax.dot_general(k_ref[pl.ds(start, span), :], qs, NT_DIMS, preferred_element_type=F32)
        sb_ref[slot] = s_band + bias_refs[t][0]
        sc_ref[slot] = lax.dot_general(k_ctx, qs, NT_DIMS, preferred_element_type=F32)

    def consume(t, h, slot):
        _, start = tile_start(t)
        s_band, s_ctx = sb_ref[slot], sc_ref[slot]
        sink = _sink_lanes(sink_ref, h, tq)
        m = jnp.maximum(jnp.maximum(jnp.max(s_band, axis=0, keepdims=True), jnp.max(s_ctx, axis=0, keepdims=True)), sink)
        p_band = jnp.exp2(s_band - m)
        p_ctx = jnp.exp2(s_ctx - m)
        den = jnp.sum(p_band, axis=0, keepdims=True) + jnp.sum(p_ctx, axis=0, keepdims=True) + jnp.exp2(sink - m)
        acc = (lax.dot_general(v_ref[pl.ds(start, span), :], p_band.astype(BF16), TN_DIMS, preferred_element_type=F32)
               + lax.dot_general(v_ctx, p_ctx.astype(BF16), TN_DIMS, preferred_element_type=F32))
        o_t = acc[h * HEAD_DIM:(h + 1) * HEAD_DIM] / den
        for j in range(GRP):
            head = h * GRP + j
            o_ref[0, head * HEAD_DIM:(head + 1) * HEAD_DIM, t * tq:(t + 1) * tq] = o_t[:, j * tq:(j + 1) * tq].astype(o_ref.dtype)

    items = [(t, h) for t in range(n_tiles) for h in range(N_KV)]
    scores(*items[0], 0)
    for n, item in enumerate(items):
        if n + 1 < len(items):
            scores(*items[n + 1], (n + 1) % 2)
        consume(*item, n % 2)


def _attention_full(q, k, vt, sink, nb, seq_len, tq, tk, cache=None):
    tiles = seq_len // tq
    q_spec = pl.BlockSpec((tq, D_ATT), lambda b, i: (b * tiles + i, 0))
    in_specs = [q_spec, pl.BlockSpec((seq_len, D_KV), lambda b, i: (b, 0)), pl.BlockSpec((D_KV, seq_len), lambda b, i: (0, b))]
    args = [q, k, vt]
    if cache is not None:
        n_c = cache[0].shape[1]
        in_specs += [pl.BlockSpec((1, n_c, D_KV), lambda b, i: (b, 0, 0)), pl.BlockSpec((1, D_KV, n_c), lambda b, i: (b, 0, 0))]
        args += list(cache)
    if sink is not None:
        in_specs, args = [pl.BlockSpec(memory_space=pltpu.SMEM)] + in_specs, [sink] + args
    rows = GRP * tq
    out_spec, out_shape = _transposed_out(nb, seq_len, tq)
    return _tokens_major(pl.pallas_call(
        functools.partial(_attn_full_kernel, tq=tq, tk=tk, use_sink=sink is not None, has_cache=cache is not None),
        grid=(nb, tiles),
        in_specs=in_specs,
        out_specs=out_spec,
        out_shape=out_shape,
        scratch_shapes=[pltpu.VMEM((N_KV, 1, rows), F32), pltpu.VMEM((N_KV, V_ROWS, rows), F32),
                        pltpu.VMEM((N_KV, tk, rows), F32), pltpu.VMEM((N_KV, tk, rows), F32)],
        compiler_params=_cparams(2),
        name="attention_full",
    )(*args))


def _attention_band(q, k, v, sink, nb, seq_len, tq, cache, n_tiles=4):
    steps = seq_len // (tq * n_tiles)
    tiles = seq_len // tq
    n_c = cache[0].shape[1]
    span = tq + 2 * WINDOW
    key_row = np.arange(span)[:, None]
    query = np.arange(GRP * tq)[None, :] % tq
    bias = jnp.asarray(np.stack([np.where(np.abs(key_row - lead - query) <= WINDOW, 0.0, NEG_INF)
                                 for lead in (0, WINDOW, 2 * WINDOW)]), dtype=F32)

    def bias_spec(t):
        def index(b, i):
            tile = i * n_tiles + t
            return (jnp.where(tile == 0, 0, jnp.where(tile == tiles - 1, 2, 1)), 0, 0)
        return pl.BlockSpec((1, span, GRP * tq), index)
    q_spec = pl.BlockSpec((n_tiles * tq, D_ATT), lambda b, i: (b * steps + i, 0))
    kv_spec = pl.BlockSpec((seq_len, D_KV), lambda b, i: (b, 0))
    c_spec = pl.BlockSpec((1, n_c, D_KV), lambda b, i: (b, 0, 0))
    out_spec, out_shape = _transposed_out(nb, seq_len, n_tiles * tq)
    return _tokens_major(pl.pallas_call(
        functools.partial(_attn_band_kernel, tq=tq, n_tiles=n_tiles),
        grid=(nb, steps),
        in_specs=[pl.BlockSpec(memory_space=pltpu.SMEM), q_spec, kv_spec, kv_spec, c_spec, c_spec]
                 + [bias_spec(t) for t in range(n_tiles)],
        out_specs=out_spec,
        out_shape=out_shape,
        scratch_shapes=[pltpu.VMEM((2, tq + 2 * WINDOW, GRP * tq), F32), pltpu.VMEM((2, n_c, GRP * tq), F32)],
        compiler_params=_cparams(2),
        name="attention_band",
    )(sink, q, k, v, *cache, *([bias] * n_tiles)))


def _layer_norm(z, g, b):
    mu = jnp.mean(z, axis=-1, keepdims=True)
    zc = z - mu
    var = jnp.mean(zc * zc, axis=-1, keepdims=True)
    return zc * lax.rsqrt(var + LN_EPS) * g + b


def _merge_mlp_kernel(x_ref, g1_ref, sh2_ref, sc2_ref, g2_ref, ya_ref, yw_ref, yg_ref, gate_ref,
                      wglu_ref, wa_ref, ww_ref, wg_ref, wout_ref, ln1g_ref, ln1b_ref,
                      wup_ref, wdn_ref, ln2g_ref, ln2b_ref, o_ref, ya_scr):
    ya = _chunk_rows_to_tokens(ya_ref, ya_scr)
    ya = ya * jax.nn.sigmoid(_dot(ya.astype(BF16), wglu_ref[...]))

    def gate(c):
        return gate_ref[:, c * D_MODEL:(c + 1) * D_MODEL].astype(F32)

    m = (gate(0) * _dot(ya.astype(BF16), wa_ref[...])
         + gate(1) * _dot(yw_ref[...], ww_ref[...])
         + gate(2) * _dot(yg_ref[...], wg_ref[...]))
    f = _dot(m.astype(BF16), wout_ref[...])
    x1 = _layer_norm(DEEPNORM_ALPHA * x_ref[...] + g1_ref[0] * f, ln1g_ref[...], ln1b_ref[...])

    h = (x1 * (1.0 + sc2_ref[0]) + sh2_ref[0]).astype(BF16)
    acc = jnp.zeros(x1.shape, F32)
    for c in range(D_FF // FF_TILE):
        up = jnp.maximum(_dot(h, wup_ref[:, c * FF_TILE:(c + 1) * FF_TILE]), 0.0)
        acc = acc + _dot((up * up).astype(BF16), wdn_ref[c * FF_TILE:(c + 1) * FF_TILE, :])
    o_ref[...] = _layer_norm(DEEPNORM_ALPHA * x1 + g2_ref[0] * acc, ln2g_ref[...], ln2b_ref[...])


def _merge_mlp(x, mod_l, row_of_tile, ya_rows, yw, yg, gates, lw):
    n_tok = x.shape[0]
    tm = ROW_TILE
    row = lambda w: pl.BlockSpec((tm, w), lambda i: (i, 0))
    vec = _resident((1, D_MODEL))
    return pl.pallas_call(
        _merge_mlp_kernel,
        grid=(n_tok // tm,),
        in_specs=[row(D_MODEL)] + [_mod_spec(c, row_of_tile) for c in (2, 3, 4, 5)]
                 + [pl.BlockSpec((S5_NPAIR, CHUNKS_PER_TILE, S5_ROW), lambda i: (0, i, 0)),
                    row(D_ATT), row(D_ATT), row(3 * D_MODEL),
                    _resident((D_SSM, D_SSM)), _resident((D_SSM, D_MODEL)), _resident((D_ATT, D_MODEL)),
                    _resident((D_ATT, D_MODEL)), _resident((D_MODEL, D_MODEL)), vec, vec,
                    _resident((D_MODEL, D_FF)), _resident((D_FF, D_MODEL)), vec, vec],
        out_specs=row(D_MODEL),
        out_shape=jax.ShapeDtypeStruct((n_tok, D_MODEL), F32),
        scratch_shapes=[pltpu.VMEM((LANE_SLABS, tm, LANES), F32)],
        compiler_params=_cparams(1),
        name="merge_mlp_residual",
    )(x, mod_l, mod_l, mod_l, mod_l, ya_rows, yw, yg, gates,
      lw['w_glu'], lw['w_br_ssm'], lw['w_br_win'], lw['w_br_glb'], lw['w_out'], lw['ln1_g'], lw['ln1_b'],
      lw['w_up'], lw['w_down'], lw['ln2_g'], lw['ln2_b'])


def _rope_tables(n_tok):
    rows = n_tok // GRID_W
    row = jnp.repeat(jnp.arange(rows, dtype=F32), GRID_W)
    col = jnp.tile(jnp.arange(GRID_W, dtype=F32), rows)
    n_freq = HEAD_DIM // 4
    inv = ROPE_BASE ** (-jnp.arange(n_freq, dtype=F32) / n_freq)
    ang = jnp.concatenate([row[:, None] * inv, col[:, None] * inv], axis=-1)
    cos, sin = jnp.cos(ang), jnp.sin(ang)
    cos_t = jnp.tile(jnp.concatenate([cos, cos], axis=-1), (1, LANES // HEAD_DIM))
    sin_t = jnp.tile(jnp.concatenate([-sin, sin], axis=-1), (1, LANES // HEAD_DIM))
    return cos_t, sin_t


def _block_diag_ones(n):
    idx = np.arange(n) // HEAD_DIM
    return jnp.asarray(idx[:, None] == idx[None, :], dtype=BF16)


def _permute_q_columns(w):
    def perm(block):
        return block.reshape(D_MODEL, N_HEADS, HEAD_DIM)[:, np.array(Q_SLOT_ORDER), :].reshape(D_MODEL, D_ATT)
    return jnp.concatenate([w[:, :O_QW], perm(w[:, O_QW:O_KW]), w[:, O_KW:O_QG], perm(w[:, O_QG:O_KG]), w[:, O_KG:]], axis=1)


def _layer(x, lw, mod_l, row_of_tile, nb, seq_len, ctx, rope_tabs):
    latent = ctx is not None
    u_rows, qw, qg, gates, kw, kg, vg_t, vw, *kv_f32 = _in_projection(
        x, mod_l, row_of_tile, lw['w_in'], lw['qn'], lw['kn'], lw['bdq'], lw['bdk'], rope_tabs, seq_len,
        vw_transposed=not latent, keep_f32_kv=not latent)
    if latent:
        s0 = [ctx[0][:, d, part].reshape(nb, N_STATE) for d in (0, 1) for part in (0, 1)]
    else:
        s0 = [jnp.zeros((nb, N_STATE), F32)] * 4
    ya_rows, s_fin = _s5_branch(u_rows, lw['s5_ops'], lw['d_skip'], s0, nb, seq_len)

    if latent:
        n_ctx = ctx[1].shape[1]
        k_wc, v_wc, k_gc, v_gc = [t.reshape(nb, n_ctx, D_KV).astype(BF16) for t in ctx[1:]]
        yw = _attention_band(qw, kw, vw, lw['sink'], nb, seq_len, 256, (k_wc, v_wc))
        yg = _attention_full(qg, kg, vg_t, None, nb, seq_len, 256, 256, (k_gc, jnp.swapaxes(v_gc, 1, 2)))
        new_ctx = None
    else:
        yw = _attention_full(qw, kw, vw, lw['sink'], nb, seq_len, seq_len, seq_len)
        yg = _attention_full(qg, kg, vg_t, None, nb, seq_len, seq_len, seq_len)
        new_ctx = (s_fin,) + tuple(kv_f32)
    x2 = _merge_mlp(x, mod_l, row_of_tile, ya_rows, yw, yg, gates, lw)
    return x2, new_ctx


def kernel(x_prompt, x_sample, state_ssm, cache_k_win, cache_v_win, cache_k_glb, cache_v_glb, c, c_ctx, w_mod, b_mod, w_in, ssm_lam_re, ssm_lam_im, ssm_log_step, ssm_b_re, ssm_b_im, ssm_c_re, ssm_c_im, ssm_d, w_glu, sink_win, q_norm_glb, k_norm_glb, w_br_ssm, w_br_win, w_br_glb, w_out, ln1_g, ln1_b, w_up, w_down, ln2_g, ln2_b):
    n_ctx_b, ctx_len, _ = x_prompt.shape
    n_lat_b, lat_len, _ = x_sample.shape
    assert ctx_len % ROW_TILE == 0 or ROW_TILE % ctx_len == 0
    assert lat_len % ROW_TILE == 0 and (n_ctx_b * ctx_len) % ROW_TILE == 0

    cond8 = jnp.zeros((8, D_MODEL), F32).at[0].set(c_ctx).at[1:1 + n_lat_b].set(c)
    mod = _modulation(cond8, w_mod, b_mod).reshape(DEPTH, 8, 1, 6 * D_MODEL)
    rope_tabs = _rope_tables(lat_len)
    bdq, bdk = _block_diag_ones(D_ATT), _block_diag_ones(D_KV)
    lat_tiles = lat_len // ROW_TILE
    ctx_row = lambda i: 0
    lat_row = lambda i: 1 + i // lat_tiles

    xp = x_prompt.reshape(n_ctx_b * ctx_len, D_MODEL)
    xs = x_sample.reshape(n_lat_b * lat_len, D_MODEL)
    new_ssm, new_kw, new_vw, new_kg, new_vg = [], [], [], [], []
    for l in range(DEPTH):
        lw = dict(
            w_in=_permute_q_columns(w_in[l]).astype(BF16),
            qn=jnp.tile(q_norm_glb[l], N_HEADS).reshape(1, D_ATT), kn=jnp.tile(k_norm_glb[l], N_KV).reshape(1, D_KV),
            bdq=bdq, bdk=bdk,
            s5_ops=_s5_operators(ssm_lam_re[l], ssm_lam_im[l], ssm_log_step[l],
                                 ssm_b_re[l], ssm_b_im[l], ssm_c_re[l], ssm_c_im[l]),
            d_skip=ssm_d[l],
            sink=sink_win[l],
            w_glu=w_glu[l].astype(BF16), w_br_ssm=w_br_ssm[l].astype(BF16), w_br_win=w_br_win[l].astype(BF16),
            w_br_glb=w_br_glb[l].astype(BF16), w_out=w_out[l].astype(BF16),
            ln1_g=ln1_g[l].reshape(1, D_MODEL), ln1_b=ln1_b[l].reshape(1, D_MODEL),
            w_up=w_up[l].astype(BF16), w_down=w_down[l].astype(BF16),
            ln2_g=ln2_g[l].reshape(1, D_MODEL), ln2_b=ln2_b[l].reshape(1, D_MODEL),
        )
        xp, (s_fin, kw, vw, kg, vg) = _layer(xp, lw, mod[l], ctx_row, n_ctx_b, ctx_len, None, None)
        new_ssm.append(jnp.stack(s_fin, axis=1).reshape(n_ctx_b, 2, 2, SSM_GROUPS, SSM_STATE))
        for acc, t in ((new_kw, kw), (new_vw, vw), (new_kg, kg), (new_vg, vg)):
            acc.append(t.reshape(n_ctx_b, ctx_len, N_KV, HEAD_DIM))
        ctx = (state_ssm[:, l], cache_k_win[:, l], cache_v_win[:, l], cache_k_glb[:, l], cache_v_glb[:, l])
        xs, _ = _layer(xs, lw, mod[l], lat_row, n_lat_b, lat_len, ctx, rope_tabs)
    return (xp.reshape(x_prompt.shape), xs.reshape(x_sample.shape),
            jnp.stack(new_ssm, axis=1), jnp.stack(new_kw, axis=1), jnp.stack(new_vw, axis=1),
            jnp.stack(new_kg, axis=1), jnp.stack(new_vg, axis=1))
```

```python
import functools

import jax
import jax.numpy as jnp
import numpy as np
from jax import lax
from jax.experimental import pallas as pl
from jax.experimental.pallas import tpu as pltpu

F32 = jnp.float32
BF16 = jnp.bfloat16

D_MODEL = 1024
DEPTH = 2
GRID_W = 64
HEAD_DIM = 64
D_SSM = 512
SSM_GROUP_CH = 16
SSM_GROUPS = 32
SSM_STATE = 64
N_HEADS = 8
N_KV = 2
GRP = N_HEADS // N_KV
D_ATT = N_HEADS * HEAD_DIM
D_KV = N_KV * HEAD_DIM
WINDOW = 128
ROPE_BASE = 10000.0
D_FF = 4 * D_MODEL
LN_EPS = 1e-5
RMS_EPS = 1e-6
ATTN_SCALE = HEAD_DIM ** -0.5
DEEPNORM_ALPHA = (2.0 * DEPTH) ** 0.25
NEG_INF = -1e30
LOG2E = 1.4426950408889634
Q_SCALE = ATTN_SCALE * LOG2E
V_ROWS = HEAD_DIM + 16
N_IN = D_SSM + 2 * (D_ATT + 2 * D_KV) + 3 * D_MODEL
O_U = 0
O_QW = O_U + D_SSM
O_KW = O_QW + D_ATT
O_VW = O_KW + D_KV
O_QG = O_VW + D_KV
O_KG = O_QG + D_ATT
O_VG = O_KG + D_KV
O_GATE = O_VG + D_KV

S5_CHUNK = 16
S5_PAIR = 2 * SSM_GROUP_CH
S5_NPAIR = SSM_GROUPS // 2
S5_PAIR_BLOCK = 4
S5_ROW = S5_CHUNK * S5_PAIR
S5_PSTATE = 2 * SSM_STATE
N_STATE = SSM_GROUPS * SSM_STATE

LANES = 128
ROW_TILE = 512
FF_TILE = 1024
CHUNKS_PER_TILE = ROW_TILE // S5_CHUNK
LANE_SLABS = D_SSM // LANES
PAIRS_PER_SLAB = LANES // S5_PAIR
VMEM_LIMIT = 56 * 1024 * 1024
NT_DIMS = (((1,), (1,)), ((), ()))
TN_DIMS = (((0,), (0,)), ((), ()))


def _cparams(n_axes):
    return pltpu.CompilerParams(dimension_semantics=("arbitrary",) * n_axes, vmem_limit_bytes=VMEM_LIMIT)


def _resident(shape):
    nd = len(shape)
    return pl.BlockSpec(shape, lambda *_: (0,) * nd, pipeline_mode=pl.Buffered(1))


def _dot(a, b):
    return jnp.dot(a, b, preferred_element_type=F32)


def _mod_kernel(c_ref, w_ref, b_ref, o_ref):
    c = c_ref[...]
    a = (c * jax.nn.sigmoid(c)).astype(BF16)
    o_ref[0] = _dot(a, w_ref[0].astype(BF16)) + b_ref[0]


def _modulation(cond8, w_mod, b_mod):
    tn = 1536
    n_out = w_mod.shape[-1]
    return pl.pallas_call(
        _mod_kernel,
        grid=(DEPTH, n_out // tn),
        in_specs=[
            pl.BlockSpec((8, D_MODEL), lambda l, n: (0, 0)),
            pl.BlockSpec((1, D_MODEL, tn), lambda l, n: (l, 0, n)),
            pl.BlockSpec((1, 1, tn), lambda l, n: (l, 0, n)),
        ],
        out_specs=pl.BlockSpec((1, 8, tn), lambda l, n: (l, 0, n)),
        out_shape=jax.ShapeDtypeStruct((DEPTH, 8, n_out), F32),
        compiler_params=_cparams(2),
        name="modulation",
    )(cond8, w_mod, b_mod.reshape(DEPTH, 1, n_out))


def _mod_spec(chunk, row_of_tile):
    return pl.BlockSpec((1, 1, D_MODEL), lambda i: (row_of_tile(i), 0, chunk))


def _tokens_to_chunk_rows(u, scr, out_ref):
    n_chunks = u.shape[0] // S5_CHUNK
    for j in range(LANE_SLABS):
        scr[j] = u[:, j * LANES:(j + 1) * LANES]
    pieces = [[None] * S5_CHUNK for _ in range(S5_NPAIR)]
    for j in range(LANE_SLABS):
        for t in range(S5_CHUNK):
            step_rows = scr[j, pl.ds(t, n_chunks, stride=S5_CHUNK), :]
            for m in range(PAIRS_PER_SLAB):
                pieces[j * PAIRS_PER_SLAB + m][t] = step_rows[:, m * S5_PAIR:(m + 1) * S5_PAIR]
    for n in range(S5_NPAIR):
        out_ref[n] = jnp.concatenate(pieces[n], axis=1)


def _chunk_rows_to_tokens(y_ref, scr):
    n_chunks = y_ref.shape[1]
    for j in range(LANE_SLABS):
        for t in range(S5_CHUNK):
            piece = jnp.concatenate(
                [y_ref[j * PAIRS_PER_SLAB + m, :, t * S5_PAIR:(t + 1) * S5_PAIR] for m in range(PAIRS_PER_SLAB)], axis=1)
            scr[j, pl.ds(t, n_chunks, stride=S5_CHUNK), :] = piece
    return jnp.concatenate([scr[j] for j in range(LANE_SLABS)], axis=1)


def _head_rms(x, gain, ones_bd):
    ss = _dot((x * x).astype(BF16), ones_bd)
    return x * lax.rsqrt(ss * (1.0 / HEAD_DIM) + RMS_EPS) * gain


def _rope_chunk(xc, cos_t, sin_t):
    lane = lax.broadcasted_iota(jnp.int32, xc.shape, 1)
    first_half = (lane & (HEAD_DIM - 1)) < (HEAD_DIM // 2)
    partner = jnp.where(first_half, pltpu.roll(xc, LANES - HEAD_DIM // 2, 1), pltpu.roll(xc, HEAD_DIM // 2, 1))
    return xc * cos_t + partner * sin_t


def _inproj_kernel(*refs, rope, vw_transposed, keep_f32_kv):
    n_in = 10 if rope else 8
    x_ref, sh_ref, sc_ref, w_ref, qn_ref, kn_ref, bdq_ref, bdk_ref = refs[:8]
    u_ref, qw_ref, qg_ref, gate_ref, kw_ref, kg_ref, vgt_ref, vw_ref = refs[n_in:n_in + 8]
    f32_refs, u_scr = refs[n_in + 8:-1], refs[-1]
    h = (x_ref[...] * (1.0 + sc_ref[0]) + sh_ref[0]).astype(BF16)

    def proj(lo, width):
        return _dot(h, w_ref[:, lo:lo + width])

    def rotate(x):
        if not rope:
            return x
        chunks = [_rope_chunk(x[:, c * LANES:(c + 1) * LANES], refs[8][...], refs[9][...]) for c in range(x.shape[-1] // LANES)]
        return chunks[0] if len(chunks) == 1 else jnp.concatenate(chunks, axis=1)

    _tokens_to_chunk_rows(proj(O_U, D_SSM), u_scr, u_ref)
    qw_ref[...] = (rotate(proj(O_QW, D_ATT)) * Q_SCALE).astype(BF16)
    qg_ref[...] = (rotate(_head_rms(proj(O_QG, D_ATT), qn_ref[...], bdq_ref[...])) * Q_SCALE).astype(BF16)
    kw = rotate(proj(O_KW, D_KV))
    kg = rotate(_head_rms(proj(O_KG, D_KV), kn_ref[...], bdk_ref[...]))
    vw, vg = proj(O_VW, D_KV), proj(O_VG, D_KV)
    kw_ref[...] = kw.astype(BF16)
    kg_ref[...] = kg.astype(BF16)
    vgt_ref[...] = vg.T.astype(BF16)
    vw_ref[...] = (vw.T if vw_transposed else vw).astype(BF16)
    if keep_f32_kv:
        for out_ref, val in zip(f32_refs, (kw, vw, kg, vg)):
            out_ref[...] = val
    for c in range(3):
        gate = jax.nn.sigmoid(proj(O_GATE + c * D_MODEL, D_MODEL))
        gate_ref[:, c * D_MODEL:(c + 1) * D_MODEL] = gate.astype(gate_ref.dtype)


def _in_projection(x, mod_l, row_of_tile, w_in, qn, kn, bdq, bdk, rope_tabs, seq_len, vw_transposed, keep_f32_kv):
    n_tok = x.shape[0]
    tm = ROW_TILE
    rope = rope_tabs is not None
    row = lambda w: pl.BlockSpec((tm, w), lambda i: (i, 0))
    col = pl.BlockSpec((D_KV, tm), lambda i: (0, i))
    in_specs = [
        row(D_MODEL),
        _mod_spec(0, row_of_tile), _mod_spec(1, row_of_tile),
        _resident((D_MODEL, N_IN)),
        _resident((1, D_ATT)), _resident((1, D_KV)),
        _resident((D_ATT, D_ATT)), _resident((D_KV, D_KV)),
    ]
    args = [x, mod_l, mod_l, w_in, qn, kn, bdq, bdk]
    if rope:
        tiles_per_seq = seq_len // tm
        tab = pl.BlockSpec((tm, LANES), lambda i: (i % tiles_per_seq, 0))
        in_specs += [tab, tab]
        args += list(rope_tabs)
    tok = lambda w, dt: (jax.ShapeDtypeStruct((n_tok, w), dt), row(w))
    tr = (jax.ShapeDtypeStruct((D_KV, n_tok), BF16), col)
    outs = [(jax.ShapeDtypeStruct((S5_NPAIR, n_tok // S5_CHUNK, S5_ROW), F32),
             pl.BlockSpec((S5_NPAIR, CHUNKS_PER_TILE, S5_ROW), lambda i: (0, i, 0))),
            tok(D_ATT, BF16), tok(D_ATT, BF16), tok(3 * D_MODEL, BF16),
            tok(D_KV, BF16), tok(D_KV, BF16), tr, tr if vw_transposed else tok(D_KV, BF16)]
    if keep_f32_kv:
        outs += [tok(D_KV, F32)] * 4
    return pl.pallas_call(
        functools.partial(_inproj_kernel, rope=rope, vw_transposed=vw_transposed, keep_f32_kv=keep_f32_kv),
        grid=(n_tok // tm,),
        in_specs=in_specs,
        out_specs=[o[1] for o in outs],
        out_shape=[o[0] for o in outs],
        scratch_shapes=[pltpu.VMEM((LANE_SLABS, tm, LANES), F32)],
        compiler_params=_cparams(1),
        name="in_projection",
    )(*args)


def _zoh(lr, li, ls):
    dt = jnp.exp(ls)
    mag = jnp.exp(lr * dt)
    ar, ai = mag * jnp.cos(li * dt), mag * jnp.sin(li * dt)
    den = lr * lr + li * li
    fr = ((ar - 1.0) * lr + ai * li) / den
    fi = (ai * lr - (ar - 1.0) * li) / den
    return ar, ai, fr, fi


def _powers(ar, ai, n):
    out = [(jnp.ones_like(ar), jnp.zeros_like(ar))]
    for _ in range(n):
        pr, pi = out[-1]
        out.append((pr * ar - pi * ai, pr * ai + pi * ar))
    return out


def _s5_operator_kernel(lr_ref, li_ref, ls_ref, btr_ref, bti_ref, cr_ref, ci_ref,
                        top_ref, e0, e1, e2, e3, m0, m1, m2, m3, d0, d1, d2, d3, ext_scr):
    tc, cg, p = S5_CHUNK, SSM_GROUP_CH, SSM_STATE
    ends = ((e0, e1), (e2, e3))
    carries = ((m0, m1), (m2, m3))
    decays = ((d0, d1), (d2, d3))
    ext_scr[...] = jnp.zeros(ext_scr.shape, F32)
    zero_half = jnp.zeros((cg, p), F32)

    def place(out_ref, t, g2, val):
        halves = [val, zero_half] if g2 == 0 else [zero_half, val]
        out_ref[0, t * S5_PAIR + g2 * cg:t * S5_PAIR + (g2 + 1) * cg, :] = jnp.concatenate(halves, axis=1).astype(BF16)

    for d in range(2):
        for g2 in range(2):
            ar, ai, fr, fi = _zoh(lr_ref[d, 0, g2], li_ref[d, 0, g2], ls_ref[d, 0, g2])
            btr, bti = btr_ref[d, 0, g2], bti_ref[d, 0, g2]
            bbr, bbi = fr * btr - fi * bti, fr * bti + fi * btr
            cr, ci = cr_ref[d, 0, g2], ci_ref[d, 0, g2]
            pw = _powers(ar, ai, tc)
            group_rows = slice(g2 * cg, (g2 + 1) * cg)
            for j in range(tc + 1):
                pr, pi = pw[j]
                car, cai = cr * pr - ci * pi, cr * pi + ci * pr
                if j >= 1:
                    t = j - 1 if d == 0 else tc - j
                    place(carries[d][0], t, g2, car)
                    place(carries[d][1], t, g2, -cai)
                if j == tc:
                    break
                t = tc - 1 - j if d == 0 else j
                place(ends[d][0], t, g2, pr * bbr - pi * bbi)
                place(ends[d][1], t, g2, pr * bbi + pi * bbr)
                kt = (lax.dot_general(bbr, car, NT_DIMS, precision=lax.Precision.HIGHEST, preferred_element_type=F32)
                      - lax.dot_general(bbi, cai, NT_DIMS, precision=lax.Precision.HIGHEST, preferred_element_type=F32))
                slot = tc - 1 + j if d == 0 else tc - 1 - j
                lanes = slice(slot * S5_PAIR + g2 * cg, slot * S5_PAIR + (g2 + 1) * cg)
                if d == 1 and j == 0:
                    ext_scr[group_rows, lanes] = ext_scr[group_rows, lanes] + kt
                else:
                    ext_scr[group_rows, lanes] = kt
            for part in range(2):
                decays[d][part][:, g2 * p:(g2 + 1) * p] = pw[tc][part]
    for t in range(tc):
        for g2 in range(2):
            rows = slice(t * S5_PAIR + g2 * cg, t * S5_PAIR + (g2 + 1) * cg)
            window = slice((tc - 1 - t) * S5_PAIR, (tc - 1 - t) * S5_PAIR + S5_ROW)
            top_ref[0, rows, :] = ext_scr[g2 * cg:(g2 + 1) * cg, window].astype(BF16)


def _s5_operators(lam_re, lam_im, log_step, b_re, b_im, c_re, c_im):
    g, p, cg = SSM_GROUPS, SSM_STATE, SSM_GROUP_CH
    ls = jnp.broadcast_to(log_step[:, :, None], (2, g, p))

    def rows(v):
        return v.reshape(2, S5_NPAIR, 2, 1, p)

    def mat(v):
        return v.reshape(2, S5_NPAIR, 2, cg, p)

    def spec(*tail):
        return pl.BlockSpec((2, 1, 2) + tail, lambda n: (0, n, 0, 0, 0))

    state_op = pl.BlockSpec((1, S5_ROW, S5_PSTATE), lambda n: (n, 0, 0))
    out = pl.pallas_call(
        _s5_operator_kernel,
        grid=(S5_NPAIR,),
        in_specs=[spec(1, p)] * 3 + [spec(cg, p)] * 4,
        out_specs=([pl.BlockSpec((1, S5_ROW, S5_ROW), lambda n: (n, 0, 0))] + [state_op] * 8
                   + [pl.BlockSpec((1, S5_PSTATE), lambda n: (0, n))] * 4),
        out_shape=([jax.ShapeDtypeStruct((S5_NPAIR, S5_ROW, S5_ROW), BF16)]
                   + [jax.ShapeDtypeStruct((S5_NPAIR, S5_ROW, S5_PSTATE), BF16)] * 8
                   + [jax.ShapeDtypeStruct((1, N_STATE), F32)] * 4),
        scratch_shapes=[pltpu.VMEM((S5_PAIR, 2 * S5_ROW), F32)],
        compiler_params=_cparams(1),
        name="s5_operators",
    )(rows(lam_re), rows(lam_im), rows(ls),
      mat(jnp.swapaxes(b_re, -1, -2)), mat(jnp.swapaxes(b_im, -1, -2)), mat(c_re), mat(c_im))
    return out[0], out[1:5], out[5:9], out[9:13]


def _s5_kernel(u_ref, top_ref, d_ref, e0, e1, e2, e3, m0, m1, m2, m3, a0, a1, a2, a3, i0, i1, i2, i3,
               y_ref, f0, f1, f2, f3, se0, se1, se2, se3, st0, st1, st2, st3, *, nb, nc):
    ends, carries, decays, inits, finals = (e0, e1, e2, e3), (m0, m1, m2, m3), (a0, a1, a2, a3), (i0, i1, i2, i3), (f0, f1, f2, f3)
    se, st = (se0, se1, se2, se3), (st0, st1, st2, st3)
    for p in range(S5_PAIR_BLOCK):
        lanes = slice(p * S5_PSTATE, (p + 1) * S5_PSTATE)
        u = u_ref[p].astype(BF16)
        for k in range(4):
            se[k][:, lanes] = _dot(u, ends[k][p])

    a_fr, a_fi, a_br, a_bi = [a[...] for a in decays]
    for b in range(nb):
        base = b * nc

        def body(c, carry, base=base):
            fr, fi, br, bi = carry
            rc, rb = pl.ds(base + c, 1), pl.ds(base + nc - 1 - c, 1)
            st0[rc, :] = fr
            st1[rc, :] = fi
            st2[rb, :] = br
            st3[rb, :] = bi
            nfr = a_fr * fr - a_fi * fi + se0[rc, :]
            nfi = a_fr * fi + a_fi * fr + se1[rc, :]
            nbr = a_br * br - a_bi * bi + se2[rb, :]
            nbi = a_br * bi + a_bi * br + se3[rb, :]
            return nfr, nfi, nbr, nbi

        last = lax.fori_loop(0, nc, body, tuple(i_ref[b] for i_ref in inits))
        for f_ref, val in zip(finals, last):
            f_ref[b] = val

    for p in range(S5_PAIR_BLOCK):
        lanes = slice(p * S5_PSTATE, (p + 1) * S5_PSTATE)
        u = u_ref[p]
        y = _dot(u.astype(BF16), top_ref[p]) + d_ref[p] * u
        for k in range(4):
            y = y + lax.dot_general(st[k][:, lanes].astype(BF16), carries[k][p], NT_DIMS, preferred_element_type=F32)
        y_ref[p] = jax.nn.gelu(y)


def _s5_branch(u_rows, ops, d_skip, s0, nb, seq_len):
    top, ends, carries, decay = ops
    nc = seq_len // S5_CHUNK
    n_rows = nc * nb
    pg = S5_PAIR_BLOCK
    lanes = pg * S5_PSTATE
    d_rows = jnp.tile(d_skip.reshape(S5_NPAIR, 1, S5_PAIR), (1, S5_CHUNK, 1)).reshape(S5_NPAIR, 1, S5_ROW)
    rows_spec = pl.BlockSpec((pg, n_rows, S5_ROW), lambda n: (n, 0, 0))
    op_spec = pl.BlockSpec((pg, S5_ROW, S5_PSTATE), lambda n: (n, 0, 0))
    state_spec = pl.BlockSpec((nb, 1, lanes), lambda n: (0, 0, n))
    out = pl.pallas_call(
        functools.partial(_s5_kernel, nb=nb, nc=nc),
        grid=(S5_NPAIR // pg,),
        in_specs=[rows_spec, pl.BlockSpec((pg, S5_ROW, S5_ROW), lambda n: (n, 0, 0)),
                  pl.BlockSpec((pg, 1, S5_ROW), lambda n: (n, 0, 0))]
                 + [op_spec] * 8 + [pl.BlockSpec((1, lanes), lambda n: (0, n))] * 4 + [state_spec] * 4,
        out_specs=[rows_spec] + [state_spec] * 4,
        out_shape=[jax.ShapeDtypeStruct(u_rows.shape, F32)] + [jax.ShapeDtypeStruct((nb, 1, N_STATE), F32)] * 4,
        scratch_shapes=[pltpu.VMEM((n_rows, lanes), F32)] * 8,
        compiler_params=_cparams(1),
        name="s5_chunked",
    )(u_rows, top, d_rows, *ends, *carries, *decay, *[s.reshape(nb, 1, N_STATE) for s in s0])
    return out[0], [s.reshape(nb, N_STATE) for s in out[1:]]


Q_SLOT_ORDER = tuple(h for j in range(GRP) for h in (j, GRP + j))


def _stack_group_queries(q_ref, h, tq):
    lane = lax.broadcasted_iota(jnp.int32, (tq, LANES), 1)
    keep = (lane >= h * HEAD_DIM) & (lane < (h + 1) * HEAD_DIM)
    zero = jnp.zeros((tq, LANES), BF16)
    return jnp.concatenate([jnp.where(keep, q_ref[:, j * LANES:(j + 1) * LANES], zero) for j in range(GRP)], axis=0)


def _sink_lanes(sink_ref, h, tq):
    return jnp.concatenate([jnp.full((1, tq), sink_ref[h * GRP + j] * LOG2E, F32) for j in range(GRP)], axis=1)


def _values_with_ones(vt):
    return jnp.concatenate([vt, jnp.ones((V_ROWS - HEAD_DIM, vt.shape[1]), BF16)], axis=0)


def _store_heads(o_ref, h, o_t, tq):
    for j in range(GRP):
        head = h * GRP + j
        o_ref[head * HEAD_DIM:(head + 1) * HEAD_DIM, :] = o_t[:, j * tq:(j + 1) * tq].astype(o_ref.dtype)


def _transposed_out(nb, seq_len, tq, n_seq=1):
    return (pl.BlockSpec((n_seq, D_ATT, tq), lambda b, i: (b, 0, i)), jax.ShapeDtypeStruct((nb, D_ATT, seq_len), BF16))


def _attn_full_kernel(*refs, tq, tk, n_seq, use_sink, has_cache):
    refs = list(refs)
    sink_ref = refs.pop(0) if use_sink else None
    q_ref, k_ref, vt_ref = refs[:3]
    kc_ref, vtc_ref = refs[3:5] if has_cache else (None, None)
    o_ref, m_ref, acc_ref, sa_ref, sb_ref = refs[-5:]
    seq_len = k_ref.shape[0] // n_seq
    n_own = seq_len // tk
    n_chunks = n_own + (kc_ref.shape[1] // tk if has_cache else 0)
    rows = GRP * tq
    qs = [[_stack_group_queries(q_ref.at[s * tq:(s + 1) * tq], h, tq) for h in range(N_KV)] for s in range(n_seq)]
    for u in range(n_seq * N_KV):
        if use_sink:
            m_ref[u] = _sink_lanes(sink_ref, u % N_KV, tq)
            acc_ref[u] = jnp.concatenate([jnp.zeros((HEAD_DIM, rows), F32), jnp.ones((V_ROWS - HEAD_DIM, rows), F32)], axis=0)
        else:
            m_ref[u] = jnp.full((1, rows), NEG_INF, F32)
            acc_ref[u] = jnp.zeros((V_ROWS, rows), F32)

    def keys(s, c):
        if c < n_own:
            return k_ref[s * seq_len + c * tk:s * seq_len + (c + 1) * tk, :]
        return kc_ref[0, (c - n_own) * tk:(c - n_own + 1) * tk, :]

    def values_t(s, c, h):
        hd = slice(h * HEAD_DIM, (h + 1) * HEAD_DIM)
        if c < n_own:
            return vt_ref[hd, s * seq_len + c * tk:s * seq_len + (c + 1) * tk]
        return vtc_ref[0, hd, (c - n_own) * tk:(c - n_own + 1) * tk]

    def scores(s, c, s_ref):
        kc = keys(s, c)
        for h in range(N_KV):
            s_ref[h] = lax.dot_general(kc, qs[s][h], NT_DIMS, preferred_element_type=F32)

    def consume(s, c, s_ref):
        for h in range(N_KV):
            u = s * N_KV + h
            sc = s_ref[h]
            m_old = m_ref[u]
            m_new = jnp.maximum(m_old, jnp.max(sc, axis=0, keepdims=True))
            p = jnp.exp2(sc - m_new).astype(BF16)
            alpha = jnp.exp2(m_old - m_new)
            acc_ref[u] = alpha * acc_ref[u] + _dot(_values_with_ones(values_t(s, c, h)), p)
            m_ref[u] = m_new

    items = [(s, c) for s in range(n_seq) for c in range(n_chunks)]
    bufs = (sa_ref, sb_ref)
    scores(*items[0], bufs[0])
    for n, item in enumerate(items):
        if n + 1 < len(items):
            scores(*items[n + 1], bufs[(n + 1) % 2])
        consume(*item, bufs[n % 2])
    for s in range(n_seq):
        for h in range(N_KV):
            acc = acc_ref[s * N_KV + h]
            _store_heads(o_ref.at[s], h, acc[:HEAD_DIM] / acc[HEAD_DIM:HEAD_DIM + 1], tq)


def _attn_band_kernel(sink_ref, q_ref, k_ref, v_ref, kc_ref, vc_ref, *rest, tq, n_tiles):
    bias_refs, (o_ref, sb_ref, sc_ref) = rest[:n_tiles], rest[n_tiles:]
    n_lat = k_ref.shape[0]
    span = tq + 2 * WINDOW
    k_ctx, v_ctx = kc_ref[0], vc_ref[0]

    def tile_start(t):
        i = pl.program_id(1) * n_tiles + t
        return i, pl.multiple_of(jnp.clip(i * tq - WINDOW, 0, n_lat - span), WINDOW)

    def scores(t, h, slot):
        i, start = tile_start(t)
        qs = _stack_group_queries(q_ref.at[t * tq:(t + 1) * tq], h, tq)
        s_band = lax.dot_general(k_ref[pl.ds(start, span), :], qs, NT_DIMS, preferred_element_type=F32)
        sb_ref[slot] = s_band + bias_refs[t][0]
        sc_ref[slot] = lax.dot_general(k_ctx, qs, NT_DIMS, preferred_element_type=F32)

    def consume(t, h, slot):
        _, start = tile_start(t)
        s_band, s_ctx = sb_ref[slot], sc_ref[slot]
        sink = _sink_lanes(sink_ref, h, tq)
        m = jnp.maximum(jnp.maximum(jnp.max(s_band, axis=0, keepdims=True), jnp.max(s_ctx, axis=0, keepdims=True)), sink)
        p_band = jnp.exp2(s_band - m)
        p_ctx = jnp.exp2(s_ctx - m)
        den = jnp.sum(p_band, axis=0, keepdims=True) + jnp.sum(p_ctx, axis=0, keepdims=True) + jnp.exp2(sink - m)
        acc = (lax.dot_general(v_ref[pl.ds(start, span), :], p_band.astype(BF16), TN_DIMS, preferred_element_type=F32)
               + lax.dot_general(v_ctx, p_ctx.astype(BF16), TN_DIMS, preferred_element_type=F32))
        o_t = acc[h * HEAD_DIM:(h + 1) * HEAD_DIM] / den
        for j in range(GRP):
            head = h * GRP + j
            o_ref[0, head * HEAD_DIM:(head + 1) * HEAD_DIM, t * tq:(t + 1) * tq] = o_t[:, j * tq:(j + 1) * tq].astype(o_ref.dtype)

    items = [(t, h) for t in range(n_tiles) for h in range(N_KV)]
    scores(*items[0], 0)
    for n, item in enumerate(items):
        if n + 1 < len(items):
            scores(*items[n + 1], (n + 1) % 2)
        consume(*item, n % 2)


def _attention_full(q, k, vt, sink, nb, seq_len, tq, tk, cache=None, n_seq=1):
    assert n_seq == 1 or (tq == seq_len and cache is None)
    tiles = seq_len // tq
    q_spec = pl.BlockSpec((n_seq * tq, D_ATT), lambda b, i: (b * tiles + i, 0))
    in_specs = [q_spec, pl.BlockSpec((n_seq * seq_len, D_KV), lambda b, i: (b, 0)),
                pl.BlockSpec((D_KV, n_seq * seq_len), lambda b, i: (0, b))]
    args = [q, k, vt]
    if cache is not None:
        n_c = cache[0].shape[1]
        in_specs += [pl.BlockSpec((1, n_c, D_KV), lambda b, i: (b, 0, 0)), pl.BlockSpec((1, D_KV, n_c), lambda b, i: (b, 0, 0))]
        args += list(cache)
    if sink is not None:
        in_specs, args = [pl.BlockSpec(memory_space=pltpu.SMEM)] + in_specs, [sink] + args
    rows = GRP * tq
    out_spec, out_shape = _transposed_out(nb, seq_len, tq, n_seq)
    return pl.pallas_call(
        functools.partial(_attn_full_kernel, tq=tq, tk=tk, n_seq=n_seq, use_sink=sink is not None, has_cache=cache is not None),
        grid=(nb // n_seq, tiles),
        in_specs=in_specs,
        out_specs=out_spec,
        out_shape=out_shape,
        scratch_shapes=[pltpu.VMEM((n_seq * N_KV, 1, rows), F32), pltpu.VMEM((n_seq * N_KV, V_ROWS, rows), F32),
                        pltpu.VMEM((N_KV, tk, rows), F32), pltpu.VMEM((N_KV, tk, rows), F32)],
        compiler_params=_cparams(2),
        name="attention_full",
    )(*args)


def _attention_band(q, k, v, sink, nb, seq_len, tq, cache, n_tiles=4):
    steps = seq_len // (tq * n_tiles)
    tiles = seq_len // tq
    n_c = cache[0].shape[1]
    span = tq + 2 * WINDOW
    key_row = np.arange(span)[:, None]
    query = np.arange(GRP * tq)[None, :] % tq
    bias = jnp.asarray(np.stack([np.where(np.abs(key_row - lead - query) <= WINDOW, 0.0, NEG_INF)
                                 for lead in (0, WINDOW, 2 * WINDOW)]), dtype=F32)

    def bias_spec(t):
        def index(b, i):
            tile = i * n_tiles + t
            return (jnp.where(tile == 0, 0, jnp.where(tile == tiles - 1, 2, 1)), 0, 0)
        return pl.BlockSpec((1, span, GRP * tq), index)
    q_spec = pl.BlockSpec((n_tiles * tq, D_ATT), lambda b, i: (b * steps + i, 0))
    kv_spec = pl.BlockSpec((seq_len, D_KV), lambda b, i: (b, 0))
    c_spec = pl.BlockSpec((1, n_c, D_KV), lambda b, i: (b, 0, 0))
    out_spec, out_shape = _transposed_out(nb, seq_len, n_tiles * tq)
    return pl.pallas_call(
        functools.partial(_attn_band_kernel, tq=tq, n_tiles=n_tiles),
        grid=(nb, steps),
        in_specs=[pl.BlockSpec(memory_space=pltpu.SMEM), q_spec, kv_spec, kv_spec, c_spec, c_spec]
                 + [bias_spec(t) for t in range(n_tiles)],
        out_specs=out_spec,
        out_shape=out_shape,
        scratch_shapes=[pltpu.VMEM((2, tq + 2 * WINDOW, GRP * tq), F32), pltpu.VMEM((2, n_c, GRP * tq), F32)],
        compiler_params=_cparams(2),
        name="attention_band",
    )(sink, q, k, v, *cache, *([bias] * n_tiles))


def _layer_norm(z, g, b):
    mu = jnp.mean(z, axis=-1, keepdims=True)
    zc = z - mu
    var = jnp.mean(zc * zc, axis=-1, keepdims=True)
    return zc * lax.rsqrt(var + LN_EPS) * g + b


def _merge_mlp_kernel(x_ref, g1_ref, sh2_ref, sc2_ref, g2_ref, ya_ref, yw_ref, yg_ref, gate_ref,
                      wglu_ref, wa_ref, ww_ref, wg_ref, wout_ref, ln1g_ref, ln1b_ref,
                      wup_ref, wdn_ref, ln2g_ref, ln2b_ref, o_ref, ya_scr):
    ya = _chunk_rows_to_tokens(ya_ref, ya_scr)
    ya = ya * jax.nn.sigmoid(_dot(ya.astype(BF16), wglu_ref[...]))

    def gate(c):
        return gate_ref[:, c * D_MODEL:(c + 1) * D_MODEL].astype(F32)

    def branch(yt_ref, w_ref):
        outs = [lax.dot_general(yt_ref[s], w_ref[...], TN_DIMS, preferred_element_type=F32) for s in range(yt_ref.shape[0])]
        return outs[0] if len(outs) == 1 else jnp.concatenate(outs, axis=0)

    m = (gate(0) * _dot(ya.astype(BF16), wa_ref[...])
         + gate(1) * branch(yw_ref, ww_ref)
         + gate(2) * branch(yg_ref, wg_ref))
    f = _dot(m.astype(BF16), wout_ref[...])
    x1 = _layer_norm(DEEPNORM_ALPHA * x_ref[...] + g1_ref[0] * f, ln1g_ref[...], ln1b_ref[...])

    h = (x1 * (1.0 + sc2_ref[0]) + sh2_ref[0]).astype(BF16)
    acc = jnp.zeros(x1.shape, F32)
    for c in range(D_FF // FF_TILE):
        up = jnp.maximum(_dot(h, wup_ref[:, c * FF_TILE:(c + 1) * FF_TILE]), 0.0)
        acc = acc + _dot((up * up).astype(BF16), wdn_ref[c * FF_TILE:(c + 1) * FF_TILE, :])
    o_ref[...] = _layer_norm(DEEPNORM_ALPHA * x1 + g2_ref[0] * acc, ln2g_ref[...], ln2b_ref[...])


def _merge_mlp(x, mod_l, row_of_tile, ya_rows, yw_t, yg_t, gates, lw):
    n_tok = x.shape[0]
    tm = ROW_TILE
    seq_len = yw_t.shape[2]
    if seq_len >= tm:
        tiles_per_seq = seq_len // tm
        att = pl.BlockSpec((1, D_ATT, tm), lambda i: (i // tiles_per_seq, 0, i % tiles_per_seq))
    else:
        att = pl.BlockSpec((tm // seq_len, D_ATT, seq_len), lambda i: (i, 0, 0))
    row = lambda w: pl.BlockSpec((tm, w), lambda i: (i, 0))
    vec = _resident((1, D_MODEL))
    return pl.pallas_call(
        _merge_mlp_kernel,
        grid=(n_tok // tm,),
        in_specs=[row(D_MODEL)] + [_mod_spec(c, row_of_tile) for c in (2, 3, 4, 5)]
                 + [pl.BlockSpec((S5_NPAIR, CHUNKS_PER_TILE, S5_ROW), lambda i: (0, i, 0)),
                    att, att, row(3 * D_MODEL),
                    _resident((D_SSM, D_SSM)), _resident((D_SSM, D_MODEL)), _resident((D_ATT, D_MODEL)),
                    _resident((D_ATT, D_MODEL)), _resident((D_MODEL, D_MODEL)), vec, vec,
                    _resident((D_MODEL, D_FF)), _resident((D_FF, D_MODEL)), vec, vec],
        out_specs=row(D_MODEL),
        out_shape=jax.ShapeDtypeStruct((n_tok, D_MODEL), F32),
        scratch_shapes=[pltpu.VMEM((LANE_SLABS, tm, LANES), F32)],
        compiler_params=_cparams(1),
        name="merge_mlp_residual",
    )(x, mod_l, mod_l, mod_l, mod_l, ya_rows, yw_t, yg_t, gates,
      lw['w_glu'], lw['w_br_ssm'], lw['w_br_win'], lw['w_br_glb'], lw['w_out'], lw['ln1_g'], lw['ln1_b'],
      lw['w_up'], lw['w_down'], lw['ln2_g'], lw['ln2_b'])


def _rope_tables(n_tok):
    rows = n_tok // GRID_W
    row = jnp.repeat(jnp.arange(rows, dtype=F32), GRID_W)
    col = jnp.tile(jnp.arange(GRID_W, dtype=F32), rows)
    n_freq = HEAD_DIM // 4
    inv = ROPE_BASE ** (-jnp.arange(n_freq, dtype=F32) / n_freq)
    ang = jnp.concatenate([row[:, None] * inv, col[:, None] * inv], axis=-1)
    cos, sin = jnp.cos(ang), jnp.sin(ang)
    cos_t = jnp.tile(jnp.concatenate([cos, cos], axis=-1), (1, LANES // HEAD_DIM))
    sin_t = jnp.tile(jnp.concatenate([-sin, sin], axis=-1), (1, LANES // HEAD_DIM))
    return cos_t, sin_t


def _block_diag_ones(n):
    idx = np.arange(n) // HEAD_DIM
    return jnp.asarray(idx[:, None] == idx[None, :], dtype=BF16)


def _permute_q_columns(w):
    def perm(block):
        return block.reshape(D_MODEL, N_HEADS, HEAD_DIM)[:, np.array(Q_SLOT_ORDER), :].reshape(D_MODEL, D_ATT)
    return jnp.concatenate([w[:, :O_QW], perm(w[:, O_QW:O_KW]), w[:, O_KW:O_QG], perm(w[:, O_QG:O_KG]), w[:, O_KG:]], axis=1)


def _layer(x, lw, mod_l, row_of_tile, nb, seq_len, ctx, rope_tabs):
    latent = ctx is not None
    u_rows, qw, qg, gates, kw, kg, vg_t, vw, *kv_f32 = _in_projection(
        x, mod_l, row_of_tile, lw['w_in'], lw['qn'], lw['kn'], lw['bdq'], lw['bdk'], rope_tabs, seq_len,
        vw_transposed=not latent, keep_f32_kv=not latent)
    if latent:
        s0 = [ctx[0][:, d, part].reshape(nb, N_STATE) for d in (0, 1) for part in (0, 1)]
    else:
        s0 = [jnp.zeros((nb, N_STATE), F32)] * 4
    ya_rows, s_fin = _s5_branch(u_rows, lw['s5_ops'], lw['d_skip'], s0, nb, seq_len)

    if latent:
        n_ctx = ctx[1].shape[1]
        k_wc, v_wc, k_gc, v_gc = [t.reshape(nb, n_ctx, D_KV).astype(BF16) for t in ctx[1:]]
        yw = _attention_band(qw, kw, vw, lw['sink'], nb, seq_len, 256, (k_wc, v_wc))
        yg = _attention_full(qg, kg, vg_t, None, nb, seq_len, 256, 256, (k_gc, jnp.swapaxes(v_gc, 1, 2)))
        new_ctx = None
    else:
        yw = _attention_full(qw, kw, vw, lw['sink'], nb, seq_len, seq_len, seq_len, n_seq=2)
        yg = _attention_full(qg, kg, vg_t, None, nb, seq_len, seq_len, seq_len, n_seq=2)
        new_ctx = (s_fin,) + tuple(kv_f32)
    x2 = _merge_mlp(x, mod_l, row_of_tile, ya_rows, yw, yg, gates, lw)
    return x2, new_ctx


def kernel(x_prompt, x_sample, state_ssm, cache_k_win, cache_v_win, cache_k_glb, cache_v_glb, c, c_ctx, w_mod, b_mod, w_in, ssm_lam_re, ssm_lam_im, ssm_log_step, ssm_b_re, ssm_b_im, ssm_c_re, ssm_c_im, ssm_d, w_glu, sink_win, q_norm_glb, k_norm_glb, w_br_ssm, w_br_win, w_br_glb, w_out, ln1_g, ln1_b, w_up, w_down, ln2_g, ln2_b):
    n_ctx_b, ctx_len, _ = x_prompt.shape
    n_lat_b, lat_len, _ = x_sample.shape
    assert ctx_len % ROW_TILE == 0 or ROW_TILE % ctx_len == 0
    assert lat_len % ROW_TILE == 0 and (n_ctx_b * ctx_len) % ROW_TILE == 0

    cond8 = jnp.zeros((8, D_MODEL), F32).at[0].set(c_ctx).at[1:1 + n_lat_b].set(c)
    mod = _modulation(cond8, w_mod, b_mod).reshape(DEPTH, 8, 1, 6 * D_MODEL)
    rope_tabs = _rope_tables(lat_len)
    bdq, bdk = _block_diag_ones(D_ATT), _block_diag_ones(D_KV)
    lat_tiles = lat_len // ROW_TILE
    ctx_row = lambda i: 0
    lat_row = lambda i: 1 + i // lat_tiles

    xp = x_prompt.reshape(n_ctx_b * ctx_len, D_MODEL)
    xs = x_sample.reshape(n_lat_b * lat_len, D_MODEL)
    new_ssm, new_kw, new_vw, new_kg, new_vg = [], [], [], [], []
    for l in range(DEPTH):
        lw = dict(
            w_in=_permute_q_columns(w_in[l]).astype(BF16),
            qn=jnp.tile(q_norm_glb[l], N_HEADS).reshape(1, D_ATT), kn=jnp.tile(k_norm_glb[l], N_KV).reshape(1, D_KV),
            bdq=bdq, bdk=bdk,
            s5_ops=_s5_operators(ssm_lam_re[l], ssm_lam_im[l], ssm_log_step[l],
                                 ssm_b_re[l], ssm_b_im[l], ssm_c_re[l], ssm_c_im[l]),
            d_skip=ssm_d[l],
            sink=sink_win[l],
            w_glu=w_glu[l].astype(BF16), w_br_ssm=w_br_ssm[l].astype(BF16), w_br_win=w_br_win[l].astype(BF16),
            w_br_glb=w_br_glb[l].astype(BF16), w_out=w_out[l].astype(BF16),
            ln1_g=ln1_g[l].reshape(1, D_MODEL), ln1_b=ln1_b[l].reshape(1, D_MODEL),
            w_up=w_up[l].astype(BF16), w_down=w_down[l].astype(BF16),
            ln2_g=ln2_g[l].reshape(1, D_MODEL), ln2_b=ln2_b[l].reshape(1, D_MODEL),
        )
        xp, (s_fin, kw, vw, kg, vg) = _layer(xp, lw, mod[l], ctx_row, n_ctx_b, ctx_len, None, None)
        new_ssm.append(jnp.stack(s_fin, axis=1).reshape(n_ctx_b, 2, 2, SSM_GROUPS, SSM_STATE))
        for acc, t in ((new_kw, kw), (new_vw, vw), (new_kg, kg), (new_vg, vg)):
            acc.append(t.reshape(n_ctx_b, ctx_len, N_KV, HEAD_DIM))
        ctx = (state_ssm[:, l], cache_k_win[:, l], cache_v_win[:, l], cache_k_glb[:, l], cache_v_glb[:, l])
        xs, _ = _layer(xs, lw, mod[l], lat_row, n_lat_b, lat_len, ctx, rope_tabs)
    return (xp.reshape(x_prompt.shape), xs.reshape(x_sample.shape),
            jnp.stack(new_ssm, axis=1), jnp.stack(new_kw, axis=1), jnp.stack(new_vw, axis=1),
            jnp.stack(new_kg, axis=1), jnp.stack(new_vg, axis=1))
```

```python
import functools

import jax
import jax.numpy as jnp
import numpy as np
from jax import lax
from jax.experimental import pallas as pl
from jax.experimental.pallas import tpu as pltpu

F32 = jnp.float32
BF16 = jnp.bfloat16

D_MODEL = 1024
DEPTH = 2
GRID_W = 64
HEAD_DIM = 64
D_SSM = 512
SSM_GROUP_CH = 16
SSM_GROUPS = 32
SSM_STATE = 64
N_HEADS = 8
N_KV = 2
GRP = N_HEADS // N_KV
D_ATT = N_HEADS * HEAD_DIM
D_KV = N_KV * HEAD_DIM
WINDOW = 128
ROPE_BASE = 10000.0
D_FF = 4 * D_MODEL
LN_EPS = 1e-5
RMS_EPS = 1e-6
ATTN_SCALE = HEAD_DIM ** -0.5
DEEPNORM_ALPHA = (2.0 * DEPTH) ** 0.25
NEG_INF = -1e30
LOG2E = 1.4426950408889634
Q_SCALE = ATTN_SCALE * LOG2E
V_ROWS = HEAD_DIM + 16
Q_SLOT_ORDER = tuple(h for j in range(GRP) for h in (j, GRP + j))
N_IN = D_SSM + 2 * (D_ATT + 2 * D_KV) + 3 * D_MODEL
O_U = 0
O_QW = O_U + D_SSM
O_KW = O_QW + D_ATT
O_VW = O_KW + D_KV
O_QG = O_VW + D_KV
O_KG = O_QG + D_ATT
O_VG = O_KG + D_KV
O_GATE = O_VG + D_KV

S5_CHUNK = 16
S5_PAIR = 2 * SSM_GROUP_CH
S5_NPAIR = SSM_GROUPS // 2
S5_PAIR_BLOCK = 4
S5_ROW = S5_CHUNK * S5_PAIR
S5_PSTATE = 2 * SSM_STATE
N_STATE = SSM_GROUPS * SSM_STATE

LANES = 128
ROW_TILE = 512
FF_TILE = 1024
CHUNKS_PER_TILE = ROW_TILE // S5_CHUNK
LANE_SLABS = D_SSM // LANES
PAIRS_PER_SLAB = LANES // S5_PAIR
VMEM_LIMIT = 56 * 1024 * 1024
NT_DIMS = (((1,), (1,)), ((), ()))
TN_DIMS = (((0,), (0,)), ((), ()))


def _cparams(n_axes):
    return pltpu.CompilerParams(dimension_semantics=("arbitrary",) * n_axes, vmem_limit_bytes=VMEM_LIMIT)


def _resident(shape):
    nd = len(shape)
    return pl.BlockSpec(shape, lambda *_: (0,) * nd, pipeline_mode=pl.Buffered(1))


def _dot(a, b):
    return jnp.dot(a, b, preferred_element_type=F32)


def _mod_kernel(c_ref, w_ref, b_ref, o_ref):
    c = c_ref[...]
    a = (c * jax.nn.sigmoid(c)).astype(BF16)
    o_ref[0] = _dot(a, w_ref[0].astype(BF16)) + b_ref[0]


def _modulation(cond8, w_mod, b_mod):
    tn = 1536
    n_out = w_mod.shape[-1]
    return pl.pallas_call(
        _mod_kernel,
        grid=(DEPTH, n_out // tn),
        in_specs=[
            pl.BlockSpec((8, D_MODEL), lambda l, n: (0, 0)),
            pl.BlockSpec((1, D_MODEL, tn), lambda l, n: (l, 0, n)),
            pl.BlockSpec((1, 1, tn), lambda l, n: (l, 0, n)),
        ],
        out_specs=pl.BlockSpec((1, 8, tn), lambda l, n: (l, 0, n)),
        out_shape=jax.ShapeDtypeStruct((DEPTH, 8, n_out), F32),
        compiler_params=_cparams(2),
        name="modulation",
    )(cond8, w_mod, b_mod.reshape(DEPTH, 1, n_out))


def _mod_spec(chunk, row_of_tile):
    return pl.BlockSpec((1, 1, D_MODEL), lambda i: (row_of_tile(i), 0, chunk))


def _tokens_to_chunk_rows(u, scr, out_ref):
    n_chunks = u.shape[0] // S5_CHUNK
    for j in range(LANE_SLABS):
        scr[j] = u[:, j * LANES:(j + 1) * LANES]
    pieces = [[None] * S5_CHUNK for _ in range(S5_NPAIR)]
    for j in range(LANE_SLABS):
        for t in range(S5_CHUNK):
            step_rows = scr[j, pl.ds(t, n_chunks, stride=S5_CHUNK), :]
            for m in range(PAIRS_PER_SLAB):
                pieces[j * PAIRS_PER_SLAB + m][t] = step_rows[:, m * S5_PAIR:(m + 1) * S5_PAIR]
    for n in range(S5_NPAIR):
        out_ref[n] = jnp.concatenate(pieces[n], axis=1)


def _chunk_rows_to_tokens(y_ref, scr):
    n_chunks = y_ref.shape[1]
    for j in range(LANE_SLABS):
        for t in range(S5_CHUNK):
            piece = jnp.concatenate(
                [y_ref[j * PAIRS_PER_SLAB + m, :, t * S5_PAIR:(t + 1) * S5_PAIR] for m in range(PAIRS_PER_SLAB)], axis=1)
            scr[j, pl.ds(t, n_chunks, stride=S5_CHUNK), :] = piece
    return jnp.concatenate([scr[j] for j in range(LANE_SLABS)], axis=1)


def _head_rms(x, gain, ones_bd):
    ss = _dot((x * x).astype(BF16), ones_bd)
    return x * lax.rsqrt(ss * (1.0 / HEAD_DIM) + RMS_EPS) * gain


def _rope_chunk(xc, cos_t, sin_t):
    lane = lax.broadcasted_iota(jnp.int32, xc.shape, 1)
    first_half = (lane & (HEAD_DIM - 1)) < (HEAD_DIM // 2)
    partner = jnp.where(first_half, pltpu.roll(xc, LANES - HEAD_DIM // 2, 1), pltpu.roll(xc, HEAD_DIM // 2, 1))
    return xc * cos_t + partner * sin_t


def _inproj_kernel(*refs, rope, vw_transposed, keep_f32_kv):
    n_in = 10 if rope else 8
    x_ref, sh_ref, sc_ref, w_ref, qn_ref, kn_ref, bdq_ref, bdk_ref = refs[:8]
    u_ref, qw_ref, qg_ref, gate_ref, kw_ref, kg_ref, vgt_ref, vw_ref = refs[n_in:n_in + 8]
    f32_refs, u_scr = refs[n_in + 8:-1], refs[-1]
    h = (x_ref[...] * (1.0 + sc_ref[0]) + sh_ref[0]).astype(BF16)

    def proj(lo, width):
        return _dot(h, w_ref[:, lo:lo + width])

    def rotate(x):
        if not rope:
            return x
        chunks = [_rope_chunk(x[:, c * LANES:(c + 1) * LANES], refs[8][...], refs[9][...]) for c in range(x.shape[-1] // LANES)]
        return chunks[0] if len(chunks) == 1 else jnp.concatenate(chunks, axis=1)

    def slot_order(x):
        return jnp.concatenate([x[:, h * HEAD_DIM:(h + 1) * HEAD_DIM] for h in Q_SLOT_ORDER], axis=1)

    _tokens_to_chunk_rows(proj(O_U, D_SSM), u_scr, u_ref)
    qw_ref[...] = slot_order((rotate(proj(O_QW, D_ATT)) * Q_SCALE).astype(BF16))
    qg_ref[...] = slot_order((rotate(_head_rms(proj(O_QG, D_ATT), qn_ref[...], bdq_ref[...])) * Q_SCALE).astype(BF16))
    kw = rotate(proj(O_KW, D_KV))
    kg = rotate(_head_rms(proj(O_KG, D_KV), kn_ref[...], bdk_ref[...]))
    vw, vg = proj(O_VW, D_KV), proj(O_VG, D_KV)
    kw_ref[...] = kw.astype(BF16)
    kg_ref[...] = kg.astype(BF16)
    vgt_ref[...] = vg.T.astype(BF16)
    vw_ref[...] = (vw.T if vw_transposed else vw).astype(BF16)
    if keep_f32_kv:
        for out_ref, val in zip(f32_refs, (kw, vw, kg, vg)):
            out_ref[...] = val
    for c in range(3):
        gate = jax.nn.sigmoid(proj(O_GATE + c * D_MODEL, D_MODEL))
        gate_ref[:, c * D_MODEL:(c + 1) * D_MODEL] = gate.astype(gate_ref.dtype)


def _in_projection(x, mod_l, row_of_tile, w_in, qn, kn, bdq, bdk, rope_tabs, seq_len, vw_transposed, keep_f32_kv):
    n_tok = x.shape[0]
    tm = ROW_TILE
    rope = rope_tabs is not None
    row = lambda w: pl.BlockSpec((tm, w), lambda i: (i, 0))
    col = pl.BlockSpec((D_KV, tm), lambda i: (0, i))
    in_specs = [
        row(D_MODEL),
        _mod_spec(0, row_of_tile), _mod_spec(1, row_of_tile),
        _resident((D_MODEL, N_IN)),
        _resident((1, D_ATT)), _resident((1, D_KV)),
        _resident((D_ATT, D_ATT)), _resident((D_KV, D_KV)),
    ]
    args = [x, mod_l, mod_l, w_in, qn, kn, bdq, bdk]
    if rope:
        tiles_per_seq = seq_len // tm
        tab = pl.BlockSpec((tm, LANES), lambda i: (i % tiles_per_seq, 0))
        in_specs += [tab, tab]
        args += list(rope_tabs)
    tok = lambda w, dt: (jax.ShapeDtypeStruct((n_tok, w), dt), row(w))
    tr = (jax.ShapeDtypeStruct((D_KV, n_tok), BF16), col)
    outs = [(jax.ShapeDtypeStruct((S5_NPAIR, n_tok // S5_CHUNK, S5_ROW), F32),
             pl.BlockSpec((S5_NPAIR, CHUNKS_PER_TILE, S5_ROW), lambda i: (0, i, 0))),
            tok(D_ATT, BF16), tok(D_ATT, BF16), tok(3 * D_MODEL, BF16),
            tok(D_KV, BF16), tok(D_KV, BF16), tr, tr if vw_transposed else tok(D_KV, BF16)]
    if keep_f32_kv:
        outs += [tok(D_KV, F32)] * 4
    return pl.pallas_call(
        functools.partial(_inproj_kernel, rope=rope, vw_transposed=vw_transposed, keep_f32_kv=keep_f32_kv),
        grid=(n_tok // tm,),
        in_specs=in_specs,
        out_specs=[o[1] for o in outs],
        out_shape=[o[0] for o in outs],
        scratch_shapes=[pltpu.VMEM((LANE_SLABS, tm, LANES), F32)],
        compiler_params=_cparams(1),
        name="in_projection",
    )(*args)


def _zoh(lr, li, ls):
    dt = jnp.exp(ls)
    mag = jnp.exp(lr * dt)
    ar, ai = mag * jnp.cos(li * dt), mag * jnp.sin(li * dt)
    den = lr * lr + li * li
    fr = ((ar - 1.0) * lr + ai * li) / den
    fi = (ai * lr - (ar - 1.0) * li) / den
    return ar, ai, fr, fi


def _powers(ar, ai, n):
    out = [(jnp.ones_like(ar), jnp.zeros_like(ar))]
    for _ in range(n):
        pr, pi = out[-1]
        out.append((pr * ar - pi * ai, pr * ai + pi * ar))
    return out


def _s5_operator_kernel(lr_ref, li_ref, ls_ref, btr_ref, bti_ref, cr_ref, ci_ref,
                        top_ref, e0, e1, e2, e3, m0, m1, m2, m3, d0, d1, d2, d3, ext_scr):
    tc, cg, p = S5_CHUNK, SSM_GROUP_CH, SSM_STATE
    ends = ((e0, e1), (e2, e3))
    carries = ((m0, m1), (m2, m3))
    decays = ((d0, d1), (d2, d3))
    ext_scr[...] = jnp.zeros(ext_scr.shape, F32)
    zero_half = jnp.zeros((cg, p), F32)

    def place(out_ref, t, g2, val):
        halves = [val, zero_half] if g2 == 0 else [zero_half, val]
        out_ref[0, t * S5_PAIR + g2 * cg:t * S5_PAIR + (g2 + 1) * cg, :] = jnp.concatenate(halves, axis=1).astype(BF16)

    for d in range(2):
        for g2 in range(2):
            ar, ai, fr, fi = _zoh(lr_ref[d, 0, g2], li_ref[d, 0, g2], ls_ref[d, 0, g2])
            btr, bti = btr_ref[d, 0, g2], bti_ref[d, 0, g2]
            bbr, bbi = fr * btr - fi * bti, fr * bti + fi * btr
            cr, ci = cr_ref[d, 0, g2], ci_ref[d, 0, g2]
            pw = _powers(ar, ai, tc)
            group_rows = slice(g2 * cg, (g2 + 1) * cg)
            for j in range(tc + 1):
                pr, pi = pw[j]
                car, cai = cr * pr - ci * pi, cr * pi + ci * pr
                if j >= 1:
                    t = j - 1 if d == 0 else tc - j
                    place(carries[d][0], t, g2, car)
                    place(carries[d][1], t, g2, -cai)
                if j == tc:
                    break
                t = tc - 1 - j if d == 0 else j
                place(ends[d][0], t, g2, pr * bbr - pi * bbi)
                place(ends[d][1], t, g2, pr * bbi + pi * bbr)
                kt = (lax.dot_general(bbr, car, NT_DIMS, precision=lax.Precision.HIGHEST, preferred_element_type=F32)
                      - lax.dot_general(bbi, cai, NT_DIMS, precision=lax.Precision.HIGHEST, preferred_element_type=F32))
                slot = tc - 1 + j if d == 0 else tc - 1 - j
                lanes = slice(slot * S5_PAIR + g2 * cg, slot * S5_PAIR + (g2 + 1) * cg)
                if d == 1 and j == 0:
                    ext_scr[group_rows, lanes] = ext_scr[group_rows, lanes] + kt
                else:
                    ext_scr[group_rows, lanes] = kt
            for part in range(2):
                decays[d][part][:, g2 * p:(g2 + 1) * p] = pw[tc][part]
    for t in range(tc):
        for g2 in range(2):
            rows = slice(t * S5_PAIR + g2 * cg, t * S5_PAIR + (g2 + 1) * cg)
            window = slice((tc - 1 - t) * S5_PAIR, (tc - 1 - t) * S5_PAIR + S5_ROW)
            top_ref[0, rows, :] = ext_scr[g2 * cg:(g2 + 1) * cg, window].astype(BF16)


def _s5_operators(lam_re, lam_im, log_step, b_re, b_im, c_re, c_im):
    g, p, cg = SSM_GROUPS, SSM_STATE, SSM_GROUP_CH
    ls = jnp.broadcast_to(log_step[:, :, None], (2, g, p))

    def rows(v):
        return v.reshape(2, S5_NPAIR, 2, 1, p)

    def mat(v):
        return v.reshape(2, S5_NPAIR, 2, cg, p)

    def spec(*tail):
        return pl.BlockSpec((2, 1, 2) + tail, lambda n: (0, n, 0, 0, 0))

    state_op = pl.BlockSpec((1, S5_ROW, S5_PSTATE), lambda n: (n, 0, 0))
    out = pl.pallas_call(
        _s5_operator_kernel,
        grid=(S5_NPAIR,),
        in_specs=[spec(1, p)] * 3 + [spec(cg, p)] * 4,
        out_specs=([pl.BlockSpec((1, S5_ROW, S5_ROW), lambda n: (n, 0, 0))] + [state_op] * 8
                   + [pl.BlockSpec((1, S5_PSTATE), lambda n: (0, n))] * 4),
        out_shape=([jax.ShapeDtypeStruct((S5_NPAIR, S5_ROW, S5_ROW), BF16)]
                   + [jax.ShapeDtypeStruct((S5_NPAIR, S5_ROW, S5_PSTATE), BF16)] * 8
                   + [jax.ShapeDtypeStruct((1, N_STATE), F32)] * 4),
        scratch_shapes=[pltpu.VMEM((S5_PAIR, 2 * S5_ROW), F32)],
        compiler_params=_cparams(1),
        name="s5_operators",
    )(rows(lam_re), rows(lam_im), rows(ls),
      mat(jnp.swapaxes(b_re, -1, -2)), mat(jnp.swapaxes(b_im, -1, -2)), mat(c_re), mat(c_im))
    return out[0], out[1:5], out[5:9], out[9:13]


def _s5_kernel(u_ref, top_ref, d_ref, e0, e1, e2, e3, m0, m1, m2, m3, a0, a1, a2, a3, i0, i1, i2, i3,
               y_ref, f0, f1, f2, f3, se0, se1, se2, se3, st0, st1, st2, st3, *, nb, nc):
    ends, carries, decays, inits, finals = (e0, e1, e2, e3), (m0, m1, m2, m3), (a0, a1, a2, a3), (i0, i1, i2, i3), (f0, f1, f2, f3)
    se, st = (se0, se1, se2, se3), (st0, st1, st2, st3)
    for p in range(S5_PAIR_BLOCK):
        lanes = slice(p * S5_PSTATE, (p + 1) * S5_PSTATE)
        u = u_ref[p].astype(BF16)
        for k in range(4):
            se[k][:, lanes] = _dot(u, ends[k][p])

    a_fr, a_fi, a_br, a_bi = [a[...] for a in decays]
    for b in range(nb):
        base = b * nc

        def body(c, carry, base=base):
            fr, fi, br, bi = carry
            rc, rb = pl.ds(base + c, 1), pl.ds(base + nc - 1 - c, 1)
            st0[rc, :] = fr
            st1[rc, :] = fi
            st2[rb, :] = br
            st3[rb, :] = bi
            nfr = a_fr * fr - a_fi * fi + se0[rc, :]
            nfi = a_fr * fi + a_fi * fr + se1[rc, :]
            nbr = a_br * br - a_bi * bi + se2[rb, :]
            nbi = a_br * bi + a_bi * br + se3[rb, :]
            return nfr, nfi, nbr, nbi

        last = lax.fori_loop(0, nc, body, tuple(i_ref[b] for i_ref in inits))
        for f_ref, val in zip(finals, last):
            f_ref[b] = val

    for p in range(S5_PAIR_BLOCK):
        lanes = slice(p * S5_PSTATE, (p + 1) * S5_PSTATE)
        u = u_ref[p]
        y = _dot(u.astype(BF16), top_ref[p]) + d_ref[p] * u
        for k in range(4):
            y = y + lax.dot_general(st[k][:, lanes].astype(BF16), carries[k][p], NT_DIMS, preferred_element_type=F32)
        y_ref[p] = jax.nn.gelu(y)


def _s5_branch(u_rows, ops, d_skip, s0, nb, seq_len):
    top, ends, carries, decay = ops
    nc = seq_len // S5_CHUNK
    n_rows = nc * nb
    pg = S5_PAIR_BLOCK
    lanes = pg * S5_PSTATE
    d_rows = jnp.tile(d_skip.reshape(S5_NPAIR, 1, S5_PAIR), (1, S5_CHUNK, 1)).reshape(S5_NPAIR, 1, S5_ROW)
    rows_spec = pl.BlockSpec((pg, n_rows, S5_ROW), lambda n: (n, 0, 0))
    op_spec = pl.BlockSpec((pg, S5_ROW, S5_PSTATE), lambda n: (n, 0, 0))
    state_spec = pl.BlockSpec((nb, 1, lanes), lambda n: (0, 0, n))
    out = pl.pallas_call(
        functools.partial(_s5_kernel, nb=nb, nc=nc),
        grid=(S5_NPAIR // pg,),
        in_specs=[rows_spec, pl.BlockSpec((pg, S5_ROW, S5_ROW), lambda n: (n, 0, 0)),
                  pl.BlockSpec((pg, 1, S5_ROW), lambda n: (n, 0, 0))]
                 + [op_spec] * 8 + [pl.BlockSpec((1, lanes), lambda n: (0, n))] * 4 + [state_spec] * 4,
        out_specs=[rows_spec] + [state_spec] * 4,
        out_shape=[jax.ShapeDtypeStruct(u_rows.shape, F32)] + [jax.ShapeDtypeStruct((nb, 1, N_STATE), F32)] * 4,
        scratch_shapes=[pltpu.VMEM((n_rows, lanes), F32)] * 8,
        compiler_params=_cparams(1),
        name="s5_chunked",
    )(u_rows, top, d_rows, *ends, *carries, *decay, *[s.reshape(nb, 1, N_STATE) for s in s0])
    return out[0], [s.reshape(nb, N_STATE) for s in out[1:]]


def _stack_group_queries(q_ref, h, tq):
    lane = lax.broadcasted_iota(jnp.int32, (tq, LANES), 1)
    keep = (lane >= h * HEAD_DIM) & (lane < (h + 1) * HEAD_DIM)
    zero = jnp.zeros((tq, LANES), BF16)
    return jnp.concatenate([jnp.where(keep, q_ref[:, j * LANES:(j + 1) * LANES], zero) for j in range(GRP)], axis=0)


def _sink_lanes(sink_ref, h, tq):
    return jnp.concatenate([jnp.full((1, tq), sink_ref[h * GRP + j] * LOG2E, F32) for j in range(GRP)], axis=1)


def _values_with_ones(vt):
    return jnp.concatenate([vt, jnp.ones((V_ROWS - HEAD_DIM, vt.shape[1]), BF16)], axis=0)


def _store_heads(o_ref, h, o_t, tq):
    for j in range(GRP):
        head = h * GRP + j
        o_ref[head * HEAD_DIM:(head + 1) * HEAD_DIM, :] = o_t[:, j * tq:(j + 1) * tq].astype(o_ref.dtype)


def _transposed_out(nb, seq_len, tq, n_seq=1):
    return (pl.BlockSpec((n_seq, D_ATT, tq), lambda b, i: (b, 0, i)), jax.ShapeDtypeStruct((nb, D_ATT, seq_len), BF16))


def _attn_full_kernel(*refs, tq, tk, n_seq, use_sink, has_cache):
    refs = list(refs)
    sink_ref = refs.pop(0) if use_sink else None
    q_ref, k_ref, vt_ref = refs[:3]
    kc_ref, vtc_ref = refs[3:5] if has_cache else (None, None)
    o_ref, m_ref, acc_ref, sa_ref, sb_ref = refs[-5:]
    seq_len = k_ref.shape[0] // n_seq
    n_own = seq_len // tk
    n_chunks = n_own + (kc_ref.shape[1] // tk if has_cache else 0)
    rows = GRP * tq
    qs = [[_stack_group_queries(q_ref.at[s * tq:(s + 1) * tq], h, tq) for h in range(N_KV)] for s in range(n_seq)]
    for u in range(n_seq * N_KV):
        if use_sink:
            m_ref[u] = _sink_lanes(sink_ref, u % N_KV, tq)
            acc_ref[u] = jnp.concatenate([jnp.zeros((HEAD_DIM, rows), F32), jnp.ones((V_ROWS - HEAD_DIM, rows), F32)], axis=0)
        else:
            m_ref[u] = jnp.full((1, rows), NEG_INF, F32)
            acc_ref[u] = jnp.zeros((V_ROWS, rows), F32)

    def keys(s, c):
        if c < n_own:
            return k_ref[s * seq_len + c * tk:s * seq_len + (c + 1) * tk, :]
        return kc_ref[0, (c - n_own) * tk:(c - n_own + 1) * tk, :]

    def values_t(s, c, h):
        hd = slice(h * HEAD_DIM, (h + 1) * HEAD_DIM)
        if c < n_own:
            return vt_ref[hd, s * seq_len + c * tk:s * seq_len + (c + 1) * tk]
        return vtc_ref[0, hd, (c - n_own) * tk:(c - n_own + 1) * tk]

    def scores(s, c, s_ref):
        kc = keys(s, c)
        for h in range(N_KV):
            s_ref[h] = lax.dot_general(kc, qs[s][h], NT_DIMS, preferred_element_type=F32)

    def consume(s, c, s_ref):
        for h in range(N_KV):
            u = s * N_KV + h
            sc = s_ref[h]
            m_old = m_ref[u]
            m_new = jnp.maximum(m_old, jnp.max(sc, axis=0, keepdims=True))
            p = jnp.exp2(sc - m_new).astype(BF16)
            alpha = jnp.exp2(m_old - m_new)
            acc_ref[u] = alpha * acc_ref[u] + _dot(_values_with_ones(values_t(s, c, h)), p)
            m_ref[u] = m_new

    items = [(s, c) for s in range(n_seq) for c in range(n_chunks)]
    bufs = (sa_ref, sb_ref)
    scores(*items[0], bufs[0])
    for n, item in enumerate(items):
        if n + 1 < len(items):
            scores(*items[n + 1], bufs[(n + 1) % 2])
        consume(*item, bufs[n % 2])
    for s in range(n_seq):
        for h in range(N_KV):
            acc = acc_ref[s * N_KV + h]
            _store_heads(o_ref.at[s], h, acc[:HEAD_DIM] / acc[HEAD_DIM:HEAD_DIM + 1], tq)


def _attn_band_kernel(sink_ref, q_ref, k_ref, v_ref, kc_ref, vc_ref, *rest, tq, n_tiles):
    bias_refs, (o_ref, sb_ref, sc_ref) = rest[:n_tiles], rest[n_tiles:]
    n_lat = k_ref.shape[0]
    span = tq + 2 * WINDOW
    k_ctx, v_ctx = kc_ref[0], vc_ref[0]

    def tile_start(t):
        i = pl.program_id(1) * n_tiles + t
        return i, pl.multiple_of(jnp.clip(i * tq - WINDOW, 0, n_lat - span), WINDOW)

    def scores(t, h, slot):
        i, start = tile_start(t)
        qs = _stack_group_queries(q_ref.at[t * tq:(t + 1) * tq], h, tq)
        s_band = lax.dot_general(k_ref[pl.ds(start, span), :], qs, NT_DIMS, preferred_element_type=F32)
        sb_ref[slot] = s_band + bias_refs[t][0]
        sc_ref[slot] = lax.dot_general(k_ctx, qs, NT_DIMS, preferred_element_type=F32)

    def consume(t, h, slot):
        _, start = tile_start(t)
        s_band, s_ctx = sb_ref[slot], sc_ref[slot]
        sink = _sink_lanes(sink_ref, h, tq)
        m = jnp.maximum(jnp.maximum(jnp.max(s_band, axis=0, keepdims=True), jnp.max(s_ctx, axis=0, keepdims=True)), sink)
        p_band = jnp.exp2(s_band - m)
        p_ctx = jnp.exp2(s_ctx - m)
        den = jnp.sum(p_band, axis=0, keepdims=True) + jnp.sum(p_ctx, axis=0, keepdims=True) + jnp.exp2(sink - m)
        acc = (lax.dot_general(v_ref[pl.ds(start, span), :], p_band.astype(BF16), TN_DIMS, preferred_element_type=F32)
               + lax.dot_general(v_ctx, p_ctx.astype(BF16), TN_DIMS, preferred_element_type=F32))
        o_t = acc[h * HEAD_DIM:(h + 1) * HEAD_DIM] / den
        for j in range(GRP):
            head = h * GRP + j
            o_ref[0, head * HEAD_DIM:(head + 1) * HEAD_DIM, t * tq:(t + 1) * tq] = o_t[:, j * tq:(j + 1) * tq].astype(o_ref.dtype)

    items = [(t, h) for t in range(n_tiles) for h in range(N_KV)]
    scores(*items[0], 0)
    for n, item in enumerate(items):
        if n + 1 < len(items):
            scores(*items[n + 1], (n + 1) % 2)
        consume(*item, n % 2)


def _attention_full(q, k, vt, sink, nb, seq_len, tq, tk, cache=None, n_seq=1):
    assert n_seq == 1 or (tq == seq_len and cache is None)
    tiles = seq_len // tq
    q_spec = pl.BlockSpec((n_seq * tq, D_ATT), lambda b, i: (b * tiles + i, 0))
    in_specs = [q_spec, pl.BlockSpec((n_seq * seq_len, D_KV), lambda b, i: (b, 0)),
                pl.BlockSpec((D_KV, n_seq * seq_len), lambda b, i: (0, b))]
    args = [q, k, vt]
    if cache is not None:
        n_c = cache[0].shape[1]
        in_specs += [pl.BlockSpec((1, n_c, D_KV), lambda b, i: (b, 0, 0)), pl.BlockSpec((1, D_KV, n_c), lambda b, i: (b, 0, 0))]
        args += list(cache)
    if sink is not None:
        in_specs, args = [pl.BlockSpec(memory_space=pltpu.SMEM)] + in_specs, [sink] + args
    rows = GRP * tq
    out_spec, out_shape = _transposed_out(nb, seq_len, tq, n_seq)
    return pl.pallas_call(
        functools.partial(_attn_full_kernel, tq=tq, tk=tk, n_seq=n_seq, use_sink=sink is not None, has_cache=cache is not None),
        grid=(nb // n_seq, tiles),
        in_specs=in_specs,
        out_specs=out_spec,
        out_shape=out_shape,
        scratch_shapes=[pltpu.VMEM((n_seq * N_KV, 1, rows), F32), pltpu.VMEM((n_seq * N_KV, V_ROWS, rows), F32),
                        pltpu.VMEM((N_KV, tk, rows), F32), pltpu.VMEM((N_KV, tk, rows), F32)],
        compiler_params=_cparams(2),
        name="attention_full",
    )(*args)


def _attention_band(q, k, v, sink, nb, seq_len, tq, cache, n_tiles=4):
    steps = seq_len // (tq * n_tiles)
    tiles = seq_len // tq
    n_c = cache[0].shape[1]
    span = tq + 2 * WINDOW
    key_row = np.arange(span)[:, None]
    query = np.arange(GRP * tq)[None, :] % tq
    bias = jnp.asarray(np.stack([np.where(np.abs(key_row - lead - query) <= WINDOW, 0.0, NEG_INF)
                                 for lead in (0, WINDOW, 2 * WINDOW)]), dtype=F32)

    def bias_spec(t):
        def index(b, i):
            tile = i * n_tiles + t
            return (jnp.where(tile == 0, 0, jnp.where(tile == tiles - 1, 2, 1)), 0, 0)
        return pl.BlockSpec((1, span, GRP * tq), index)
    q_spec = pl.BlockSpec((n_tiles * tq, D_ATT), lambda b, i: (b * steps + i, 0))
    kv_spec = pl.BlockSpec((seq_len, D_KV), lambda b, i: (b, 0))
    c_spec = pl.BlockSpec((1, n_c, D_KV), lambda b, i: (b, 0, 0))
    out_spec, out_shape = _transposed_out(nb, seq_len, n_tiles * tq)
    return pl.pallas_call(
        functools.partial(_attn_band_kernel, tq=tq, n_tiles=n_tiles),
        grid=(nb, steps),
        in_specs=[pl.BlockSpec(memory_space=pltpu.SMEM), q_spec, kv_spec, kv_spec, c_spec, c_spec]
                 + [bias_spec(t) for t in range(n_tiles)],
        out_specs=out_spec,
        out_shape=out_shape,
        scratch_shapes=[pltpu.VMEM((2, tq + 2 * WINDOW, GRP * tq), F32), pltpu.VMEM((2, n_c, GRP * tq), F32)],
        compiler_params=_cparams(2),
        name="attention_band",
    )(sink, q, k, v, *cache, *([bias] * n_tiles))


def _layer_norm(z, g, b):
    mu = jnp.mean(z, axis=-1, keepdims=True)
    zc = z - mu
    var = jnp.mean(zc * zc, axis=-1, keepdims=True)
    return zc * lax.rsqrt(var + LN_EPS) * g + b


def _merge_mlp_kernel(x_ref, g1_ref, sh2_ref, sc2_ref, g2_ref, ya_ref, yw_ref, yg_ref, gate_ref,
                      wglu_ref, wa_ref, ww_ref, wg_ref, wout_ref, ln1g_ref, ln1b_ref,
                      wup_ref, wdn_ref, ln2g_ref, ln2b_ref, o_ref, ya_scr):
    ya = _chunk_rows_to_tokens(ya_ref, ya_scr)
    ya = ya * jax.nn.sigmoid(_dot(ya.astype(BF16), wglu_ref[...]))

    def gate(c):
        return gate_ref[:, c * D_MODEL:(c + 1) * D_MODEL].astype(F32)

    def branch(yt_ref, w_ref):
        outs = [lax.dot_general(yt_ref[s], w_ref[...], TN_DIMS, preferred_element_type=F32) for s in range(yt_ref.shape[0])]
        return outs[0] if len(outs) == 1 else jnp.concatenate(outs, axis=0)

    m = (gate(0) * _dot(ya.astype(BF16), wa_ref[...])
         + gate(1) * branch(yw_ref, ww_ref)
         + gate(2) * branch(yg_ref, wg_ref))
    f = _dot(m.astype(BF16), wout_ref[...])
    x1 = _layer_norm(DEEPNORM_ALPHA * x_ref[...] + g1_ref[0] * f, ln1g_ref[...], ln1b_ref[...])

    h = (x1 * (1.0 + sc2_ref[0]) + sh2_ref[0]).astype(BF16)
    acc = jnp.zeros(x1.shape, F32)
    for c in range(D_FF // FF_TILE):
        up = jnp.maximum(_dot(h, wup_ref[:, c * FF_TILE:(c + 1) * FF_TILE]), 0.0)
        acc = acc + _dot((up * up).astype(BF16), wdn_ref[c * FF_TILE:(c + 1) * FF_TILE, :])
    o_ref[...] = _layer_norm(DEEPNORM_ALPHA * x1 + g2_ref[0] * acc, ln2g_ref[...], ln2b_ref[...])


def _merge_mlp(x, mod_l, row_of_tile, ya_rows, yw_t, yg_t, gates, lw):
    n_tok = x.shape[0]
    tm = ROW_TILE
    seq_len = yw_t.shape[2]
    if seq_len >= tm:
        tiles_per_seq = seq_len // tm
        att = pl.BlockSpec((1, D_ATT, tm), lambda i: (i // tiles_per_seq, 0, i % tiles_per_seq))
    else:
        att = pl.BlockSpec((tm // seq_len, D_ATT, seq_len), lambda i: (i, 0, 0))
    row = lambda w: pl.BlockSpec((tm, w), lambda i: (i, 0))
    vec = _resident((1, D_MODEL))
    return pl.pallas_call(
        _merge_mlp_kernel,
        grid=(n_tok // tm,),
        in_specs=[row(D_MODEL)] + [_mod_spec(c, row_of_tile) for c in (2, 3, 4, 5)]
                 + [pl.BlockSpec((S5_NPAIR, CHUNKS_PER_TILE, S5_ROW), lambda i: (0, i, 0)),
                    att, att, row(3 * D_MODEL),
                    _resident((D_SSM, D_SSM)), _resident((D_SSM, D_MODEL)), _resident((D_ATT, D_MODEL)),
                    _resident((D_ATT, D_MODEL)), _resident((D_MODEL, D_MODEL)), vec, vec,
                    _resident((D_MODEL, D_FF)), _resident((D_FF, D_MODEL)), vec, vec],
        out_specs=row(D_MODEL),
        out_shape=jax.ShapeDtypeStruct((n_tok, D_MODEL), F32),
        scratch_shapes=[pltpu.VMEM((LANE_SLABS, tm, LANES), F32)],
        compiler_params=_cparams(1),
        name="merge_mlp_residual",
    )(x, mod_l, mod_l, mod_l, mod_l, ya_rows, yw_t, yg_t, gates,
      lw['w_glu'], lw['w_br_ssm'], lw['w_br_win'], lw['w_br_glb'], lw['w_out'], lw['ln1_g'], lw['ln1_b'],
      lw['w_up'], lw['w_down'], lw['ln2_g'], lw['ln2_b'])


def _rope_tables(n_tok):
    rows = n_tok // GRID_W
    row = jnp.repeat(jnp.arange(rows, dtype=F32), GRID_W)
    col = jnp.tile(jnp.arange(GRID_W, dtype=F32), rows)
    n_freq = HEAD_DIM // 4
    inv = ROPE_BASE ** (-jnp.arange(n_freq, dtype=F32) / n_freq)
    ang = jnp.concatenate([row[:, None] * inv, col[:, None] * inv], axis=-1)
    cos, sin = jnp.cos(ang), jnp.sin(ang)
    cos_t = jnp.tile(jnp.concatenate([cos, cos], axis=-1), (1, LANES // HEAD_DIM))
    sin_t = jnp.tile(jnp.concatenate([-sin, sin], axis=-1), (1, LANES // HEAD_DIM))
    return cos_t, sin_t


def _block_diag_ones(n):
    idx = np.arange(n) // HEAD_DIM
    return jnp.asarray(idx[:, None] == idx[None, :], dtype=BF16)


def _layer(x, lw, mod_l, row_of_tile, nb, seq_len, ctx, rope_tabs):
    latent = ctx is not None
    u_rows, qw, qg, gates, kw, kg, vg_t, vw, *kv_f32 = _in_projection(
        x, mod_l, row_of_tile, lw['w_in'], lw['qn'], lw['kn'], lw['bdq'], lw['bdk'], rope_tabs, seq_len,
        vw_transposed=not latent, keep_f32_kv=not latent)
    if latent:
        s0 = [ctx[0][:, d, part].reshape(nb, N_STATE) for d in (0, 1) for part in (0, 1)]
    else:
        s0 = [jnp.zeros((nb, N_STATE), F32)] * 4
    ya_rows, s_fin = _s5_branch(u_rows, lw['s5_ops'], lw['d_skip'], s0, nb, seq_len)

    if latent:
        n_ctx = ctx[1].shape[1]
        k_wc, v_wc, k_gc, v_gc = [t.reshape(nb, n_ctx, D_KV).astype(BF16) for t in ctx[1:]]
        yw = _attention_band(qw, kw, vw, lw['sink'], nb, seq_len, 256, (k_wc, v_wc))
        yg = _attention_full(qg, kg, vg_t, None, nb, seq_len, 256, 256, (k_gc, jnp.swapaxes(v_gc, 1, 2)))
        new_ctx = None
    else:
        yw = _attention_full(qw, kw, vw, lw['sink'], nb, seq_len, seq_len, seq_len, n_seq=2)
        yg = _attention_full(qg, kg, vg_t, None, nb, seq_len, seq_len, seq_len, n_seq=2)
        new_ctx = (s_fin,) + tuple(kv_f32)
    x2 = _merge_mlp(x, mod_l, row_of_tile, ya_rows, yw, yg, gates, lw)
    return x2, new_ctx


def kernel(x_prompt, x_sample, state_ssm, cache_k_win, cache_v_win, cache_k_glb, cache_v_glb, c, c_ctx, w_mod, b_mod, w_in, ssm_lam_re, ssm_lam_im, ssm_log_step, ssm_b_re, ssm_b_im, ssm_c_re, ssm_c_im, ssm_d, w_glu, sink_win, q_norm_glb, k_norm_glb, w_br_ssm, w_br_win, w_br_glb, w_out, ln1_g, ln1_b, w_up, w_down, ln2_g, ln2_b):
    n_ctx_b, ctx_len, _ = x_prompt.shape
    n_lat_b, lat_len, _ = x_sample.shape
    assert ctx_len % ROW_TILE == 0 or ROW_TILE % ctx_len == 0
    assert lat_len % ROW_TILE == 0 and (n_ctx_b * ctx_len) % ROW_TILE == 0

    cond8 = jnp.zeros((8, D_MODEL), F32).at[0].set(c_ctx).at[1:1 + n_lat_b].set(c)
    mod = _modulation(cond8, w_mod, b_mod).reshape(DEPTH, 8, 1, 6 * D_MODEL)
    rope_tabs = _rope_tables(lat_len)
    bdq, bdk = _block_diag_ones(D_ATT), _block_diag_ones(D_KV)
    lat_tiles = lat_len // ROW_TILE
    ctx_row = lambda i: 0
    lat_row = lambda i: 1 + i // lat_tiles

    xp = x_prompt.reshape(n_ctx_b * ctx_len, D_MODEL)
    xs = x_sample.reshape(n_lat_b * lat_len, D_MODEL)
    new_ssm, new_kw, new_vw, new_kg, new_vg = [], [], [], [], []
    for l in range(DEPTH):
        lw = dict(
            w_in=w_in[l].astype(BF16),
            qn=jnp.tile(q_norm_glb[l], N_HEADS).reshape(1, D_ATT), kn=jnp.tile(k_norm_glb[l], N_KV).reshape(1, D_KV),
            bdq=bdq, bdk=bdk,
            s5_ops=_s5_operators(ssm_lam_re[l], ssm_lam_im[l], ssm_log_step[l],
                                 ssm_b_re[l], ssm_b_im[l], ssm_c_re[l], ssm_c_im[l]),
            d_skip=ssm_d[l],
            sink=sink_win[l],
            w_glu=w_glu[l].astype(BF16), w_br_ssm=w_br_ssm[l].astype(BF16), w_br_win=w_br_win[l].astype(BF16),
            w_br_glb=w_br_glb[l].astype(BF16), w_out=w_out[l].astype(BF16),
            ln1_g=ln1_g[l].reshape(1, D_MODEL), ln1_b=ln1_b[l].reshape(1, D_MODEL),
            w_up=w_up[l].astype(BF16), w_down=w_down[l].astype(BF16),
            ln2_g=ln2_g[l].reshape(1, D_MODEL), ln2_b=ln2_b[l].reshape(1, D_MODEL),
        )
        xp, (s_fin, kw, vw, kg, vg) = _layer(xp, lw, mod[l], ctx_row, n_ctx_b, ctx_len, None, None)
        new_ssm.append(jnp.stack(s_fin, axis=1).reshape(n_ctx_b, 2, 2, SSM_GROUPS, SSM_STATE))
        for acc, t in ((new_kw, kw), (new_vw, vw), (new_kg, kg), (new_vg, vg)):
            acc.append(t.reshape(n_ctx_b, ctx_len, N_KV, HEAD_DIM))
        ctx = (state_ssm[:, l], cache_k_win[:, l], cache_v_win[:, l], cache_k_glb[:, l], cache_v_glb[:, l])
        xs, _ = _layer(xs, lw, mod[l], lat_row, n_lat_b, lat_len, ctx, rope_tabs)
    return (xp.reshape(x_prompt.shape), xs.reshape(x_sample.shape),
            jnp.stack(new_ssm, axis=1), jnp.stack(new_kw, axis=1), jnp.stack(new_vw, axis=1),
            jnp.stack(new_kg, axis=1), jnp.stack(new_vg, axis=1))
```

```python
import functools

import jax
import jax.numpy as jnp
import numpy as np
from jax import lax
from jax.experimental import pallas as pl
from jax.experimental.pallas import tpu as pltpu

F32 = jnp.float32
BF16 = jnp.bfloat16

D_MODEL = 1024
DEPTH = 2
GRID_W = 64
HEAD_DIM = 64
D_SSM = 512
SSM_GROUP_CH = 16
SSM_GROUPS = 32
SSM_STATE = 64
N_HEADS = 8
N_KV = 2
GRP = N_HEADS // N_KV
D_ATT = N_HEADS * HEAD_DIM
D_KV = N_KV * HEAD_DIM
WINDOW = 128
ROPE_BASE = 10000.0
D_FF = 4 * D_MODEL
LN_EPS = 1e-5
RMS_EPS = 1e-6
ATTN_SCALE = HEAD_DIM ** -0.5
DEEPNORM_ALPHA = (2.0 * DEPTH) ** 0.25
NEG_INF = -1e30
LOG2E = 1.4426950408889634
Q_SCALE = ATTN_SCALE * LOG2E
V_ROWS = HEAD_DIM + 16
Q_SLOT_ORDER = tuple(h for j in range(GRP) for h in (j, GRP + j))
N_IN = D_SSM + 2 * (D_ATT + 2 * D_KV) + 3 * D_MODEL
O_U = 0
O_QW = O_U + D_SSM
O_KW = O_QW + D_ATT
O_VW = O_KW + D_KV
O_QG = O_VW + D_KV
O_KG = O_QG + D_ATT
O_VG = O_KG + D_KV
O_GATE = O_VG + D_KV

S5_CHUNK = 16
S5_PAIR = 2 * SSM_GROUP_CH
S5_NPAIR = SSM_GROUPS // 2
S5_PAIR_BLOCK = 4
S5_ROW = S5_CHUNK * S5_PAIR
S5_PSTATE = 2 * SSM_STATE
N_STATE = SSM_GROUPS * SSM_STATE

LANES = 128
ROW_TILE = 512
FF_TILE = 1024
CHUNKS_PER_TILE = ROW_TILE // S5_CHUNK
LANE_SLABS = D_SSM // LANES
PAIRS_PER_SLAB = LANES // S5_PAIR
VMEM_LIMIT = 56 * 1024 * 1024
NT_DIMS = (((1,), (1,)), ((), ()))
TN_DIMS = (((0,), (0,)), ((), ()))


def _cparams(n_axes):
    return pltpu.CompilerParams(dimension_semantics=("arbitrary",) * n_axes, vmem_limit_bytes=VMEM_LIMIT)


def _resident(shape):
    nd = len(shape)
    return pl.BlockSpec(shape, lambda *_: (0,) * nd, pipeline_mode=pl.Buffered(1))


def _dot(a, b):
    return jnp.dot(a, b, preferred_element_type=F32)


def _mod_kernel(c_ref, w_ref, b_ref, o_ref):
    c = c_ref[...]
    a = (c * jax.nn.sigmoid(c)).astype(BF16)
    o_ref[0] = _dot(a, w_ref[0].astype(BF16)) + b_ref[0]


def _modulation(cond8, w_mod, b_mod):
    tn = 1536
    n_out = w_mod.shape[-1]
    return pl.pallas_call(
        _mod_kernel,
        grid=(DEPTH, n_out // tn),
        in_specs=[
            pl.BlockSpec((8, D_MODEL), lambda l, n: (0, 0)),
            pl.BlockSpec((1, D_MODEL, tn), lambda l, n: (l, 0, n)),
            pl.BlockSpec((1, 1, tn), lambda l, n: (l, 0, n)),
        ],
        out_specs=pl.BlockSpec((1, 8, tn), lambda l, n: (l, 0, n)),
        out_shape=jax.ShapeDtypeStruct((DEPTH, 8, n_out), F32),
        compiler_params=_cparams(2),
        name="modulation",
    )(cond8, w_mod, b_mod.reshape(DEPTH, 1, n_out))


def _mod_spec(chunk, row_of_tile):
    return pl.BlockSpec((1, 1, D_MODEL), lambda i: (row_of_tile(i), 0, chunk))


def _tokens_to_chunk_rows(u, scr, out_ref):
    n_chunks = u.shape[0] // S5_CHUNK
    for j in range(LANE_SLABS):
        scr[j] = u[:, j * LANES:(j + 1) * LANES]
    pieces = [[None] * S5_CHUNK for _ in range(S5_NPAIR)]
    for j in range(LANE_SLABS):
        for t in range(S5_CHUNK):
            step_rows = scr[j, pl.ds(t, n_chunks, stride=S5_CHUNK), :]
            for m in range(PAIRS_PER_SLAB):
                pieces[j * PAIRS_PER_SLAB + m][t] = step_rows[:, m * S5_PAIR:(m + 1) * S5_PAIR]
    for n in range(S5_NPAIR):
        out_ref[n] = jnp.concatenate(pieces[n], axis=1)


def _chunk_rows_to_tokens(y_ref, scr):
    n_chunks = y_ref.shape[1]
    for j in range(LANE_SLABS):
        for t in range(S5_CHUNK):
            piece = jnp.concatenate(
                [y_ref[j * PAIRS_PER_SLAB + m, :, t * S5_PAIR:(t + 1) * S5_PAIR] for m in range(PAIRS_PER_SLAB)], axis=1)
            scr[j, pl.ds(t, n_chunks, stride=S5_CHUNK), :] = piece
    return jnp.concatenate([scr[j] for j in range(LANE_SLABS)], axis=1)


def _head_rms(x, gain, ones_bd):
    ss = _dot((x * x).astype(BF16), ones_bd)
    return x * lax.rsqrt(ss * (1.0 / HEAD_DIM) + RMS_EPS) * gain


def _rope_chunk(xc, cos_t, sin_t):
    lane = lax.broadcasted_iota(jnp.int32, xc.shape, 1)
    first_half = (lane & (HEAD_DIM - 1)) < (HEAD_DIM // 2)
    partner = jnp.where(first_half, pltpu.roll(xc, LANES - HEAD_DIM // 2, 1), pltpu.roll(xc, HEAD_DIM // 2, 1))
    return xc * cos_t + partner * sin_t


def _inproj_kernel(*refs, rope, vw_transposed, keep_f32_kv):
    n_in = 10 if rope else 8
    x_ref, sh_ref, sc_ref, w_ref, qn_ref, kn_ref, bdq_ref, bdk_ref = refs[:8]
    u_ref, qw_ref, qg_ref, gate_ref, kw_ref, kg_ref, vgt_ref, vw_ref = refs[n_in:n_in + 8]
    f32_refs, u_scr = refs[n_in + 8:-1], refs[-1]
    h = (x_ref[...] * (1.0 + sc_ref[0]) + sh_ref[0]).astype(BF16)

    def proj(lo, width):
        return _dot(h, w_ref[:, lo:lo + width])

    def rotate(x):
        if not rope:
            return x
        chunks = [_rope_chunk(x[:, c * LANES:(c + 1) * LANES], refs[8][...], refs[9][...]) for c in range(x.shape[-1] // LANES)]
        return chunks[0] if len(chunks) == 1 else jnp.concatenate(chunks, axis=1)

    def slot_order(x):
        return jnp.concatenate([x[:, h * HEAD_DIM:(h + 1) * HEAD_DIM] for h in Q_SLOT_ORDER], axis=1)

    _tokens_to_chunk_rows(proj(O_U, D_SSM), u_scr, u_ref)
    qw_ref[...] = slot_order((rotate(proj(O_QW, D_ATT)) * Q_SCALE).astype(BF16))
    qg_ref[...] = slot_order((rotate(_head_rms(proj(O_QG, D_ATT), qn_ref[...], bdq_ref[...])) * Q_SCALE).astype(BF16))
    kw = rotate(proj(O_KW, D_KV))
    kg = rotate(_head_rms(proj(O_KG, D_KV), kn_ref[...], bdk_ref[...]))
    vw, vg = proj(O_VW, D_KV), proj(O_VG, D_KV)
    kw_ref[...] = kw.astype(BF16)
    kg_ref[...] = kg.astype(BF16)
    vgt_ref[...] = vg.T.astype(BF16)
    vw_ref[...] = (vw.T if vw_transposed else vw).astype(BF16)
    if keep_f32_kv:
        for out_ref, val in zip(f32_refs, (kw, vw, kg, vg)):
            out_ref[...] = val
    for c in range(3):
        gate = jax.nn.sigmoid(proj(O_GATE + c * D_MODEL, D_MODEL))
        gate_ref[:, c * D_MODEL:(c + 1) * D_MODEL] = gate.astype(gate_ref.dtype)


def _in_projection(x, mod_l, row_of_tile, w_in, qn, kn, bdq, bdk, rope_tabs, seq_len, vw_transposed, keep_f32_kv):
    n_tok = x.shape[0]
    tm = ROW_TILE
    rope = rope_tabs is not None
    row = lambda w: pl.BlockSpec((tm, w), lambda i: (i, 0))
    col = pl.BlockSpec((D_KV, tm), lambda i: (0, i))
    in_specs = [
        row(D_MODEL),
        _mod_spec(0, row_of_tile), _mod_spec(1, row_of_tile),
        _resident((D_MODEL, N_IN)),
        _resident((1, D_ATT)), _resident((1, D_KV)),
        _resident((D_ATT, D_ATT)), _resident((D_KV, D_KV)),
    ]
    args = [x, mod_l, mod_l, w_in, qn, kn, bdq, bdk]
    if rope:
        tiles_per_seq = seq_len // tm
        tab = pl.BlockSpec((tm, LANES), lambda i: (i % tiles_per_seq, 0))
        in_specs += [tab, tab]
        args += list(rope_tabs)
    tok = lambda w, dt: (jax.ShapeDtypeStruct((n_tok, w), dt), row(w))
    tr = (jax.ShapeDtypeStruct((D_KV, n_tok), BF16), col)
    outs = [(jax.ShapeDtypeStruct((S5_NPAIR, n_tok // S5_CHUNK, S5_ROW), F32),
             pl.BlockSpec((S5_NPAIR, CHUNKS_PER_TILE, S5_ROW), lambda i: (0, i, 0))),
            tok(D_ATT, BF16), tok(D_ATT, BF16), tok(3 * D_MODEL, BF16),
            tok(D_KV, BF16), tok(D_KV, BF16), tr, tr if vw_transposed else tok(D_KV, BF16)]
    if keep_f32_kv:
        outs += [tok(D_KV, F32)] * 4
    return pl.pallas_call(
        functools.partial(_inproj_kernel, rope=rope, vw_transposed=vw_transposed, keep_f32_kv=keep_f32_kv),
        grid=(n_tok // tm,),
        in_specs=in_specs,
        out_specs=[o[1] for o in outs],
        out_shape=[o[0] for o in outs],
        scratch_shapes=[pltpu.VMEM((LANE_SLABS, tm, LANES), F32)],
        compiler_params=_cparams(1),
        name="in_projection",
    )(*args)


def _zoh(lr, li, ls):
    dt = jnp.exp(ls)
    mag = jnp.exp(lr * dt)
    ar, ai = mag * jnp.cos(li * dt), mag * jnp.sin(li * dt)
    den = lr * lr + li * li
    fr = ((ar - 1.0) * lr + ai * li) / den
    fi = (ai * lr - (ar - 1.0) * li) / den
    return ar, ai, fr, fi


def _powers(ar, ai, n):
    out = [(jnp.ones_like(ar), jnp.zeros_like(ar))]
    for _ in range(n):
        pr, pi = out[-1]
        out.append((pr * ar - pi * ai, pr * ai + pi * ar))
    return out


def _s5_operator_kernel(lr_ref, li_ref, ls_ref, btr_ref, bti_ref, cr_ref, ci_ref,
                        top_ref, e0, e1, e2, e3, m0, m1, m2, m3, d0, d1, d2, d3, ext_scr):
    tc, cg, p = S5_CHUNK, SSM_GROUP_CH, SSM_STATE
    ends = ((e0, e1), (e2, e3))
    carries = ((m0, m1), (m2, m3))
    decays = ((d0, d1), (d2, d3))
    ext_scr[...] = jnp.zeros(ext_scr.shape, F32)
    zero_half = jnp.zeros((cg, p), F32)

    def place(out_ref, t, g2, val):
        halves = [val, zero_half] if g2 == 0 else [zero_half, val]
        out_ref[0, t * S5_PAIR + g2 * cg:t * S5_PAIR + (g2 + 1) * cg, :] = jnp.concatenate(halves, axis=1).astype(BF16)

    for d in range(2):
        for g2 in range(2):
            ar, ai, fr, fi = _zoh(lr_ref[d, 0, g2], li_ref[d, 0, g2], ls_ref[d, 0, g2])
            btr, bti = btr_ref[d, 0, g2], bti_ref[d, 0, g2]
            bbr, bbi = fr * btr - fi * bti, fr * bti + fi * btr
            cr, ci = cr_ref[d, 0, g2], ci_ref[d, 0, g2]
            pw = _powers(ar, ai, tc)
            group_rows = slice(g2 * cg, (g2 + 1) * cg)
            cars, cais = [], []
            for j in range(tc + 1):
                pr, pi = pw[j]
                car, cai = cr * pr - ci * pi, cr * pi + ci * pr
                if j >= 1:
                    t = j - 1 if d == 0 else tc - j
                    place(carries[d][0], t, g2, car)
                    place(carries[d][1], t, g2, -cai)
                if j == tc:
                    break
                t = tc - 1 - j if d == 0 else j
                place(ends[d][0], t, g2, pr * bbr - pi * bbi)
                place(ends[d][1], t, g2, pr * bbi + pi * bbr)
                cars.append(car)
                cais.append(cai)
            kt_all = (lax.dot_general(bbr, jnp.concatenate(cars, axis=0), NT_DIMS, precision=lax.Precision.HIGHEST,
                                      preferred_element_type=F32)
                      - lax.dot_general(bbi, jnp.concatenate(cais, axis=0), NT_DIMS, precision=lax.Precision.HIGHEST,
                                        preferred_element_type=F32))
            for j in range(tc):
                kt = kt_all[:, j * cg:(j + 1) * cg]
                slot = tc - 1 + j if d == 0 else tc - 1 - j
                lanes = slice(slot * S5_PAIR + g2 * cg, slot * S5_PAIR + (g2 + 1) * cg)
                if d == 1 and j == 0:
                    ext_scr[group_rows, lanes] = ext_scr[group_rows, lanes] + kt
                else:
                    ext_scr[group_rows, lanes] = kt
            for part in range(2):
                decays[d][part][:, g2 * p:(g2 + 1) * p] = pw[tc][part]
    for t in range(tc):
        for g2 in range(2):
            rows = slice(t * S5_PAIR + g2 * cg, t * S5_PAIR + (g2 + 1) * cg)
            window = slice((tc - 1 - t) * S5_PAIR, (tc - 1 - t) * S5_PAIR + S5_ROW)
            top_ref[0, rows, :] = ext_scr[g2 * cg:(g2 + 1) * cg, window].astype(BF16)


def _s5_operators(lam_re, lam_im, log_step, b_re, b_im, c_re, c_im):
    g, p, cg = SSM_GROUPS, SSM_STATE, SSM_GROUP_CH
    ls = jnp.broadcast_to(log_step[:, :, None], (2, g, p))

    def rows(v):
        return v.reshape(2, S5_NPAIR, 2, 1, p)

    def mat(v):
        return v.reshape(2, S5_NPAIR, 2, cg, p)

    def spec(*tail):
        return pl.BlockSpec((2, 1, 2) + tail, lambda n: (0, n, 0, 0, 0))

    state_op = pl.BlockSpec((1, S5_ROW, S5_PSTATE), lambda n: (n, 0, 0))
    out = pl.pallas_call(
        _s5_operator_kernel,
        grid=(S5_NPAIR,),
        in_specs=[spec(1, p)] * 3 + [spec(cg, p)] * 4,
        out_specs=([pl.BlockSpec((1, S5_ROW, S5_ROW), lambda n: (n, 0, 0))] + [state_op] * 8
                   + [pl.BlockSpec((1, S5_PSTATE), lambda n: (0, n))] * 4),
        out_shape=([jax.ShapeDtypeStruct((S5_NPAIR, S5_ROW, S5_ROW), BF16)]
                   + [jax.ShapeDtypeStruct((S5_NPAIR, S5_ROW, S5_PSTATE), BF16)] * 8
                   + [jax.ShapeDtypeStruct((1, N_STATE), F32)] * 4),
        scratch_shapes=[pltpu.VMEM((S5_PAIR, 2 * S5_ROW), F32)],
        compiler_params=_cparams(1),
        name="s5_operators",
    )(rows(lam_re), rows(lam_im), rows(ls),
      mat(jnp.swapaxes(b_re, -1, -2)), mat(jnp.swapaxes(b_im, -1, -2)), mat(c_re), mat(c_im))
    return out[0], out[1:5], out[5:9], out[9:13]


def _s5_kernel(u_ref, top_ref, d_ref, e0, e1, e2, e3, m0, m1, m2, m3, a0, a1, a2, a3, i0, i1, i2, i3,
               y_ref, f0, f1, f2, f3, se0, se1, se2, se3, st0, st1, st2, st3, *, nb, nc):
    ends, carries, decays, inits, finals = (e0, e1, e2, e3), (m0, m1, m2, m3), (a0, a1, a2, a3), (i0, i1, i2, i3), (f0, f1, f2, f3)
    se, st = (se0, se1, se2, se3), (st0, st1, st2, st3)
    for p in range(S5_PAIR_BLOCK):
        lanes = slice(p * S5_PSTATE, (p + 1) * S5_PSTATE)
        u = u_ref[p].astype(BF16)
        for k in range(4):
            se[k][:, lanes] = _dot(u, ends[k][p])

    a_fr, a_fi, a_br, a_bi = [a[...] for a in decays]
    for b in range(nb):
        base = b * nc

        def body(c, carry, base=base):
            fr, fi, br, bi = carry
            rc, rb = pl.ds(base + c, 1), pl.ds(base + nc - 1 - c, 1)
            st0[rc, :] = fr
            st1[rc, :] = fi
            st2[rb, :] = br
            st3[rb, :] = bi
            nfr = a_fr * fr - a_fi * fi + se0[rc, :]
            nfi = a_fr * fi + a_fi * fr + se1[rc, :]
            nbr = a_br * br - a_bi * bi + se2[rb, :]
            nbi = a_br * bi + a_bi * br + se3[rb, :]
            return nfr, nfi, nbr, nbi

        last = lax.fori_loop(0, nc, body, tuple(i_ref[b] for i_ref in inits))
        for f_ref, val in zip(finals, last):
            f_ref[b] = val

    for p in range(S5_PAIR_BLOCK):
        lanes = slice(p * S5_PSTATE, (p + 1) * S5_PSTATE)
        u = u_ref[p]
        y = _dot(u.astype(BF16), top_ref[p]) + d_ref[p] * u
        for k in range(4):
            y = y + lax.dot_general(st[k][:, lanes].astype(BF16), carries[k][p], NT_DIMS, preferred_element_type=F32)
        y_ref[p] = jax.nn.gelu(y)


def _s5_branch(u_rows, ops, d_skip, s0, nb, seq_len):
    top, ends, carries, decay = ops
    nc = seq_len // S5_CHUNK
    n_rows = nc * nb
    pg = S5_PAIR_BLOCK
    lanes = pg * S5_PSTATE
    d_rows = jnp.tile(d_skip.reshape(S5_NPAIR, 1, S5_PAIR), (1, S5_CHUNK, 1)).reshape(S5_NPAIR, 1, S5_ROW)
    rows_spec = pl.BlockSpec((pg, n_rows, S5_ROW), lambda n: (n, 0, 0))
    op_spec = pl.BlockSpec((pg, S5_ROW, S5_PSTATE), lambda n: (n, 0, 0))
    state_spec = pl.BlockSpec((nb, 1, lanes), lambda n: (0, 0, n))
    out = pl.pallas_call(
        functools.partial(_s5_kernel, nb=nb, nc=nc),
        grid=(S5_NPAIR // pg,),
        in_specs=[rows_spec, pl.BlockSpec((pg, S5_ROW, S5_ROW), lambda n: (n, 0, 0)),
                  pl.BlockSpec((pg, 1, S5_ROW), lambda n: (n, 0, 0))]
                 + [op_spec] * 8 + [pl.BlockSpec((1, lanes), lambda n: (0, n))] * 4 + [state_spec] * 4,
        out_specs=[rows_spec] + [state_spec] * 4,
        out_shape=[jax.ShapeDtypeStruct(u_rows.shape, F32)] + [jax.ShapeDtypeStruct((nb, 1, N_STATE), F32)] * 4,
        scratch_shapes=[pltpu.VMEM((n_rows, lanes), F32)] * 8,
        compiler_params=_cparams(1),
        name="s5_chunked",
    )(u_rows, top, d_rows, *ends, *carries, *decay, *[s.reshape(nb, 1, N_STATE) for s in s0])
    return out[0], [s.reshape(nb, N_STATE) for s in out[1:]]


def _stack_group_queries(q_ref, h, tq):
    lane = lax.broadcasted_iota(jnp.int32, (tq, LANES), 1)
    keep = (lane >= h * HEAD_DIM) & (lane < (h + 1) * HEAD_DIM)
    zero = jnp.zeros((tq, LANES), BF16)
    return jnp.concatenate([jnp.where(keep, q_ref[:, j * LANES:(j + 1) * LANES], zero) for j in range(GRP)], axis=0)


def _sink_lanes(sink_ref, h, tq):
    return jnp.concatenate([jnp.full((1, tq), sink_ref[h * GRP + j] * LOG2E, F32) for j in range(GRP)], axis=1)


def _values_with_ones(vt):
    return jnp.concatenate([vt, jnp.ones((V_ROWS - HEAD_DIM, vt.shape[1]), BF16)], axis=0)


def _store_heads(o_ref, h, o_t, tq):
    for j in range(GRP):
        head = h * GRP + j
        o_ref[head * HEAD_DIM:(head + 1) * HEAD_DIM, :] = o_t[:, j * tq:(j + 1) * tq].astype(o_ref.dtype)


def _transposed_out(nb, seq_len, tq, n_seq=1):
    return (pl.BlockSpec((n_seq, D_ATT, tq), lambda b, i: (b, 0, i)), jax.ShapeDtypeStruct((nb, D_ATT, seq_len), BF16))


def _attn_full_kernel(*refs, tq, tk, n_seq, use_sink, has_cache):
    refs = list(refs)
    sink_ref = refs.pop(0) if use_sink else None
    q_ref, k_ref, vt_ref = refs[:3]
    kc_ref, vtc_ref = refs[3:5] if has_cache else (None, None)
    o_ref, m_ref, acc_ref, sa_ref, sb_ref = refs[-5:]
    seq_len = k_ref.shape[0] // n_seq
    n_own = seq_len // tk
    n_chunks = n_own + (kc_ref.shape[1] // tk if has_cache else 0)
    rows = GRP * tq
    qs = [[_stack_group_queries(q_ref.at[s * tq:(s + 1) * tq], h, tq) for h in range(N_KV)] for s in range(n_seq)]
    for u in range(n_seq * N_KV):
        if use_sink:
            m_ref[u] = _sink_lanes(sink_ref, u % N_KV, tq)
            acc_ref[u] = jnp.concatenate([jnp.zeros((HEAD_DIM, rows), F32), jnp.ones((V_ROWS - HEAD_DIM, rows), F32)], axis=0)
        else:
            m_ref[u] = jnp.full((1, rows), NEG_INF, F32)
            acc_ref[u] = jnp.zeros((V_ROWS, rows), F32)

    def keys(s, c):
        if c < n_own:
            return k_ref[s * seq_len + c * tk:s * seq_len + (c + 1) * tk, :]
        return kc_ref[0, (c - n_own) * tk:(c - n_own + 1) * tk, :]

    def values_t(s, c, h):
        hd = slice(h * HEAD_DIM, (h + 1) * HEAD_DIM)
        if c < n_own:
            return vt_ref[hd, s * seq_len + c * tk:s * seq_len + (c + 1) * tk]
        return vtc_ref[0, hd, (c - n_own) * tk:(c - n_own + 1) * tk]

    def scores(s, c, s_ref):
        kc = keys(s, c)
        for h in range(N_KV):
            s_ref[h] = lax.dot_general(kc, qs[s][h], NT_DIMS, preferred_element_type=F32)

    def consume(s, c, s_ref):
        for h in range(N_KV):
            u = s * N_KV + h
            sc = s_ref[h]
            m_old = m_ref[u]
            m_new = jnp.maximum(m_old, jnp.max(sc, axis=0, keepdims=True))
            p = jnp.exp2(sc - m_new).astype(BF16)
            alpha = jnp.exp2(m_old - m_new)
            acc_ref[u] = alpha * acc_ref[u] + _dot(_values_with_ones(values_t(s, c, h)), p)
            m_ref[u] = m_new

    items = [(s, c) for s in range(n_seq) for c in range(n_chunks)]
    bufs = (sa_ref, sb_ref)
    scores(*items[0], bufs[0])
    for n, item in enumerate(items):
        if n + 1 < len(items):
            scores(*items[n + 1], bufs[(n + 1) % 2])
        consume(*item, bufs[n % 2])
    for s in range(n_seq):
        for h in range(N_KV):
            acc = acc_ref[s * N_KV + h]
            _store_heads(o_ref.at[s], h, acc[:HEAD_DIM] / acc[HEAD_DIM:HEAD_DIM + 1], tq)


def _attn_band_kernel(sink_ref, q_ref, k_ref, v_ref, kc_ref, vc_ref, *rest, tq, n_tiles):
    bias_refs, (o_ref, sb_ref, sc_ref) = rest[:n_tiles], rest[n_tiles:]
    n_lat = k_ref.shape[0]
    span = tq + 2 * WINDOW
    k_ctx, v_ctx = kc_ref[0], vc_ref[0]

    def tile_start(t):
        i = pl.program_id(1) * n_tiles + t
        return i, pl.multiple_of(jnp.clip(i * tq - WINDOW, 0, n_lat - span), WINDOW)

    def scores(t, h, slot):
        i, start = tile_start(t)
        qs = _stack_group_queries(q_ref.at[t * tq:(t + 1) * tq], h, tq)
        s_band = lax.dot_general(k_ref[pl.ds(start, span), :], qs, NT_DIMS, preferred_element_type=F32)
        sb_ref[slot] = s_band + bias_refs[t][0]
        sc_ref[slot] = lax.dot_general(k_ctx, qs, NT_DIMS, preferred_element_type=F32)

    def consume(t, h, slot):
        _, start = tile_start(t)
        s_band, s_ctx = sb_ref[slot], sc_ref[slot]
        sink = _sink_lanes(sink_ref, h, tq)
        m = jnp.maximum(jnp.maximum(jnp.max(s_band, axis=0, keepdims=True), jnp.max(s_ctx, axis=0, keepdims=True)), sink)
        p_band = jnp.exp2(s_band - m)
        p_ctx = jnp.exp2(s_ctx - m)
        den = jnp.sum(p_band, axis=0, keepdims=True) + jnp.sum(p_ctx, axis=0, keepdims=True) + jnp.exp2(sink - m)
        acc = (lax.dot_general(v_ref[pl.ds(start, span), :], p_band.astype(BF16), TN_DIMS, preferred_element_type=F32)
               + lax.dot_general(v_ctx, p_ctx.astype(BF16), TN_DIMS, preferred_element_type=F32))
        o_t = acc[h * HEAD_DIM:(h + 1) * HEAD_DIM] / den
        for j in range(GRP):
            head = h * GRP + j
            o_ref[0, head * HEAD_DIM:(head + 1) * HEAD_DIM, t * tq:(t + 1) * tq] = o_t[:, j * tq:(j + 1) * tq].astype(o_ref.dtype)

    items = [(t, h) for t in range(n_tiles) for h in range(N_KV)]
    scores(*items[0], 0)
    for n, item in enumerate(items):
        if n + 1 < len(items):
            scores(*items[n + 1], (n + 1) % 2)
        consume(*item, n % 2)


def _attention_full(q, k, vt, sink, nb, seq_len, tq, tk, cache=None, n_seq=1):
    assert n_seq == 1 or (tq == seq_len and cache is None)
    tiles = seq_len // tq
    q_spec = pl.BlockSpec((n_seq * tq, D_ATT), lambda b, i: (b * tiles + i, 0))
    in_specs = [q_spec, pl.BlockSpec((n_seq * seq_len, D_KV), lambda b, i: (b, 0)),
                pl.BlockSpec((D_KV, n_seq * seq_len), lambda b, i: (0, b))]
    args = [q, k, vt]
    if cache is not None:
        n_c = cache[0].shape[1]
        in_specs += [pl.BlockSpec((1, n_c, D_KV), lambda b, i: (b, 0, 0)), pl.BlockSpec((1, D_KV, n_c), lambda b, i: (b, 0, 0))]
        args += list(cache)
    if sink is not None:
        in_specs, args = [pl.BlockSpec(memory_space=pltpu.SMEM)] + in_specs, [sink] + args
    rows = GRP * tq
    out_spec, out_shape = _transposed_out(nb, seq_len, tq, n_seq)
    return pl.pallas_call(
        functools.partial(_attn_full_kernel, tq=tq, tk=tk, n_seq=n_seq, use_sink=sink is not None, has_cache=cache is not None),
        grid=(nb // n_seq, tiles),
        in_specs=in_specs,
        out_specs=out_spec,
        out_shape=out_shape,
        scratch_shapes=[pltpu.VMEM((n_seq * N_KV, 1, rows), F32), pltpu.VMEM((n_seq * N_KV, V_ROWS, rows), F32),
                        pltpu.VMEM((N_KV, tk, rows), F32), pltpu.VMEM((N_KV, tk, rows), F32)],
        compiler_params=_cparams(2),
        name="attention_full",
    )(*args)


def _attention_band(q, k, v, sink, nb, seq_len, tq, cache, n_tiles=4):
    steps = seq_len // (tq * n_tiles)
    tiles = seq_len // tq
    n_c = cache[0].shape[1]
    span = tq + 2 * WINDOW
    key_row = np.arange(span)[:, None]
    query = np.arange(GRP * tq)[None, :] % tq
    bias = jnp.asarray(np.stack([np.where(np.abs(key_row - lead - query) <= WINDOW, 0.0, NEG_INF)
                                 for lead in (0, WINDOW, 2 * WINDOW)]), dtype=F32)

    def bias_spec(t):
        def index(b, i):
            tile = i * n_tiles + t
            return (jnp.where(tile == 0, 0, jnp.where(tile == tiles - 1, 2, 1)), 0, 0)
        return pl.BlockSpec((1, span, GRP * tq), index)
    q_spec = pl.BlockSpec((n_tiles * tq, D_ATT), lambda b, i: (b * steps + i, 0))
    kv_spec = pl.BlockSpec((seq_len, D_KV), lambda b, i: (b, 0))
    c_spec = pl.BlockSpec((1, n_c, D_KV), lambda b, i: (b, 0, 0))
    out_spec, out_shape = _transposed_out(nb, seq_len, n_tiles * tq)
    return pl.pallas_call(
        functools.partial(_attn_band_kernel, tq=tq, n_tiles=n_tiles),
        grid=(nb, steps),
        in_specs=[pl.BlockSpec(memory_space=pltpu.SMEM), q_spec, kv_spec, kv_spec, c_spec, c_spec]
                 + [bias_spec(t) for t in range(n_tiles)],
        out_specs=out_spec,
        out_shape=out_shape,
        scratch_shapes=[pltpu.VMEM((2, tq + 2 * WINDOW, GRP * tq), F32), pltpu.VMEM((2, n_c, GRP * tq), F32)],
        compiler_params=_cparams(2),
        name="attention_band",
    )(sink, q, k, v, *cache, *([bias] * n_tiles))


def _layer_norm(z, g, b):
    mu = jnp.mean(z, axis=-1, keepdims=True)
    zc = z - mu
    var = jnp.mean(zc * zc, axis=-1, keepdims=True)
    return zc * lax.rsqrt(var + LN_EPS) * g + b


def _merge_mlp_kernel(x_ref, g1_ref, sh2_ref, sc2_ref, g2_ref, ya_ref, yw_ref, yg_ref, gate_ref,
                      wglu_ref, wa_ref, ww_ref, wg_ref, wout_ref, ln1g_ref, ln1b_ref,
                      wup_ref, wdn_ref, ln2g_ref, ln2b_ref, o_ref, ya_scr):
    ya = _chunk_rows_to_tokens(ya_ref, ya_scr)
    ya = ya * jax.nn.sigmoid(_dot(ya.astype(BF16), wglu_ref[...]))

    def gate(c):
        return gate_ref[:, c * D_MODEL:(c + 1) * D_MODEL].astype(F32)

    def branch(yt_ref, w_ref):
        outs = [lax.dot_general(yt_ref[s], w_ref[...], TN_DIMS, preferred_element_type=F32) for s in range(yt_ref.shape[0])]
        return outs[0] if len(outs) == 1 else jnp.concatenate(outs, axis=0)

    m = (gate(0) * _dot(ya.astype(BF16), wa_ref[...])
         + gate(1) * branch(yw_ref, ww_ref)
         + gate(2) * branch(yg_ref, wg_ref))
    f = _dot(m.astype(BF16), wout_ref[...])
    x1 = _layer_norm(DEEPNORM_ALPHA * x_ref[...] + g1_ref[0] * f, ln1g_ref[...], ln1b_ref[...])

    h = (x1 * (1.0 + sc2_ref[0]) + sh2_ref[0]).astype(BF16)
    acc = jnp.zeros(x1.shape, F32)
    for c in range(D_FF // FF_TILE):
        up = jnp.maximum(_dot(h, wup_ref[:, c * FF_TILE:(c + 1) * FF_TILE]), 0.0)
        acc = acc + _dot((up * up).astype(BF16), wdn_ref[c * FF_TILE:(c + 1) * FF_TILE, :])
    o_ref[...] = _layer_norm(DEEPNORM_ALPHA * x1 + g2_ref[0] * acc, ln2g_ref[...], ln2b_ref[...])


def _merge_mlp(x, mod_l, row_of_tile, ya_rows, yw_t, yg_t, gates, lw):
    n_tok = x.shape[0]
    tm = ROW_TILE
    seq_len = yw_t.shape[2]
    if seq_len >= tm:
        tiles_per_seq = seq_len // tm
        att = pl.BlockSpec((1, D_ATT, tm), lambda i: (i // tiles_per_seq, 0, i % tiles_per_seq))
    else:
        att = pl.BlockSpec((tm // seq_len, D_ATT, seq_len), lambda i: (i, 0, 0))
    row = lambda w: pl.BlockSpec((tm, w), lambda i: (i, 0))
    vec = _resident((1, D_MODEL))
    return pl.pallas_call(
        _merge_mlp_kernel,
        grid=(n_tok // tm,),
        in_specs=[row(D_MODEL)] + [_mod_spec(c, row_of_tile) for c in (2, 3, 4, 5)]
                 + [pl.BlockSpec((S5_NPAIR, CHUNKS_PER_TILE, S5_ROW), lambda i: (0, i, 0)),
                    att, att, row(3 * D_MODEL),
                    _resident((D_SSM, D_SSM)), _resident((D_SSM, D_MODEL)), _resident((D_ATT, D_MODEL)),
                    _resident((D_ATT, D_MODEL)), _resident((D_MODEL, D_MODEL)), vec, vec,
                    _resident((D_MODEL, D_FF)), _resident((D_FF, D_MODEL)), vec, vec],
        out_specs=row(D_MODEL),
        out_shape=jax.ShapeDtypeStruct((n_tok, D_MODEL), F32),
        scratch_shapes=[pltpu.VMEM((LANE_SLABS, tm, LANES), F32)],
        compiler_params=_cparams(1),
        name="merge_mlp_residual",
    )(x, mod_l, mod_l, mod_l, mod_l, ya_rows, yw_t, yg_t, gates,
      lw['w_glu'], lw['w_br_ssm'], lw['w_br_win'], lw['w_br_glb'], lw['w_out'], lw['ln1_g'], lw['ln1_b'],
      lw['w_up'], lw['w_down'], lw['ln2_g'], lw['ln2_b'])


def _rope_tables(n_tok):
    rows = n_tok // GRID_W
    row = jnp.repeat(jnp.arange(rows, dtype=F32), GRID_W)
    col = jnp.tile(jnp.arange(GRID_W, dtype=F32), rows)
    n_freq = HEAD_DIM // 4
    inv = ROPE_BASE ** (-jnp.arange(n_freq, dtype=F32) / n_freq)
    ang = jnp.concatenate([row[:, None] * inv, col[:, None] * inv], axis=-1)
    cos, sin = jnp.cos(ang), jnp.sin(ang)
    cos_t = jnp.tile(jnp.concatenate([cos, cos], axis=-1), (1, LANES // HEAD_DIM))
    sin_t = jnp.tile(jnp.concatenate([-sin, sin], axis=-1), (1, LANES // HEAD_DIM))
    return cos_t, sin_t


def _block_diag_ones(n):
    idx = np.arange(n) // HEAD_DIM
    return jnp.asarray(idx[:, None] == idx[None, :], dtype=BF16)


def _layer(x, lw, mod_l, row_of_tile, nb, seq_len, ctx, rope_tabs):
    latent = ctx is not None
    u_rows, qw, qg, gates, kw, kg, vg_t, vw, *kv_f32 = _in_projection(
        x, mod_l, row_of_tile, lw['w_in'], lw['qn'], lw['kn'], lw['bdq'], lw['bdk'], rope_tabs, seq_len,
        vw_transposed=not latent, keep_f32_kv=not latent)
    if latent:
        s0 = [ctx[0][:, d, part].reshape(nb, N_STATE) for d in (0, 1) for part in (0, 1)]
    else:
        s0 = [jnp.zeros((nb, N_STATE), F32)] * 4
    ya_rows, s_fin = _s5_branch(u_rows, lw['s5_ops'], lw['d_skip'], s0, nb, seq_len)

    if latent:
        n_ctx = ctx[1].shape[1]
        k_wc, v_wc, k_gc, v_gc = [t.reshape(nb, n_ctx, D_KV).astype(BF16) for t in ctx[1:]]
        yw = _attention_band(qw, kw, vw, lw['sink'], nb, seq_len, 256, (k_wc, v_wc))
        yg = _attention_full(qg, kg, vg_t, None, nb, seq_len, 256, 256, (k_gc, jnp.swapaxes(v_gc, 1, 2)))
        new_ctx = None
    else:
        yw = _attention_full(qw, kw, vw, lw['sink'], nb, seq_len, seq_len, seq_len, n_seq=2)
        yg = _attention_full(qg, kg, vg_t, None, nb, seq_len, seq_len, seq_len, n_seq=2)
        new_ctx = (s_fin,) + tuple(kv_f32)
    x2 = _merge_mlp(x, mod_l, row_of_tile, ya_rows, yw, yg, gates, lw)
    return x2, new_ctx


def kernel(x_prompt, x_sample, state_ssm, cache_k_win, cache_v_win, cache_k_glb, cache_v_glb, c, c_ctx, w_mod, b_mod, w_in, ssm_lam_re, ssm_lam_im, ssm_log_step, ssm_b_re, ssm_b_im, ssm_c_re, ssm_c_im, ssm_d, w_glu, sink_win, q_norm_glb, k_norm_glb, w_br_ssm, w_br_win, w_br_glb, w_out, ln1_g, ln1_b, w_up, w_down, ln2_g, ln2_b):
    n_ctx_b, ctx_len, _ = x_prompt.shape
    n_lat_b, lat_len, _ = x_sample.shape
    assert ctx_len % ROW_TILE == 0 or ROW_TILE % ctx_len == 0
    assert lat_len % ROW_TILE == 0 and (n_ctx_b * ctx_len) % ROW_TILE == 0

    cond8 = jnp.zeros((8, D_MODEL), F32).at[0].set(c_ctx).at[1:1 + n_lat_b].set(c)
    mod = _modulation(cond8, w_mod, b_mod).reshape(DEPTH, 8, 1, 6 * D_MODEL)
    rope_tabs = _rope_tables(lat_len)
    bdq, bdk = _block_diag_ones(D_ATT), _block_diag_ones(D_KV)
    lat_tiles = lat_len // ROW_TILE
    ctx_row = lambda i: 0
    lat_row = lambda i: 1 + i // lat_tiles

    xp = x_prompt.reshape(n_ctx_b * ctx_len, D_MODEL)
    xs = x_sample.reshape(n_lat_b * lat_len, D_MODEL)
    new_ssm, new_kw, new_vw, new_kg, new_vg = [], [], [], [], []
    for l in range(DEPTH):
        lw = dict(
            w_in=w_in[l].astype(BF16),
            qn=jnp.tile(q_norm_glb[l], N_HEADS).reshape(1, D_ATT), kn=jnp.tile(k_norm_glb[l], N_KV).reshape(1, D_KV),
            bdq=bdq, bdk=bdk,
            s5_ops=_s5_operators(ssm_lam_re[l], ssm_lam_im[l], ssm_log_step[l],
                                 ssm_b_re[l], ssm_b_im[l], ssm_c_re[l], ssm_c_im[l]),
            d_skip=ssm_d[l],
            sink=sink_win[l],
            w_glu=w_glu[l].astype(BF16), w_br_ssm=w_br_ssm[l].astype(BF16), w_br_win=w_br_win[l].astype(BF16),
            w_br_glb=w_br_glb[l].astype(BF16), w_out=w_out[l].astype(BF16),
            ln1_g=ln1_g[l].reshape(1, D_MODEL), ln1_b=ln1_b[l].reshape(1, D_MODEL),
            w_up=w_up[l].astype(BF16), w_down=w_down[l].astype(BF16),
            ln2_g=ln2_g[l].reshape(1, D_MODEL), ln2_b=ln2_b[l].reshape(1, D_MODEL),
        )
        xp, (s_fin, kw, vw, kg, vg) = _layer(xp, lw, mod[l], ctx_row, n_ctx_b, ctx_len, None, None)
        new_ssm.append(jnp.stack(s_fin, axis=1).reshape(n_ctx_b, 2, 2, SSM_GROUPS, SSM_STATE))
        for acc, t in ((new_kw, kw), (new_vw, vw), (new_kg, kg), (new_vg, vg)):
            acc.append(t.reshape(n_ctx_b, ctx_len, N_KV, HEAD_DIM))
        ctx = (state_ssm[:, l], cache_k_win[:, l], cache_v_win[:, l], cache_k_glb[:, l], cache_v_glb[:, l])
        xs, _ = _layer(xs, lw, mod[l], lat_row, n_lat_b, lat_len, ctx, rope_tabs)
    return (xp.reshape(x_prompt.shape), xs.reshape(x_sample.shape),
            jnp.stack(new_ssm, axis=1), jnp.stack(new_kw, axis=1), jnp.stack(new_vw, axis=1),
            jnp.stack(new_kg, axis=1), jnp.stack(new_vg, axis=1))
```

```python
import functools

import jax
import jax.numpy as jnp
import numpy as np
from jax import lax
from jax.experimental import pallas as pl
from jax.experimental.pallas import tpu as pltpu

F32 = jnp.float32
BF16 = jnp.bfloat16

D_MODEL = 1024
DEPTH = 2
GRID_W = 64
HEAD_DIM = 64
D_SSM = 512
SSM_GROUP_CH = 16
SSM_GROUPS = 32
SSM_STATE = 64
N_HEADS = 8
N_KV = 2
GRP = N_HEADS // N_KV
D_ATT = N_HEADS * HEAD_DIM
D_KV = N_KV * HEAD_DIM
WINDOW = 128
ROPE_BASE = 10000.0
D_FF = 4 * D_MODEL
LN_EPS = 1e-5
RMS_EPS = 1e-6
ATTN_SCALE = HEAD_DIM ** -0.5
DEEPNORM_ALPHA = (2.0 * DEPTH) ** 0.25
NEG_INF = -1e30
LOG2E = 1.4426950408889634
Q_SCALE = ATTN_SCALE * LOG2E
V_ROWS = HEAD_DIM + 16
Q_SLOT_ORDER = tuple(h for j in range(GRP) for h in (j, GRP + j))
N_IN = D_SSM + 2 * (D_ATT + 2 * D_KV) + 3 * D_MODEL
O_U = 0
O_QW = O_U + D_SSM
O_KW = O_QW + D_ATT
O_VW = O_KW + D_KV
O_QG = O_VW + D_KV
O_KG = O_QG + D_ATT
O_VG = O_KG + D_KV
O_GATE = O_VG + D_KV

S5_CHUNK = 16
S5_PAIR = 2 * SSM_GROUP_CH
S5_NPAIR = SSM_GROUPS // 2
S5_PAIR_BLOCK = 4
S5_SCAN_SEQS = 2
S5_ROW = S5_CHUNK * S5_PAIR
S5_PSTATE = 2 * SSM_STATE
N_STATE = SSM_GROUPS * SSM_STATE

LANES = 128
ROW_TILE = 512
FF_TILE = 1024
MOD_TILE = 1536
ATT_TQ = 256
ATT_TK = 256
BAND_TILES = 4
CTX_SEQS = 2
CHUNKS_PER_TILE = ROW_TILE // S5_CHUNK
LANE_SLABS = D_SSM // LANES
PAIRS_PER_SLAB = LANES // S5_PAIR
VMEM_LIMIT = 56 * 1024 * 1024
NT_DIMS = (((1,), (1,)), ((), ()))
TN_DIMS = (((0,), (0,)), ((), ()))


def _cparams(n_axes):
    return pltpu.CompilerParams(dimension_semantics=("arbitrary",) * n_axes, vmem_limit_bytes=VMEM_LIMIT)


def _resident(shape):
    nd = len(shape)
    return pl.BlockSpec(shape, lambda *_: (0,) * nd, pipeline_mode=pl.Buffered(1))


def _dot(a, b):
    return jnp.dot(a, b, preferred_element_type=F32)


def _mod_kernel(c_ref, w_ref, b_ref, o_ref):
    c = c_ref[...]
    a = (c * jax.nn.sigmoid(c)).astype(BF16)
    o_ref[0] = _dot(a, w_ref[0].astype(BF16)) + b_ref[0]


def _modulation(cond8, w_mod, b_mod):
    tn = MOD_TILE
    n_out = w_mod.shape[-1]
    return pl.pallas_call(
        _mod_kernel,
        grid=(DEPTH, n_out // tn),
        in_specs=[
            pl.BlockSpec((8, D_MODEL), lambda l, n: (0, 0)),
            pl.BlockSpec((1, D_MODEL, tn), lambda l, n: (l, 0, n)),
            pl.BlockSpec((1, 1, tn), lambda l, n: (l, 0, n)),
        ],
        out_specs=pl.BlockSpec((1, 8, tn), lambda l, n: (l, 0, n)),
        out_shape=jax.ShapeDtypeStruct((DEPTH, 8, n_out), F32),
        compiler_params=_cparams(2),
        name="modulation",
    )(cond8, w_mod, b_mod.reshape(DEPTH, 1, n_out))


def _mod_spec(chunk, row_of_tile):
    return pl.BlockSpec((1, 1, D_MODEL), lambda i: (row_of_tile(i), 0, chunk))


def _tokens_to_chunk_rows(u, scr, out_ref):
    n_chunks = u.shape[0] // S5_CHUNK
    for j in range(LANE_SLABS):
        scr[j] = u[:, j * LANES:(j + 1) * LANES]
    pieces = [[None] * S5_CHUNK for _ in range(S5_NPAIR)]
    for j in range(LANE_SLABS):
        for t in range(S5_CHUNK):
            step_rows = scr[j, pl.ds(t, n_chunks, stride=S5_CHUNK), :]
            for m in range(PAIRS_PER_SLAB):
                pieces[j * PAIRS_PER_SLAB + m][t] = step_rows[:, m * S5_PAIR:(m + 1) * S5_PAIR]
    for n in range(S5_NPAIR):
        out_ref[n] = jnp.concatenate(pieces[n], axis=1)


def _chunk_rows_to_tokens(y_ref, scr):
    n_chunks = y_ref.shape[1]
    for j in range(LANE_SLABS):
        for t in range(S5_CHUNK):
            piece = jnp.concatenate(
                [y_ref[j * PAIRS_PER_SLAB + m, :, t * S5_PAIR:(t + 1) * S5_PAIR] for m in range(PAIRS_PER_SLAB)], axis=1)
            scr[j, pl.ds(t, n_chunks, stride=S5_CHUNK), :] = piece
    return jnp.concatenate([scr[j] for j in range(LANE_SLABS)], axis=1)


def _head_rms(x, gain, ones_bd):
    ss = _dot((x * x).astype(BF16), ones_bd)
    return x * lax.rsqrt(ss * (1.0 / HEAD_DIM) + RMS_EPS) * gain


def _rope_chunk(xc, cos_t, sin_t):
    lane = lax.broadcasted_iota(jnp.int32, xc.shape, 1)
    first_half = (lane & (HEAD_DIM - 1)) < (HEAD_DIM // 2)
    partner = jnp.where(first_half, pltpu.roll(xc, LANES - HEAD_DIM // 2, 1), pltpu.roll(xc, HEAD_DIM // 2, 1))
    return xc * cos_t + partner * sin_t


def _inproj_kernel(*refs, rope, vw_transposed, keep_f32_kv):
    n_in = 10 if rope else 8
    x_ref, sh_ref, sc_ref, w_ref, qn_ref, kn_ref, bdq_ref, bdk_ref = refs[:8]
    u_ref, qw_ref, qg_ref, gate_ref, kw_ref, kg_ref, vgt_ref, vw_ref = refs[n_in:n_in + 8]
    f32_refs, u_scr = refs[n_in + 8:-1], refs[-1]
    h = (x_ref[...] * (1.0 + sc_ref[0]) + sh_ref[0]).astype(BF16)

    def proj(lo, width):
        return _dot(h, w_ref[:, lo:lo + width])

    def rotate(x):
        if not rope:
            return x
        chunks = [_rope_chunk(x[:, c * LANES:(c + 1) * LANES], refs[8][...], refs[9][...]) for c in range(x.shape[-1] // LANES)]
        return chunks[0] if len(chunks) == 1 else jnp.concatenate(chunks, axis=1)

    def slot_order(x):
        return jnp.concatenate([x[:, h * HEAD_DIM:(h + 1) * HEAD_DIM] for h in Q_SLOT_ORDER], axis=1)

    _tokens_to_chunk_rows(proj(O_U, D_SSM), u_scr, u_ref)
    qw_ref[...] = slot_order((rotate(proj(O_QW, D_ATT)) * Q_SCALE).astype(BF16))
    qg_ref[...] = slot_order((rotate(_head_rms(proj(O_QG, D_ATT), qn_ref[...], bdq_ref[...])) * Q_SCALE).astype(BF16))
    kw = rotate(proj(O_KW, D_KV))
    kg = rotate(_head_rms(proj(O_KG, D_KV), kn_ref[...], bdk_ref[...]))
    vw, vg = proj(O_VW, D_KV), proj(O_VG, D_KV)
    kw_ref[...] = kw.astype(BF16)
    kg_ref[...] = kg.astype(BF16)
    vgt_ref[...] = vg.T.astype(BF16)
    vw_ref[...] = (vw.T if vw_transposed else vw).astype(BF16)
    if keep_f32_kv:
        for out_ref, val in zip(f32_refs, (kw, vw, kg, vg)):
            out_ref[...] = val
    for c in range(3):
        gate = jax.nn.sigmoid(proj(O_GATE + c * D_MODEL, D_MODEL))
        gate_ref[:, c * D_MODEL:(c + 1) * D_MODEL] = gate.astype(gate_ref.dtype)


def _in_projection(x, mod_l, row_of_tile, w_in, qn, kn, bdq, bdk, rope_tabs, seq_len, vw_transposed, keep_f32_kv):
    n_tok = x.shape[0]
    tm = ROW_TILE
    rope = rope_tabs is not None
    row = lambda w: pl.BlockSpec((tm, w), lambda i: (i, 0))
    col = pl.BlockSpec((D_KV, tm), lambda i: (0, i))
    in_specs = [
        row(D_MODEL),
        _mod_spec(0, row_of_tile), _mod_spec(1, row_of_tile),
        _resident((D_MODEL, N_IN)),
        _resident((1, D_ATT)), _resident((1, D_KV)),
        _resident((D_ATT, D_ATT)), _resident((D_KV, D_KV)),
    ]
    args = [x, mod_l, mod_l, w_in, qn, kn, bdq, bdk]
    if rope:
        tiles_per_seq = seq_len // tm
        tab = pl.BlockSpec((tm, LANES), lambda i: (i % tiles_per_seq, 0))
        in_specs += [tab, tab]
        args += list(rope_tabs)
    tok = lambda w, dt: (jax.ShapeDtypeStruct((n_tok, w), dt), row(w))
    tr = (jax.ShapeDtypeStruct((D_KV, n_tok), BF16), col)
    outs = [(jax.ShapeDtypeStruct((S5_NPAIR, n_tok // S5_CHUNK, S5_ROW), F32),
             pl.BlockSpec((S5_NPAIR, CHUNKS_PER_TILE, S5_ROW), lambda i: (0, i, 0))),
            tok(D_ATT, BF16), tok(D_ATT, BF16), tok(3 * D_MODEL, BF16),
            tok(D_KV, BF16), tok(D_KV, BF16), tr, tr if vw_transposed else tok(D_KV, BF16)]
    if keep_f32_kv:
        outs += [tok(D_KV, F32)] * 4
    return pl.pallas_call(
        functools.partial(_inproj_kernel, rope=rope, vw_transposed=vw_transposed, keep_f32_kv=keep_f32_kv),
        grid=(n_tok // tm,),
        in_specs=in_specs,
        out_specs=[o[1] for o in outs],
        out_shape=[o[0] for o in outs],
        scratch_shapes=[pltpu.VMEM((LANE_SLABS, tm, LANES), F32)],
        compiler_params=_cparams(1),
        name="in_projection",
    )(*args)


def _zoh(lr, li, ls):
    dt = jnp.exp(ls)
    mag = jnp.exp(lr * dt)
    ar, ai = mag * jnp.cos(li * dt), mag * jnp.sin(li * dt)
    den = lr * lr + li * li
    fr = ((ar - 1.0) * lr + ai * li) / den
    fi = (ai * lr - (ar - 1.0) * li) / den
    return ar, ai, fr, fi


def _powers(ar, ai, n):
    out = [(jnp.ones_like(ar), jnp.zeros_like(ar))]
    for _ in range(n):
        pr, pi = out[-1]
        out.append((pr * ar - pi * ai, pr * ai + pi * ar))
    return out


def _s5_operator_kernel(lr_ref, li_ref, ls_ref, btr_ref, bti_ref, cr_ref, ci_ref,
                        top_ref, e0, e1, e2, e3, m0, m1, m2, m3, d0, d1, d2, d3, ext_scr):
    tc, cg, p = S5_CHUNK, SSM_GROUP_CH, SSM_STATE
    ends = ((e0, e1), (e2, e3))
    carries = ((m0, m1), (m2, m3))
    decays = ((d0, d1), (d2, d3))
    ext_scr[...] = jnp.zeros(ext_scr.shape, F32)
    zero_half = jnp.zeros((cg, p), F32)

    def place(out_ref, t, g2, val):
        halves = [val, zero_half] if g2 == 0 else [zero_half, val]
        out_ref[0, t * S5_PAIR + g2 * cg:t * S5_PAIR + (g2 + 1) * cg, :] = jnp.concatenate(halves, axis=1).astype(BF16)

    for d in range(2):
        for g2 in range(2):
            ar, ai, fr, fi = _zoh(lr_ref[d, 0, g2], li_ref[d, 0, g2], ls_ref[d, 0, g2])
            btr, bti = btr_ref[d, 0, g2], bti_ref[d, 0, g2]
            bbr, bbi = fr * btr - fi * bti, fr * bti + fi * btr
            cr, ci = cr_ref[d, 0, g2], ci_ref[d, 0, g2]
            pw = _powers(ar, ai, tc)
            group_rows = slice(g2 * cg, (g2 + 1) * cg)
            cars, cais = [], []
            for j in range(tc + 1):
                pr, pi = pw[j]
                car, cai = cr * pr - ci * pi, cr * pi + ci * pr
                if j >= 1:
                    t = j - 1 if d == 0 else tc - j
                    place(carries[d][0], t, g2, car)
                    place(carries[d][1], t, g2, -cai)
                if j == tc:
                    break
                t = tc - 1 - j if d == 0 else j
                place(ends[d][0], t, g2, pr * bbr - pi * bbi)
                place(ends[d][1], t, g2, pr * bbi + pi * bbr)
                cars.append(car)
                cais.append(cai)
            kt_all = (lax.dot_general(bbr, jnp.concatenate(cars, axis=0), NT_DIMS, precision=lax.Precision.HIGHEST,
                                      preferred_element_type=F32)
                      - lax.dot_general(bbi, jnp.concatenate(cais, axis=0), NT_DIMS, precision=lax.Precision.HIGHEST,
                                        preferred_element_type=F32))
            for j in range(tc):
                kt = kt_all[:, j * cg:(j + 1) * cg]
                slot = tc - 1 + j if d == 0 else tc - 1 - j
                lanes = slice(slot * S5_PAIR + g2 * cg, slot * S5_PAIR + (g2 + 1) * cg)
                if d == 1 and j == 0:
                    ext_scr[group_rows, lanes] = ext_scr[group_rows, lanes] + kt
                else:
                    ext_scr[group_rows, lanes] = kt
            for part in range(2):
                decays[d][part][:, g2 * p:(g2 + 1) * p] = pw[tc][part]
    for t in range(tc):
        for g2 in range(2):
            rows = slice(t * S5_PAIR + g2 * cg, t * S5_PAIR + (g2 + 1) * cg)
            window = slice((tc - 1 - t) * S5_PAIR, (tc - 1 - t) * S5_PAIR + S5_ROW)
            top_ref[0, rows, :] = ext_scr[g2 * cg:(g2 + 1) * cg, window].astype(BF16)


def _s5_operators(lam_re, lam_im, log_step, b_re, b_im, c_re, c_im):
    g, p, cg = SSM_GROUPS, SSM_STATE, SSM_GROUP_CH
    ls = jnp.broadcast_to(log_step[:, :, None], (2, g, p))

    def rows(v):
        return v.reshape(2, S5_NPAIR, 2, 1, p)

    def mat(v):
        return v.reshape(2, S5_NPAIR, 2, cg, p)

    def spec(*tail):
        return pl.BlockSpec((2, 1, 2) + tail, lambda n: (0, n, 0, 0, 0))

    state_op = pl.BlockSpec((1, S5_ROW, S5_PSTATE), lambda n: (n, 0, 0))
    out = pl.pallas_call(
        _s5_operator_kernel,
        grid=(S5_NPAIR,),
        in_specs=[spec(1, p)] * 3 + [spec(cg, p)] * 4,
        out_specs=([pl.BlockSpec((1, S5_ROW, S5_ROW), lambda n: (n, 0, 0))] + [state_op] * 8
                   + [pl.BlockSpec((1, S5_PSTATE), lambda n: (0, n))] * 4),
        out_shape=([jax.ShapeDtypeStruct((S5_NPAIR, S5_ROW, S5_ROW), BF16)]
                   + [jax.ShapeDtypeStruct((S5_NPAIR, S5_ROW, S5_PSTATE), BF16)] * 8
                   + [jax.ShapeDtypeStruct((1, N_STATE), F32)] * 4),
        scratch_shapes=[pltpu.VMEM((S5_PAIR, 2 * S5_ROW), F32)],
        compiler_params=_cparams(1),
        name="s5_operators",
    )(rows(lam_re), rows(lam_im), rows(ls),
      mat(jnp.swapaxes(b_re, -1, -2)), mat(jnp.swapaxes(b_im, -1, -2)), mat(c_re), mat(c_im))
    return out[0], out[1:5], out[5:9], out[9:13]


def _s5_kernel(u_ref, top_ref, d_ref, e0, e1, e2, e3, m0, m1, m2, m3, a0, a1, a2, a3, i0, i1, i2, i3,
               y_ref, f0, f1, f2, f3, se0, se1, se2, se3, st0, st1, st2, st3, *, nb, nc):
    ends, carries, decays, inits, finals = (e0, e1, e2, e3), (m0, m1, m2, m3), (a0, a1, a2, a3), (i0, i1, i2, i3), (f0, f1, f2, f3)
    se, st = (se0, se1, se2, se3), (st0, st1, st2, st3)
    for p in range(S5_PAIR_BLOCK):
        lanes = slice(p * S5_PSTATE, (p + 1) * S5_PSTATE)
        u = u_ref[p].astype(BF16)
        for k in range(4):
            se[k][:, lanes] = _dot(u, ends[k][p])

    a_fr, a_fi, a_br, a_bi = [a[...] for a in decays]
    for b0 in range(0, nb, S5_SCAN_SEQS):
        seqs = range(b0, min(b0 + S5_SCAN_SEQS, nb))

        def body(c, carry, seqs=seqs):
            out = []
            for n, b in enumerate(seqs):
                fr, fi, br, bi = carry[4 * n:4 * n + 4]
                rc, rb = pl.ds(b * nc + c, 1), pl.ds(b * nc + nc - 1 - c, 1)
                st0[rc, :] = fr
                st1[rc, :] = fi
                st2[rb, :] = br
                st3[rb, :] = bi
                out += [a_fr * fr - a_fi * fi + se0[rc, :], a_fr * fi + a_fi * fr + se1[rc, :],
                        a_br * br - a_bi * bi + se2[rb, :], a_br * bi + a_bi * br + se3[rb, :]]
            return tuple(out)

        last = lax.fori_loop(0, nc, body, tuple(i_ref[b] for b in seqs for i_ref in inits))
        for n, b in enumerate(seqs):
            for f_ref, val in zip(finals, last[4 * n:4 * n + 4]):
                f_ref[b] = val

    for p in range(S5_PAIR_BLOCK):
        lanes = slice(p * S5_PSTATE, (p + 1) * S5_PSTATE)
        u = u_ref[p]
        y = _dot(u.astype(BF16), top_ref[p]) + d_ref[p] * u
        for k in range(4):
            y = y + lax.dot_general(st[k][:, lanes].astype(BF16), carries[k][p], NT_DIMS, preferred_element_type=F32)
        y_ref[p] = jax.nn.gelu(y)


def _s5_branch(u_rows, ops, d_skip, s0, nb, seq_len):
    top, ends, carries, decay = ops
    nc = seq_len // S5_CHUNK
    n_rows = nc * nb
    pg = S5_PAIR_BLOCK
    lanes = pg * S5_PSTATE
    d_rows = jnp.tile(d_skip.reshape(S5_NPAIR, 1, S5_PAIR), (1, S5_CHUNK, 1)).reshape(S5_NPAIR, 1, S5_ROW)
    rows_spec = pl.BlockSpec((pg, n_rows, S5_ROW), lambda n: (n, 0, 0))
    op_spec = pl.BlockSpec((pg, S5_ROW, S5_PSTATE), lambda n: (n, 0, 0))
    state_spec = pl.BlockSpec((nb, 1, lanes), lambda n: (0, 0, n))
    out = pl.pallas_call(
        functools.partial(_s5_kernel, nb=nb, nc=nc),
        grid=(S5_NPAIR // pg,),
        in_specs=[rows_spec, pl.BlockSpec((pg, S5_ROW, S5_ROW), lambda n: (n, 0, 0)),
                  pl.BlockSpec((pg, 1, S5_ROW), lambda n: (n, 0, 0))]
                 + [op_spec] * 8 + [pl.BlockSpec((1, lanes), lambda n: (0, n))] * 4 + [state_spec] * 4,
        out_specs=[rows_spec] + [state_spec] * 4,
        out_shape=[jax.ShapeDtypeStruct(u_rows.shape, F32)] + [jax.ShapeDtypeStruct((nb, 1, N_STATE), F32)] * 4,
        scratch_shapes=[pltpu.VMEM((n_rows, lanes), F32)] * 8,
        compiler_params=_cparams(1),
        name="s5_chunked",
    )(u_rows, top, d_rows, *ends, *carries, *decay, *[s.reshape(nb, 1, N_STATE) for s in s0])
    return out[0], [s.reshape(nb, N_STATE) for s in out[1:]]


def _stack_group_queries(q_ref, h, tq):
    lane = lax.broadcasted_iota(jnp.int32, (tq, LANES), 1)
    keep = (lane >= h * HEAD_DIM) & (lane < (h + 1) * HEAD_DIM)
    zero = jnp.zeros((tq, LANES), BF16)
    return jnp.concatenate([jnp.where(keep, q_ref[:, j * LANES:(j + 1) * LANES], zero) for j in range(GRP)], axis=0)


def _sink_lanes(sink_ref, h, tq):
    return jnp.concatenate([jnp.full((1, tq), sink_ref[h * GRP + j] * LOG2E, F32) for j in range(GRP)], axis=1)


def _values_with_ones(vt):
    return jnp.concatenate([vt, jnp.ones((V_ROWS - HEAD_DIM, vt.shape[1]), BF16)], axis=0)


def _store_heads(o_ref, h, o_t, tq):
    for j in range(GRP):
        head = h * GRP + j
        o_ref[head * HEAD_DIM:(head + 1) * HEAD_DIM, :] = o_t[:, j * tq:(j + 1) * tq].astype(o_ref.dtype)


def _transposed_out(nb, seq_len, tq, n_seq=1):
    return (pl.BlockSpec((n_seq, D_ATT, tq), lambda b, i: (b, 0, i)), jax.ShapeDtypeStruct((nb, D_ATT, seq_len), BF16))


def _attn_full_kernel(*refs, tq, tk, n_seq, use_sink, has_cache):
    refs = list(refs)
    sink_ref = refs.pop(0) if use_sink else None
    q_ref, k_ref, vt_ref = refs[:3]
    kc_ref, vtc_ref = refs[3:5] if has_cache else (None, None)
    o_ref, m_ref, acc_ref, sa_ref, sb_ref = refs[-5:]
    seq_len = k_ref.shape[0] // n_seq
    n_own = seq_len // tk
    n_chunks = n_own + (kc_ref.shape[1] // tk if has_cache else 0)
    rows = GRP * tq
    qs = [[_stack_group_queries(q_ref.at[s * tq:(s + 1) * tq], h, tq) for h in range(N_KV)] for s in range(n_seq)]
    for u in range(n_seq * N_KV):
        if use_sink:
            m_ref[u] = _sink_lanes(sink_ref, u % N_KV, tq)
            acc_ref[u] = jnp.concatenate([jnp.zeros((HEAD_DIM, rows), F32), jnp.ones((V_ROWS - HEAD_DIM, rows), F32)], axis=0)
        else:
            m_ref[u] = jnp.full((1, rows), NEG_INF, F32)
            acc_ref[u] = jnp.zeros((V_ROWS, rows), F32)

    def keys(s, c):
        if c < n_own:
            return k_ref[s * seq_len + c * tk:s * seq_len + (c + 1) * tk, :]
        return kc_ref[0, (c - n_own) * tk:(c - n_own + 1) * tk, :]

    def values_t(s, c, h):
        hd = slice(h * HEAD_DIM, (h + 1) * HEAD_DIM)
        if c < n_own:
            return vt_ref[hd, s * seq_len + c * tk:s * seq_len + (c + 1) * tk]
        return vtc_ref[0, hd, (c - n_own) * tk:(c - n_own + 1) * tk]

    def scores(s, c, s_ref):
        kc = keys(s, c)
        for h in range(N_KV):
            s_ref[h] = lax.dot_general(kc, qs[s][h], NT_DIMS, preferred_element_type=F32)

    def consume(s, c, s_ref):
        for h in range(N_KV):
            u = s * N_KV + h
            sc = s_ref[h]
            m_old = m_ref[u]
            m_new = jnp.maximum(m_old, jnp.max(sc, axis=0, keepdims=True))
            p = jnp.exp2(sc - m_new).astype(BF16)
            alpha = jnp.exp2(m_old - m_new)
            acc_ref[u] = alpha * acc_ref[u] + _dot(_values_with_ones(values_t(s, c, h)), p)
            m_ref[u] = m_new

    items = [(s, c) for s in range(n_seq) for c in range(n_chunks)]
    bufs = (sa_ref, sb_ref)
    scores(*items[0], bufs[0])
    for n, item in enumerate(items):
        if n + 1 < len(items):
            scores(*items[n + 1], bufs[(n + 1) % 2])
        consume(*item, bufs[n % 2])
    for s in range(n_seq):
        for h in range(N_KV):
            acc = acc_ref[s * N_KV + h]
            _store_heads(o_ref.at[s], h, acc[:HEAD_DIM] / acc[HEAD_DIM:HEAD_DIM + 1], tq)


def _attn_band_kernel(sink_ref, q_ref, k_ref, v_ref, kc_ref, vc_ref, *rest, tq, n_tiles):
    bias_refs, (o_ref, sb_ref, sc_ref) = rest[:n_tiles], rest[n_tiles:]
    n_lat = k_ref.shape[0]
    span = tq + 2 * WINDOW
    k_ctx, v_ctx = kc_ref[0], vc_ref[0]

    def tile_start(t):
        i = pl.program_id(1) * n_tiles + t
        return i, pl.multiple_of(jnp.clip(i * tq - WINDOW, 0, n_lat - span), WINDOW)

    def scores(t, h, slot):
        i, start = tile_start(t)
        qs = _stack_group_queries(q_ref.at[t * tq:(t + 1) * tq], h, tq)
        s_band = lax.dot_general(k_ref[pl.ds(start, span), :], qs, NT_DIMS, preferred_element_type=F32)
        sb_ref[slot] = s_band + bias_refs[t][0]
        sc_ref[slot] = lax.dot_general(k_ctx, qs, NT_DIMS, preferred_element_type=F32)

    def consume(t, h, slot):
        _, start = tile_start(t)
        s_band, s_ctx = sb_ref[slot], sc_ref[slot]
        sink = _sink_lanes(sink_ref, h, tq)
        m = jnp.maximum(jnp.maximum(jnp.max(s_band, axis=0, keepdims=True), jnp.max(s_ctx, axis=0, keepdims=True)), sink)
        p_band = jnp.exp2(s_band - m)
        p_ctx = jnp.exp2(s_ctx - m)
        den = jnp.sum(p_band, axis=0, keepdims=True) + jnp.sum(p_ctx, axis=0, keepdims=True) + jnp.exp2(sink - m)
        acc = (lax.dot_general(v_ref[pl.ds(start, span), :], p_band.astype(BF16), TN_DIMS, preferred_element_type=F32)
               + lax.dot_general(v_ctx, p_ctx.astype(BF16), TN_DIMS, preferred_element_type=F32))
        o_t = acc[h * HEAD_DIM:(h + 1) * HEAD_DIM] / den
        for j in range(GRP):
            head = h * GRP + j
            o_ref[0, head * HEAD_DIM:(head + 1) * HEAD_DIM, t * tq:(t + 1) * tq] = o_t[:, j * tq:(j + 1) * tq].astype(o_ref.dtype)

    items = [(t, h) for t in range(n_tiles) for h in range(N_KV)]
    scores(*items[0], 0)
    for n, item in enumerate(items):
        if n + 1 < len(items):
            scores(*items[n + 1], (n + 1) % 2)
        consume(*item, n % 2)


def _attention_full(q, k, vt, sink, nb, seq_len, tq, tk, cache=None, n_seq=1):
    assert n_seq == 1 or (tq == seq_len and cache is None)
    tiles = seq_len // tq
    q_spec = pl.BlockSpec((n_seq * tq, D_ATT), lambda b, i: (b * tiles + i, 0))
    in_specs = [q_spec, pl.BlockSpec((n_seq * seq_len, D_KV), lambda b, i: (b, 0)),
                pl.BlockSpec((D_KV, n_seq * seq_len), lambda b, i: (0, b))]
    args = [q, k, vt]
    if cache is not None:
        n_c = cache[0].shape[1]
        in_specs += [pl.BlockSpec((1, n_c, D_KV), lambda b, i: (b, 0, 0)), pl.BlockSpec((1, D_KV, n_c), lambda b, i: (b, 0, 0))]
        args += list(cache)
    if sink is not None:
        in_specs, args = [pl.BlockSpec(memory_space=pltpu.SMEM)] + in_specs, [sink] + args
    rows = GRP * tq
    out_spec, out_shape = _transposed_out(nb, seq_len, tq, n_seq)
    return pl.pallas_call(
        functools.partial(_attn_full_kernel, tq=tq, tk=tk, n_seq=n_seq, use_sink=sink is not None, has_cache=cache is not None),
        grid=(nb // n_seq, tiles),
        in_specs=in_specs,
        out_specs=out_spec,
        out_shape=out_shape,
        scratch_shapes=[pltpu.VMEM((n_seq * N_KV, 1, rows), F32), pltpu.VMEM((n_seq * N_KV, V_ROWS, rows), F32),
                        pltpu.VMEM((N_KV, tk, rows), F32), pltpu.VMEM((N_KV, tk, rows), F32)],
        compiler_params=_cparams(2),
        name="attention_full",
    )(*args)


def _attention_band(q, k, v, sink, nb, seq_len, tq, cache, n_tiles=BAND_TILES):
    steps = seq_len // (tq * n_tiles)
    tiles = seq_len // tq
    n_c = cache[0].shape[1]
    span = tq + 2 * WINDOW
    key_row = np.arange(span)[:, None]
    query = np.arange(GRP * tq)[None, :] % tq
    bias = jnp.asarray(np.stack([np.where(np.abs(key_row - lead - query) <= WINDOW, 0.0, NEG_INF)
                                 for lead in (0, WINDOW, 2 * WINDOW)]), dtype=F32)

    def bias_spec(t):
        def index(b, i):
            tile = i * n_tiles + t
            return (jnp.where(tile == 0, 0, jnp.where(tile == tiles - 1, 2, 1)), 0, 0)
        return pl.BlockSpec((1, span, GRP * tq), index)
    q_spec = pl.BlockSpec((n_tiles * tq, D_ATT), lambda b, i: (b * steps + i, 0))
    kv_spec = pl.BlockSpec((seq_len, D_KV), lambda b, i: (b, 0))
    c_spec = pl.BlockSpec((1, n_c, D_KV), lambda b, i: (b, 0, 0))
    out_spec, out_shape = _transposed_out(nb, seq_len, n_tiles * tq)
    return pl.pallas_call(
        functools.partial(_attn_band_kernel, tq=tq, n_tiles=n_tiles),
        grid=(nb, steps),
        in_specs=[pl.BlockSpec(memory_space=pltpu.SMEM), q_spec, kv_spec, kv_spec, c_spec, c_spec]
                 + [bias_spec(t) for t in range(n_tiles)],
        out_specs=out_spec,
        out_shape=out_shape,
        scratch_shapes=[pltpu.VMEM((2, tq + 2 * WINDOW, GRP * tq), F32), pltpu.VMEM((2, n_c, GRP * tq), F32)],
        compiler_params=_cparams(2),
        name="attention_band",
    )(sink, q, k, v, *cache, *([bias] * n_tiles))


def _layer_norm(z, g, b):
    mu = jnp.mean(z, axis=-1, keepdims=True)
    zc = z - mu
    var = jnp.mean(zc * zc, axis=-1, keepdims=True)
    return zc * lax.rsqrt(var + LN_EPS) * g + b


def _merge_mlp_kernel(x_ref, g1_ref, sh2_ref, sc2_ref, g2_ref, ya_ref, yw_ref, yg_ref, gate_ref,
                      wglu_ref, wa_ref, ww_ref, wg_ref, wout_ref, ln1g_ref, ln1b_ref,
                      wup_ref, wdn_ref, ln2g_ref, ln2b_ref, o_ref, ya_scr):
    ya = _chunk_rows_to_tokens(ya_ref, ya_scr)
    ya = ya * jax.nn.sigmoid(_dot(ya.astype(BF16), wglu_ref[...]))

    def gate(c):
        return gate_ref[:, c * D_MODEL:(c + 1) * D_MODEL].astype(F32)

    def branch(yt_ref, w_ref):
        outs = [lax.dot_general(yt_ref[s], w_ref[...], TN_DIMS, preferred_element_type=F32) for s in range(yt_ref.shape[0])]
        return outs[0] if len(outs) == 1 else jnp.concatenate(outs, axis=0)

    m = (gate(0) * _dot(ya.astype(BF16), wa_ref[...])
         + gate(1) * branch(yw_ref, ww_ref)
         + gate(2) * branch(yg_ref, wg_ref))
    f = _dot(m.astype(BF16), wout_ref[...])
    x1 = _layer_norm(DEEPNORM_ALPHA * x_ref[...] + g1_ref[0] * f, ln1g_ref[...], ln1b_ref[...])

    h = (x1 * (1.0 + sc2_ref[0]) + sh2_ref[0]).astype(BF16)
    acc = jnp.zeros(x1.shape, F32)
    for c in range(D_FF // FF_TILE):
        up = jnp.maximum(_dot(h, wup_ref[:, c * FF_TILE:(c + 1) * FF_TILE]), 0.0)
        acc = acc + _dot((up * up).astype(BF16), wdn_ref[c * FF_TILE:(c + 1) * FF_TILE, :])
    o_ref[...] = _layer_norm(DEEPNORM_ALPHA * x1 + g2_ref[0] * acc, ln2g_ref[...], ln2b_ref[...])


def _merge_mlp(x, mod_l, row_of_tile, ya_rows, yw_t, yg_t, gates, lw):
    n_tok = x.shape[0]
    tm = ROW_TILE
    seq_len = yw_t.shape[2]
    if seq_len >= tm:
        tiles_per_seq = seq_len // tm
        att = pl.BlockSpec((1, D_ATT, tm), lambda i: (i // tiles_per_seq, 0, i % tiles_per_seq))
    else:
        att = pl.BlockSpec((tm // seq_len, D_ATT, seq_len), lambda i: (i, 0, 0))
    row = lambda w: pl.BlockSpec((tm, w), lambda i: (i, 0))
    vec = _resident((1, D_MODEL))
    return pl.pallas_call(
        _merge_mlp_kernel,
        grid=(n_tok // tm,),
        in_specs=[row(D_MODEL)] + [_mod_spec(c, row_of_tile) for c in (2, 3, 4, 5)]
                 + [pl.BlockSpec((S5_NPAIR, CHUNKS_PER_TILE, S5_ROW), lambda i: (0, i, 0)),
                    att, att, row(3 * D_MODEL),
                    _resident((D_SSM, D_SSM)), _resident((D_SSM, D_MODEL)), _resident((D_ATT, D_MODEL)),
                    _resident((D_ATT, D_MODEL)), _resident((D_MODEL, D_MODEL)), vec, vec,
                    _resident((D_MODEL, D_FF)), _resident((D_FF, D_MODEL)), vec, vec],
        out_specs=row(D_MODEL),
        out_shape=jax.ShapeDtypeStruct((n_tok, D_MODEL), F32),
        scratch_shapes=[pltpu.VMEM((LANE_SLABS, tm, LANES), F32)],
        compiler_params=_cparams(1),
        name="merge_mlp_residual",
    )(x, mod_l, mod_l, mod_l, mod_l, ya_rows, yw_t, yg_t, gates,
      lw['w_glu'], lw['w_br_ssm'], lw['w_br_win'], lw['w_br_glb'], lw['w_out'], lw['ln1_g'], lw['ln1_b'],
      lw['w_up'], lw['w_down'], lw['ln2_g'], lw['ln2_b'])


def _rope_tables(n_tok):
    rows = n_tok // GRID_W
    row = jnp.repeat(jnp.arange(rows, dtype=F32), GRID_W)
    col = jnp.tile(jnp.arange(GRID_W, dtype=F32), rows)
    n_freq = HEAD_DIM // 4
    inv = ROPE_BASE ** (-jnp.arange(n_freq, dtype=F32) / n_freq)
    ang = jnp.concatenate([row[:, None] * inv, col[:, None] * inv], axis=-1)
    cos, sin = jnp.cos(ang), jnp.sin(ang)
    cos_t = jnp.tile(jnp.concatenate([cos, cos], axis=-1), (1, LANES // HEAD_DIM))
    sin_t = jnp.tile(jnp.concatenate([-sin, sin], axis=-1), (1, LANES // HEAD_DIM))
    return cos_t, sin_t


def _block_diag_ones(n):
    idx = np.arange(n) // HEAD_DIM
    return jnp.asarray(idx[:, None] == idx[None, :], dtype=BF16)


def _layer(x, lw, mod_l, row_of_tile, nb, seq_len, ctx, rope_tabs):
    latent = ctx is not None
    u_rows, qw, qg, gates, kw, kg, vg_t, vw, *kv_f32 = _in_projection(
        x, mod_l, row_of_tile, lw['w_in'], lw['qn'], lw['kn'], lw['bdq'], lw['bdk'], rope_tabs, seq_len,
        vw_transposed=not latent, keep_f32_kv=not latent)
    if latent:
        s0 = [ctx[0][:, d, part].reshape(nb, N_STATE) for d in (0, 1) for part in (0, 1)]
    else:
        s0 = [jnp.zeros((nb, N_STATE), F32)] * 4
    ya_rows, s_fin = _s5_branch(u_rows, lw['s5_ops'], lw['d_skip'], s0, nb, seq_len)

    if latent:
        n_ctx = ctx[1].shape[1]
        k_wc, v_wc, k_gc, v_gc = [t.reshape(nb, n_ctx, D_KV).astype(BF16) for t in ctx[1:]]
        yw = _attention_band(qw, kw, vw, lw['sink'], nb, seq_len, ATT_TQ, (k_wc, v_wc))
        yg = _attention_full(qg, kg, vg_t, None, nb, seq_len, ATT_TQ, ATT_TK, (k_gc, jnp.swapaxes(v_gc, 1, 2)))
        new_ctx = None
    else:
        yw = _attention_full(qw, kw, vw, lw['sink'], nb, seq_len, seq_len, seq_len, n_seq=CTX_SEQS)
        yg = _attention_full(qg, kg, vg_t, None, nb, seq_len, seq_len, seq_len, n_seq=CTX_SEQS)
        new_ctx = (s_fin,) + tuple(kv_f32)
    x2 = _merge_mlp(x, mod_l, row_of_tile, ya_rows, yw, yg, gates, lw)
    return x2, new_ctx


def kernel(x_prompt, x_sample, state_ssm, cache_k_win, cache_v_win, cache_k_glb, cache_v_glb, c, c_ctx, w_mod, b_mod, w_in, ssm_lam_re, ssm_lam_im, ssm_log_step, ssm_b_re, ssm_b_im, ssm_c_re, ssm_c_im, ssm_d, w_glu, sink_win, q_norm_glb, k_norm_glb, w_br_ssm, w_br_win, w_br_glb, w_out, ln1_g, ln1_b, w_up, w_down, ln2_g, ln2_b):
    n_ctx_b, ctx_len, _ = x_prompt.shape
    n_lat_b, lat_len, _ = x_sample.shape
    assert ctx_len % ROW_TILE == 0 or ROW_TILE % ctx_len == 0
    assert lat_len % ROW_TILE == 0 and (n_ctx_b * ctx_len) % ROW_TILE == 0

    cond8 = jnp.zeros((8, D_MODEL), F32).at[0].set(c_ctx).at[1:1 + n_lat_b].set(c)
    mod = _modulation(cond8, w_mod, b_mod).reshape(DEPTH, 8, 1, 6 * D_MODEL)
    rope_tabs = _rope_tables(lat_len)
    bdq, bdk = _block_diag_ones(D_ATT), _block_diag_ones(D_KV)
    lat_tiles = lat_len // ROW_TILE
    ctx_row = lambda i: 0
    lat_row = lambda i: 1 + i // lat_tiles

    xp = x_prompt.reshape(n_ctx_b * ctx_len, D_MODEL)
    xs = x_sample.reshape(n_lat_b * lat_len, D_MODEL)
    new_ssm, new_kw, new_vw, new_kg, new_vg = [], [], [], [], []
    for l in range(DEPTH):
        lw = dict(
            w_in=w_in[l].astype(BF16),
            qn=jnp.tile(q_norm_glb[l], N_HEADS).reshape(1, D_ATT), kn=jnp.tile(k_norm_glb[l], N_KV).reshape(1, D_KV),
            bdq=bdq, bdk=bdk,
            s5_ops=_s5_operators(ssm_lam_re[l], ssm_lam_im[l], ssm_log_step[l],
                                 ssm_b_re[l], ssm_b_im[l], ssm_c_re[l], ssm_c_im[l]),
            d_skip=ssm_d[l],
            sink=sink_win[l],
            w_glu=w_glu[l].astype(BF16), w_br_ssm=w_br_ssm[l].astype(BF16), w_br_win=w_br_win[l].astype(BF16),
            w_br_glb=w_br_glb[l].astype(BF16), w_out=w_out[l].astype(BF16),
            ln1_g=ln1_g[l].reshape(1, D_MODEL), ln1_b=ln1_b[l].reshape(1, D_MODEL),
            w_up=w_up[l].astype(BF16), w_down=w_down[l].astype(BF16),
            ln2_g=ln2_g[l].reshape(1, D_MODEL), ln2_b=ln2_b[l].reshape(1, D_MODEL),
        )
        xp, (s_fin, kw, vw, kg, vg) = _layer(xp, lw, mod[l], ctx_row, n_ctx_b, ctx_len, None, None)
        new_ssm.append(jnp.stack(s_fin, axis=1).reshape(n_ctx_b, 2, 2, SSM_GROUPS, SSM_STATE))
        for acc, t in ((new_kw, kw), (new_vw, vw), (new_kg, kg), (new_vg, vg)):
            acc.append(t.reshape(n_ctx_b, ctx_len, N_KV, HEAD_DIM))
        ctx = (state_ssm[:, l], cache_k_win[:, l], cache_v_win[:, l], cache_k_glb[:, l], cache_v_glb[:, l])
        xs, _ = _layer(xs, lw, mod[l], lat_row, n_lat_b, lat_len, ctx, rope_tabs)
    return (xp.reshape(x_prompt.shape), xs.reshape(x_sample.shape),
            jnp.stack(new_ssm, axis=1), jnp.stack(new_kw, axis=1), jnp.stack(new_vw, axis=1),
            jnp.stack(new_kg, axis=1), jnp.stack(new_vg, axis=1))
```
